```python
import math
import jax, jax.numpy as jnp
from jax import lax
import numpy as np

D_MODEL = 2048
BATCH = 2
SEQ = 4096
DEPTH = 1

N_META = 16
EPS = 1e-6
M_HEADS = 4
M_WIDTH = D_MODEL // 2
M_HEAD_DIM = M_WIDTH // M_HEADS
M_GATES = 4 * M_HEADS
M_CHUNK = 64
M_CONV = 3
PAD_LOG_I = -30.0
D_HEADS = 8
D_V_DIM = (D_MODEL // 2) // D_HEADS
D_QK_DIM = D_V_DIM // 2
D_WIDTH = D_HEADS * D_V_DIM
D_QK_WIDTH = D_HEADS * 2 * D_QK_DIM
Q_BLOCK = 128
ROPE_THETA = 500000.0
ROPE_DIM = D_QK_DIM // 4
N_BRANCH = 2
IN_WIDTH = 4 * M_WIDTH + M_GATES + 2 * D_QK_WIDTH + D_WIDTH + N_BRANCH * D_MODEL
FFN_DIM = 11 * D_MODEL // 4
FFN_CONV = 3

kernel_name = "hybrid_mlstm_diffattn_gated_merge"


def rmsnorm(x, g):
    xf = x.astype(jnp.float32)
    r = lax.rsqrt(jnp.mean(xf * xf, axis=-1, keepdims=True) + EPS)
    return (xf * r).astype(x.dtype) * g


def dwconv_centred(x, w):
    xp = jnp.pad(x, ((0, 0), (1, 1), (0, 0)))
    return xp[:, :-2] * w[0] + xp[:, 1:-1] * w[1] + xp[:, 2:] * w[2]


def rope_tables(L, dtype):
    inv_freq = ROPE_THETA ** (-jnp.arange(0, ROPE_DIM, 2, dtype=jnp.float32) / ROPE_DIM)
    ang = jnp.arange(L, dtype=jnp.float32)[:, None] * inv_freq[None, :]
    return jnp.cos(ang).astype(dtype), jnp.sin(ang).astype(dtype)


def rope_partial(x, cos, sin):
    half = ROPE_DIM // 2
    x1 = x[..., :half]
    x2 = x[..., half:ROPE_DIM]
    return jnp.concatenate([x1 * cos - x2 * sin, x2 * cos + x1 * sin, x[..., ROPE_DIM:]], axis=-1)


def mlstm_chunk_step(carry, inp):
    C, n, m = carry
    q, k, v, li, lf = inp
    T = q.shape[-2]
    lower = jnp.tril(jnp.ones((T, T), dtype=bool))
    b = jnp.cumsum(lf, axis=-1)
    g = b[..., -1]
    dmat = jnp.where(lower, b[..., :, None] - b[..., None, :] + li[..., None, :], -jnp.inf)
    inter = b + m[..., None]
    m_t = jnp.maximum(inter, jnp.max(dmat, axis=-1))
    s = jnp.einsum('bhtd,bhsd->bhts', q, k) * jnp.exp(dmat - m_t[..., None])
    a_t = jnp.exp(inter - m_t)
    num = jnp.einsum('bhts,bhsv->bhtv', s, v) + a_t[..., None] * jnp.einsum('bhvd,bhtd->bhtv', C, q)
    den = jnp.sum(s, axis=-1) + a_t * jnp.einsum('bhd,bhtd->bht', n, q)
    h = num / jnp.maximum(jnp.abs(den), jnp.exp(-m_t))[..., None]
    wlog = g[..., None] - b + li
    m_new = jnp.maximum(g + m, jnp.max(wlog, axis=-1))
    decay = jnp.exp(g + m - m_new)
    w = jnp.exp(wlog - m_new[..., None])
    C = decay[..., None, None] * C + jnp.einsum('bhs,bhsv,bhsd->bhvd', w, v, k)
    n = decay[..., None] * n + jnp.einsum('bhs,bhsd->bhd', w, k)
    return (C, n, m_new), h


def mlstm_scan(q, k, v, li, lf):
    B, H, Lp, dk = q.shape
    dv = v.shape[-1]
    nc = Lp // M_CHUNK

    def chunks(t):
        return jnp.moveaxis(t.reshape(t.shape[:2] + (nc, M_CHUNK) + t.shape[3:]), 2, 0)

    init = (jnp.zeros((B, H, dv, dk), jnp.float32), jnp.zeros((B, H, dk), jnp.float32),
            jnp.zeros((B, H), jnp.float32))
    _, h = lax.scan(mlstm_chunk_step, init, (chunks(q), chunks(k), chunks(v), chunks(li), chunks(lf)))
    return jnp.moveaxis(h, 0, 2).reshape(B, H, Lp, dv)


def mlstm_branch(m_qk, m_v, m_o, m_gates, conv_w, gate_bias, norm_g):
    B, L, _ = m_qk.shape
    dtype = m_qk.dtype
    qk = jax.nn.silu(dwconv_centred(m_qk, conv_w))
    q, k = jnp.split(qk, 2, axis=-1)

    def heads(t):
        return t.reshape(B, L, M_HEADS, M_HEAD_DIM).transpose(0, 2, 1, 3).astype(jnp.float32)

    q = heads(q)
    k = heads(k) * (M_HEAD_DIM ** -0.5)
    v = heads(m_v)
    gates = (m_gates + gate_bias).astype(jnp.float32).transpose(0, 2, 1)
    li_f = gates[:, 0:M_HEADS]
    lf_f = jax.nn.log_sigmoid(gates[:, M_HEADS:2 * M_HEADS])
    li_b = gates[:, 2 * M_HEADS:3 * M_HEADS]
    lf_b = jax.nn.log_sigmoid(gates[:, 3 * M_HEADS:4 * M_HEADS])
    pad = (-N_META) % M_CHUNK

    def padl(t, val):
        return jnp.pad(t, ((0, 0), (0, 0), (pad, 0)) + ((0, 0),) * (t.ndim - 3), constant_values=val)

    def flip(t):
        return jnp.flip(t, axis=2)

    qp, kp, vp = padl(q, 0.0), padl(k, 0.0), padl(v, 0.0)
    h_fwd = mlstm_scan(qp, kp, vp, padl(li_f, PAD_LOG_I), padl(lf_f, 0.0))
    h_bwd = flip(mlstm_scan(flip(qp), flip(kp), flip(vp), flip(padl(li_b, PAD_LOG_I)), flip(padl(lf_b, 0.0))))
    hs = (h_fwd + h_bwd)[:, :, pad:]
    mu = jnp.mean(hs, axis=-1, keepdims=True)
    var = jnp.mean(jnp.square(hs - mu), axis=-1, keepdims=True)
    hn = (hs - mu) * lax.rsqrt(var + EPS)
    hn = hn.astype(dtype) * norm_g.reshape(M_HEADS, 1, M_HEAD_DIM)
    hn = hn.transpose(0, 2, 1, 3).reshape(B, L, M_WIDTH)
    return jax.nn.sigmoid(m_o) * hn


def diff_attention_branch(d_q, d_k, d_v, lq1, lk1, lq2, lk2, subln_g, lam_init, cos, sin):
    B, L, _ = d_q.shape
    q = d_q.reshape(B, L, D_HEADS, 2, D_QK_DIM).transpose(0, 2, 3, 1, 4)
    k = d_k.reshape(B, L, D_HEADS, 2, D_QK_DIM).transpose(0, 2, 3, 1, 4)
    v = d_v.reshape(B, L, D_HEADS, D_V_DIM).transpose(0, 2, 1, 3)
    q = rope_partial(q, cos, sin)
    k = rope_partial(k, cos, sin)
    lam = (jnp.exp(jnp.sum(lq1.astype(jnp.float32) * lk1.astype(jnp.float32)))
           - jnp.exp(jnp.sum(lq2.astype(jnp.float32) * lk2.astype(jnp.float32))) + lam_init)
    LQ = -(-L // Q_BLOCK) * Q_BLOCK
    nb = LQ // Q_BLOCK
    qpad = jnp.pad(q, ((0, 0), (0, 0), (0, 0), (0, LQ - L), (0, 0)))
    qb = jnp.moveaxis(qpad.reshape(B, D_HEADS, 2, nb, Q_BLOCK, D_QK_DIM), 3, 0)
    scale = D_QK_DIM ** -0.5

    def attend(qblk):
        s = jnp.einsum('bhmqd,bhmkd->bhmqk', qblk, k).astype(jnp.float32) * scale
        p = jax.nn.softmax(s, axis=-1)
        a = p[:, :, 0] - lam * p[:, :, 1]
        return jnp.einsum('bhqk,bhkd->bhqd', a.astype(v.dtype), v)

    o = lax.map(attend, qb)
    o = jnp.moveaxis(o, 0, 2).reshape(B, D_HEADS, LQ, D_V_DIM)[:, :, :L]
    o = rmsnorm(o, subln_g) * (1.0 - lam_init)
    return o.transpose(0, 2, 1, 3).reshape(B, L, D_WIDTH)


def setup_inputs(seed: int = 0) -> dict:
    key = jax.random.key(seed)
    ks = jax.random.split(key, 24)
    f32 = jnp.float32

    def nrm(k, shape, scale):
        return jax.random.normal(k, shape, f32) * scale

    def gain(k, shape):
        return 1.0 + 0.02 * jax.random.normal(k, shape, f32)

    gk = jax.random.split(ks[5], 4)
    i_f = nrm(gk[0], (DEPTH, M_HEADS), 0.1)
    f_f = jnp.linspace(3.0, 6.0, M_HEADS, dtype=f32)[None] + nrm(gk[1], (DEPTH, M_HEADS), 0.1)
    i_b = nrm(gk[2], (DEPTH, M_HEADS), 0.1)
    f_b = jnp.linspace(3.0, 6.0, M_HEADS, dtype=f32)[None] + nrm(gk[3], (DEPTH, M_HEADS), 0.1)
    return {
        "x": nrm(ks[0], (BATCH, SEQ, D_MODEL), 1.0),
        "meta_tokens": nrm(ks[1], (N_META, D_MODEL), 1.0),
        "norm1_g": gain(ks[2], (DEPTH, D_MODEL)),
        "w_in": nrm(ks[3], (DEPTH, D_MODEL, IN_WIDTH), D_MODEL ** -0.5),
        "mlstm_conv_w": nrm(ks[4], (DEPTH, M_CONV, 2 * M_WIDTH), M_CONV ** -0.5),
        "mlstm_gate_bias": jnp.concatenate([i_f, f_f, i_b, f_b], axis=-1),
        "mlstm_norm_g": gain(ks[6], (DEPTH, M_WIDTH)),
        "lambda_q1": nrm(ks[7], (DEPTH, D_QK_DIM), 0.1),
        "lambda_k1": nrm(ks[8], (DEPTH, D_QK_DIM), 0.1),
        "lambda_q2": nrm(ks[9], (DEPTH, D_QK_DIM), 0.1),
        "lambda_k2": nrm(ks[10], (DEPTH, D_QK_DIM), 0.1),
        "diff_subln_g": gain(ks[11], (DEPTH, D_V_DIM)),
        "w_branch_m": nrm(ks[12], (DEPTH, M_WIDTH, D_MODEL), M_WIDTH ** -0.5),
        "w_branch_d": nrm(ks[13], (DEPTH, D_WIDTH, D_MODEL), D_WIDTH ** -0.5),
        "w_out": nrm(ks[14], (DEPTH, D_MODEL, D_MODEL), D_MODEL ** -0.5),
        "norm2_g": gain(ks[15], (DEPTH, D_MODEL)),
        "w_up": nrm(ks[16], (DEPTH, D_MODEL, 2 * FFN_DIM), D_MODEL ** -0.5),
        "ffn_conv_w": nrm(ks[17], (DEPTH, FFN_CONV, 2 * FFN_DIM), FFN_CONV ** -0.5),
        "w_down": nrm(ks[18], (DEPTH, FFN_DIM, D_MODEL), FFN_DIM ** -0.5),
        "norm_f_g": gain(ks[19], (D_MODEL,)),
    }


def reference(x, meta_tokens, norm1_g, w_in, mlstm_conv_w, mlstm_gate_bias, mlstm_norm_g,
              lambda_q1, lambda_k1, lambda_q2, lambda_k2, diff_subln_g, w_branch_m, w_branch_d,
              w_out, norm2_g, w_up, ffn_conv_w, w_down, norm_f_g):
    B = x.shape[0]
    meta = jnp.broadcast_to(meta_tokens[None].astype(x.dtype), (B, N_META, D_MODEL))
    h = jnp.concatenate([meta, x], axis=1)
    L = h.shape[1]
    cos, sin = rope_tables(L, x.dtype)
    split_at = np.cumsum([2 * M_WIDTH, M_WIDTH, M_WIDTH, M_GATES, D_QK_WIDTH, D_QK_WIDTH,
                          D_WIDTH, D_MODEL]).tolist()
    for layer in range(DEPTH):
        lam_init = 0.8 - 0.6 * math.exp(-0.3 * layer)
        u = rmsnorm(h, norm1_g[layer])
        proj = u @ w_in[layer]
        m_qk, m_v, m_o, m_gates, d_q, d_k, d_v, g_m, g_d = jnp.split(proj, split_at, axis=-1)
        a_out = mlstm_branch(m_qk, m_v, m_o, m_gates, mlstm_conv_w[layer],
                             mlstm_gate_bias[layer], mlstm_norm_g[layer])
        b_out = diff_attention_branch(d_q, d_k, d_v, lambda_q1[layer], lambda_k1[layer],
                                      lambda_q2[layer], lambda_k2[layer], diff_subln_g[layer],
                                      lam_init, cos, sin)
        merged = (jax.nn.sigmoid(g_m) * (a_out @ w_branch_m[layer])
                  + jax.nn.sigmoid(g_d) * (b_out @ w_branch_d[layer]))
        h = h + merged @ w_out[layer]
        up = dwconv_centred(rmsnorm(h, norm2_g[layer]) @ w_up[layer], ffn_conv_w[layer])
        gate, val = jnp.split(up, 2, axis=-1)
        h = h + (jax.nn.silu(gate) * val) @ w_down[layer]
    return rmsnorm(h, norm_f_g)[:, N_META:]
```

```python
import functools
import math

import numpy as np
import jax
import jax.numpy as jnp
from jax import lax
from jax.experimental import pallas as pl
from jax.experimental.pallas import tpu as pltpu

F32 = jnp.float32
BF16 = jnp.bfloat16

D_MODEL = 2048
BATCH = 2
SEQ = 4096
N_META = 16
EPS = 1e-6
M_HEADS = 4
M_WIDTH = 1024
M_HEAD_DIM = 256
D_HEADS = 8
D_V_DIM = 128
D_QK_DIM = 64
D_WIDTH = 1024
ROPE_THETA = 500000.0
ROPE_DIM = 16
FFN_DIM = 5632
LAM_INIT = 0.8 - 0.6 * math.exp(-0.3 * 0)

BLK = 128
LP = SEQ + BLK
NBLK = LP // BLK
META_BLK = SEQ // BLK
R = BATCH * LP
NEG = -1e30

C_MQK, C_MV, C_MO, C_DQ, C_DK, C_DV, C_GM, C_GD = 0, 2048, 3072, 4096, 5120, 6144, 7168, 9216
PROJ_W = 11264

VMEM_LIMIT = 56 * 1024 * 1024


def _cparams(sem):
    return pltpu.CompilerParams(dimension_semantics=sem, vmem_limit_bytes=VMEM_LIMIT)


TM1 = 1056
TN1 = 512
NORM_CH = 96


def _rmsnorm_rows_to(h_ref, g_ref, u_ref, tm):
    def body(i, carry):
        rows = pl.ds(pl.multiple_of(i * NORM_CH, NORM_CH), NORM_CH)
        x = h_ref[rows, :]
        r = lax.rsqrt(jnp.mean(x * x, axis=-1, keepdims=True) + EPS)
        u_ref[rows, :] = ((x * r) * g_ref[...]).astype(BF16)
        return carry
    lax.fori_loop(0, tm // NORM_CH, body, 0)


def _rope_tile(acc, c, s):
    lane = lax.broadcasted_iota(jnp.int32, c.shape, 1)
    first = (lane % D_QK_DIM) < (ROPE_DIM // 2)
    outs = []
    for g in range(acc.shape[1] // 128):
        y = acc[:, g * 128:(g + 1) * 128]
        rot = jnp.where(first, pltpu.roll(y, 128 - ROPE_DIM // 2, 1), pltpu.roll(y, ROPE_DIM // 2, 1))
        outs.append(y * c + rot * s)
    return jnp.concatenate(outs, axis=1)


def _inproj_kernel(h_ref, g_ref, w_ref, cq_ref, sq_ref, ck_ref, sk_ref, o_ref, u_ref):
    n = pl.program_id(1)

    @pl.when(n == 0)
    def _():
        _rmsnorm_rows_to(h_ref, g_ref, u_ref, TM1)

    acc = jnp.dot(u_ref[...], w_ref[...], preferred_element_type=F32)
    q0, k0, v0 = C_DQ // TN1, C_DK // TN1, C_DV // TN1
    is_q = jnp.logical_and(n >= q0, n < k0)
    is_k = jnp.logical_and(n >= k0, n < v0)

    @pl.when(is_q)
    def _():
        o_ref[...] = _rope_tile(acc, cq_ref[...], sq_ref[...]).astype(o_ref.dtype)

    @pl.when(is_k)
    def _():
        o_ref[...] = _rope_tile(acc, ck_ref[...], sk_ref[...]).astype(o_ref.dtype)

    @pl.when(jnp.logical_not(jnp.logical_or(is_q, is_k)))
    def _():
        o_ref[...] = acc.astype(o_ref.dtype)


def _inproj(h, g, w, tabs):
    tiles_per_batch = LP // TM1
    tab_spec = pl.BlockSpec((TM1, 128), lambda m, n: (m % tiles_per_batch, 0))
    return pl.pallas_call(
        _inproj_kernel,
        grid=(R // TM1, PROJ_W // TN1),
        in_specs=[
            pl.BlockSpec((TM1, D_MODEL), lambda m, n: (m, 0)),
            pl.BlockSpec((1, D_MODEL), lambda m, n: (0, 0)),
            pl.BlockSpec((D_MODEL, TN1), lambda m, n: (0, n)),
            tab_spec, tab_spec, tab_spec, tab_spec,
        ],
        out_specs=pl.BlockSpec((TM1, TN1), lambda m, n: (m, n)),
        out_shape=jax.ShapeDtypeStruct((R, PROJ_W), BF16),
        scratch_shapes=[pltpu.VMEM((TM1, D_MODEL), BF16)],
        compiler_params=_cparams(("parallel", "arbitrary")),
        name="inproj",
    )(h, g, w, *tabs)


def _rope_tables():
    pos = np.zeros((LP,), np.float64)
    pos[:SEQ] = np.arange(SEQ) + N_META
    pos[SEQ:SEQ + N_META] = np.arange(N_META)
    half = ROPE_DIM // 2
    inv = ROPE_THETA ** (-np.arange(0, ROPE_DIM, 2, dtype=np.float64) / ROPE_DIM)
    ang = pos[:, None] * inv[None, :]
    cos64 = np.ones((LP, D_QK_DIM)); sin64 = np.zeros((LP, D_QK_DIM))
    cos64[:, :half] = np.cos(ang); cos64[:, half:ROPE_DIM] = np.cos(ang)
    sin64[:, :half] = -np.sin(ang); sin64[:, half:ROPE_DIM] = np.sin(ang)
    cos = np.concatenate([cos64, cos64], axis=1)
    sin = np.concatenate([sin64, sin64], axis=1)
    qs = D_QK_DIM ** -0.5
    return [jnp.asarray(t, F32) for t in (cos * qs, sin * qs, cos, sin)]


TMG = 384
GROWS = 32


def _gates_kernel(h_ref, g_ref, wg_ref, b_ref, o_ref):
    x = h_ref[...]
    r = lax.rsqrt(jnp.mean(x * x, axis=-1, keepdims=True) + EPS)
    u = ((x * r) * g_ref[...]).astype(BF16)
    gt = lax.dot_general(wg_ref[...], u, (((1,), (1,)), ((), ())), preferred_element_type=F32)
    o_ref[...] = gt + b_ref[...]


def _gates(h3, g, wgt, bias):
    return pl.pallas_call(
        _gates_kernel,
        grid=(BATCH, LP // TMG),
        in_specs=[
            pl.BlockSpec((None, TMG, D_MODEL), lambda b, j: (b, j, 0)),
            pl.BlockSpec((1, D_MODEL), lambda b, j: (0, 0)),
            pl.BlockSpec((GROWS, D_MODEL), lambda b, j: (0, 0)),
            pl.BlockSpec((GROWS, 1), lambda b, j: (0, 0)),
        ],
        out_specs=pl.BlockSpec((None, GROWS, TMG), lambda b, j: (b, 0, j)),
        out_shape=jax.ShapeDtypeStruct((BATCH, GROWS, LP), F32),
        compiler_params=_cparams(("parallel", "parallel")),
        name="gates",
    )(h3, g, wgt, bias)


def _lane_scan(x, op, ident, reverse):
    lane = lax.broadcasted_iota(jnp.int32, x.shape, 1)
    k = 1
    while k < BLK:
        if reverse:
            sh = pltpu.roll(x, BLK - k, 1)
            x = op(x, jnp.where(lane < BLK - k, sh, ident))
        else:
            sh = pltpu.roll(x, k, 1)
            x = op(x, jnp.where(lane >= k, sh, ident))
        k *= 2
    return x


def _log_sigmoid(x):
    return jnp.minimum(x, 0.0) - jnp.log1p(jnp.exp(-jnp.abs(x)))


def _scan_kernel(g_ref, arow_ref, gcol_ref, colt_ref):
    colt_ref[...] = jnp.zeros_like(colt_ref)
    lane = lax.broadcasted_iota(jnp.int32, (8, BLK), 1)
    for d in range(2):
        reverse = d == 1
        carry_b = jnp.zeros((8, BLK), F32)
        carry_m = jnp.full((8, BLK), NEG, F32)
        logical = [META_BLK] + list(range(META_BLK))
        order = logical[::-1] if reverse else logical
        edge = 0 if reverse else BLK - 1
        for j in order:
            cols = slice(j * BLK, (j + 1) * BLK)
            li = g_ref[d * 16:d * 16 + 8, cols]
            lf = _log_sigmoid(g_ref[d * 16 + 8:d * 16 + 16, cols])
            if j == META_BLK:
                valid = lane < N_META
                li = jnp.where(valid, li, NEG)
                lf = jnp.where(valid, lf, 0.0)
            bc = carry_b + _lane_scan(lf, jnp.add, 0.0, reverse)
            a = li - bc
            m = jnp.maximum(carry_m, _lane_scan(a, jnp.maximum, NEG, reverse))
            m_last = jnp.broadcast_to(m[:, edge:edge + 1], (8, BLK))
            arow_ref[d * 8:(d + 1) * 8, cols] = a
            base = d * 40
            colt_ref[j, base:base + 8, :] = a
            colt_ref[j, base + 8:base + 16, :] = m
            colt_ref[j, base + 16:base + 24, :] = bc + m
            colt_ref[j, base + 24:base + 32, :] = carry_m
            colt_ref[j, base + 32:base + 40, :] = m_last
            carry_b = jnp.broadcast_to(bc[:, edge:edge + 1], (8, BLK))
            carry_m = m_last
    for j in range(NBLK):
        gcol_ref[j * BLK:(j + 1) * BLK, :] = colt_ref[j].T


def _scans(gt):
    return pl.pallas_call(
        _scan_kernel,
        grid=(BATCH,),
        in_specs=[pl.BlockSpec((None, GROWS, LP), lambda b: (b, 0, 0))],
        out_specs=[
            pl.BlockSpec((None, 16, LP), lambda b: (b, 0, 0)),
            pl.BlockSpec((None, LP, BLK), lambda b: (b, 0, 0)),
        ],
        out_shape=[
            jax.ShapeDtypeStruct((BATCH, 16, LP), F32),
            jax.ShapeDtypeStruct((BATCH, LP, BLK), F32),
        ],
        scratch_shapes=[pltpu.VMEM((NBLK, BLK, BLK), F32)],
        compiler_params=_cparams(("parallel",)),
        name="gate_scans",
    )(gt)


TC_COLS = 256
CONV_CH = 256


def _seq_conv_rows(s_ref, w, lo, n):
    return (s_ref[lo - 1:lo - 1 + n, :] * w[0:1] + s_ref[lo:lo + n, :] * w[1:2]
            + s_ref[lo + 1:lo + 1 + n, :] * w[2:3])


def _qkconv_kernel(x_ref, w_ref, sc_ref, o_ref, s_ref):
    xo = 8 + N_META
    s_ref[0:8, :] = jnp.zeros((8, TC_COLS), F32)
    s_ref[xo + SEQ:xo + SEQ + 8, :] = jnp.zeros((8, TC_COLS), F32)
    s_ref[8:xo, :] = x_ref[SEQ:SEQ + N_META, :].astype(F32)
    for i in range(SEQ // CONV_CH):
        s_ref[xo + i * CONV_CH:xo + (i + 1) * CONV_CH, :] = x_ref[i * CONV_CH:(i + 1) * CONV_CH, :].astype(F32)
    w = w_ref[...]
    sc = sc_ref[...]

    def act(y):
        return (y * jax.nn.sigmoid(y) * sc).astype(o_ref.dtype)

    for i in range(SEQ // CONV_CH):
        o_ref[i * CONV_CH:(i + 1) * CONV_CH, :] = act(_seq_conv_rows(s_ref, w, xo + i * CONV_CH, CONV_CH))
    o_ref[SEQ:SEQ + N_META, :] = act(_seq_conv_rows(s_ref, w, 8, N_META))
    o_ref[SEQ + N_META:LP, :] = jnp.zeros((LP - SEQ - N_META, TC_COLS), o_ref.dtype)


def _qkconv(proj3, w, scale):
    return pl.pallas_call(
        _qkconv_kernel,
        grid=(BATCH, 2 * M_WIDTH // TC_COLS),
        in_specs=[
            pl.BlockSpec((None, LP, TC_COLS), lambda b, c: (b, 0, C_MQK // TC_COLS + c)),
            pl.BlockSpec((3, TC_COLS), lambda b, c: (0, c)),
            pl.BlockSpec((1, TC_COLS), lambda b, c: (0, c)),
        ],
        out_specs=pl.BlockSpec((None, LP, TC_COLS), lambda b, c: (b, 0, c)),
        out_shape=jax.ShapeDtypeStruct((BATCH, LP, 2 * M_WIDTH), BF16),
        scratch_shapes=[pltpu.VMEM((SEQ + N_META + 16, TC_COLS), F32)],
        compiler_params=_cparams(("parallel", "parallel")),
        name="qk_conv",
    )(proj3, w, scale)


def _mlstm_kernel(qf, kf, vf, gcf, arf, qb, kb, vb, gcb, arb, hf_ref, hb_ref, c_ref, n_ref):
    c = pl.program_id(1)

    @pl.when(c == 0)
    def _():
        c_ref[...] = jnp.zeros_like(c_ref)
        n_ref[...] = jnp.zeros_like(n_ref)

    row = lax.broadcasted_iota(jnp.int32, (BLK, BLK), 0)
    col = lax.broadcasted_iota(jnp.int32, (BLK, BLK), 1)
    dirs = ((qf, kf, vf, gcf, arf, hf_ref), (qb, kb, vb, gcb, arb, hb_ref))
    for d, (q_ref, k_ref, v_ref, gc_ref, ar_ref, o_ref) in enumerate(dirs):
        mask = (col <= row) if d == 0 else (col >= row)
        gc = gc_ref[...]
        for h in range(M_HEADS):
            def colq(qi):
                j = d * 40 + qi * 8 + h
                return gc[:, j:j + 1]
            a_col, m_col, e_col, mp_col, ml_col = (colq(i) for i in range(5))
            a_row = ar_ref[d * 8 + h:d * 8 + h + 1, :]
            hs = slice(h * M_HEAD_DIM, (h + 1) * M_HEAD_DIM)
            q = q_ref[:, hs]
            k = k_ref[:, hs]
            v = v_ref[:, hs]
            s = lax.dot_general(q, k, (((1,), (1,)), ((), ())), preferred_element_type=F32)
            p = jnp.where(mask, jnp.exp(a_row - m_col), 0.0)
            sp = s * p
            a = jnp.exp(mp_col - m_col)
            ct = c_ref[d * M_HEADS + h]
            nrow = n_ref[d * M_HEADS + h]
            inter = jnp.dot(q, ct.astype(BF16), preferred_element_type=F32)
            num = jnp.dot(sp.astype(BF16), v, preferred_element_type=F32) + a * inter
            qn = jnp.sum(q.astype(F32) * nrow, axis=1, keepdims=True)
            den = jnp.sum(sp, axis=1, keepdims=True) + a * qn
            o_ref[:, hs] = num / jnp.maximum(jnp.abs(den), jnp.exp(-e_col))
            w = jnp.exp(a_col - ml_col)
            dec = jnp.exp(mp_col[0:1, :] - ml_col[0:1, :])
            wv = (w * v.astype(F32)).astype(BF16)
            c_ref[d * M_HEADS + h] = dec * ct + lax.dot_general(
                k, wv, (((0,), (0,)), ((), ())), preferred_element_type=F32)
            n_ref[d * M_HEADS + h] = dec * nrow + jnp.sum(w * k.astype(F32), axis=0, keepdims=True)


def _fwd_blk(c):
    return (c + NBLK - 1) % NBLK


def _bwd_blk(c):
    return (2 * NBLK - 2 - c) % NBLK


def _mlstm(qk3, proj3, gcol, arow):
    def specs(blk):
        return [
            pl.BlockSpec((None, BLK, M_WIDTH), lambda b, c: (b, blk(c), 0)),
            pl.BlockSpec((None, BLK, M_WIDTH), lambda b, c: (b, blk(c), 1)),
            pl.BlockSpec((None, BLK, M_WIDTH), lambda b, c: (b, blk(c), C_MV // M_WIDTH)),
            pl.BlockSpec((None, BLK, BLK), lambda b, c: (b, blk(c), 0)),
            pl.BlockSpec((None, 16, BLK), lambda b, c: (b, 0, blk(c))),
        ]
    out_sd = jax.ShapeDtypeStruct((BATCH, LP, M_WIDTH), F32)
    return pl.pallas_call(
        _mlstm_kernel,
        grid=(BATCH, NBLK),
        in_specs=specs(_fwd_blk) + specs(_bwd_blk),
        out_specs=[
            pl.BlockSpec((None, BLK, M_WIDTH), lambda b, c: (b, _fwd_blk(c), 0)),
            pl.BlockSpec((None, BLK, M_WIDTH), lambda b, c: (b, _bwd_blk(c), 0)),
        ],
        out_shape=[out_sd, out_sd],
        scratch_shapes=[
            pltpu.VMEM((2 * M_HEADS, M_HEAD_DIM, M_HEAD_DIM), F32),
            pltpu.VMEM((2 * M_HEADS, 1, M_HEAD_DIM), F32),
        ],
        compiler_params=_cparams(("parallel", "arbitrary")),
        name="mlstm",
    )(qk3, qk3, proj3, gcol, arow, qk3, qk3, proj3, gcol, arow)


TQ = 384
TKV = 1408


def _attn_kernel(lam_ref, q_ref, k_ref, v_ref, bias_ref, g_ref, o_ref):
    lq = lam_ref[...]
    l1 = jnp.sum(lq[0:1] * lq[1:2], axis=1, keepdims=True)
    l2 = jnp.sum(lq[2:3] * lq[3:4], axis=1, keepdims=True)
    lam = jnp.exp(l1) - jnp.exp(l2) + LAM_INIT

    q = q_ref[...]
    lane = lax.broadcasted_iota(jnp.int32, q.shape, 1)
    zero = jnp.zeros_like(q)
    q2 = jnp.concatenate([jnp.where(lane < D_QK_DIM, q, zero), jnp.where(lane >= D_QK_DIM, q, zero)], axis=0)

    m = jnp.full((2 * TQ, 1), -jnp.inf, F32)
    l = jnp.zeros((2 * TQ, 1), F32)
    acc = jnp.zeros((2 * TQ, D_V_DIM), F32)
    nkv = LP // TKV
    for j in range(nkv):
        rows = slice(j * TKV, (j + 1) * TKV)
        s = lax.dot_general(q2, k_ref[rows, :], (((1,), (1,)), ((), ())), preferred_element_type=F32)
        if j == nkv - 1:
            s = s + bias_ref[:, rows]
        m_new = jnp.maximum(m, jnp.max(s, axis=1, keepdims=True))
        p = jnp.exp(s - m_new)
        alpha = jnp.exp(m - m_new)
        l = alpha * l + jnp.sum(p, axis=1, keepdims=True)
        acc = alpha * acc + jnp.dot(p.astype(BF16), v_ref[rows, :], preferred_element_type=F32)
        m = m_new
    o = acc[:TQ] / l[:TQ] - lam * (acc[TQ:] / l[TQ:])
    r = lax.rsqrt(jnp.mean(o * o, axis=-1, keepdims=True) + EPS)
    o_ref[...] = (((o * r) * g_ref[...]) * (1.0 - LAM_INIT)).astype(o_ref.dtype)


def _attention(lam4, proj3, bias, subln_g):
    return pl.pallas_call(
        _attn_kernel,
        grid=(BATCH, D_HEADS, LP // TQ),
        in_specs=[
            pl.BlockSpec((4, D_QK_DIM), lambda b, h, i: (0, 0)),
            pl.BlockSpec((None, TQ, 128), lambda b, h, i: (b, i, C_DQ // 128 + h)),
            pl.BlockSpec((None, LP, 128), lambda b, h, i: (b, 0, C_DK // 128 + h)),
            pl.BlockSpec((None, LP, 128), lambda b, h, i: (b, 0, C_DV // 128 + h)),
            pl.BlockSpec((1, LP), lambda b, h, i: (0, 0)),
            pl.BlockSpec((1, D_V_DIM), lambda b, h, i: (0, 0)),
        ],
        out_specs=pl.BlockSpec((None, TQ, D_V_DIM), lambda b, h, i: (b, i, h)),
        out_shape=jax.ShapeDtypeStruct((BATCH, LP, D_WIDTH), BF16),
        compiler_params=_cparams(("parallel", "parallel", "arbitrary")),
        name="diff_attn",
    )(lam4, proj3, proj3, proj3, bias, subln_g)


TM4 = 1056
TN4 = 512
LN_CH = 96


def _merge_kernel(hf_ref, hb_ref, mo_ref, ng_ref, bo_ref, wa_ref, wb_ref, gm_ref, gd_ref, o_ref, a_ref):
    n = pl.program_id(1)

    @pl.when(n == 0)
    def _():
        def body(i, carry):
            rows = pl.ds(pl.multiple_of(i * LN_CH, LN_CH), LN_CH)
            for h in range(M_HEADS):
                hs = slice(h * M_HEAD_DIM, (h + 1) * M_HEAD_DIM)
                x = hf_ref[rows, hs] + hb_ref[rows, hs]
                mu = jnp.mean(x, axis=-1, keepdims=True)
                xc = x - mu
                var = jnp.mean(xc * xc, axis=-1, keepdims=True)
                hn = (xc * lax.rsqrt(var + EPS)) * ng_ref[:, hs]
                a_ref[rows, hs] = (jax.nn.sigmoid(mo_ref[rows, hs].astype(F32)) * hn).astype(BF16)
            return carry
        lax.fori_loop(0, TM4 // LN_CH, body, 0)

    ya = jnp.dot(a_ref[...], wa_ref[...], preferred_element_type=F32)
    yb = jnp.dot(bo_ref[...], wb_ref[...], preferred_element_type=F32)
    o_ref[...] = (jax.nn.sigmoid(gm_ref[...].astype(F32)) * ya
                  + jax.nn.sigmoid(gd_ref[...].astype(F32)) * yb).astype(o_ref.dtype)


def _merge(hf, hb, proj, norm_g, b_out, wa, wb):
    return pl.pallas_call(
        _merge_kernel,
        grid=(R // TM4, D_MODEL // TN4),
        in_specs=[
            pl.BlockSpec((TM4, M_WIDTH), lambda m, n: (m, 0)),
            pl.BlockSpec((TM4, M_WIDTH), lambda m, n: (m, 0)),
            pl.BlockSpec((TM4, M_WIDTH), lambda m, n: (m, C_MO // M_WIDTH)),
            pl.BlockSpec((1, M_WIDTH), lambda m, n: (0, 0)),
            pl.BlockSpec((TM4, D_WIDTH), lambda m, n: (m, 0)),
            pl.BlockSpec((M_WIDTH, TN4), lambda m, n: (0, n)),
            pl.BlockSpec((D_WIDTH, TN4), lambda m, n: (0, n)),
            pl.BlockSpec((TM4, TN4), lambda m, n: (m, C_GM // TN4 + n)),
            pl.BlockSpec((TM4, TN4), lambda m, n: (m, C_GD // TN4 + n)),
        ],
        out_specs=pl.BlockSpec((TM4, TN4), lambda m, n: (m, n)),
        out_shape=jax.ShapeDtypeStruct((R, D_MODEL), BF16),
        scratch_shapes=[pltpu.VMEM((TM4, M_WIDTH), BF16)],
        compiler_params=_cparams(("parallel", "arbitrary")),
        name="merge",
    )(hf, hb, proj, norm_g, b_out, wa, wb, proj, proj)


TM5 = 384


def _outproj_kernel(h0_ref, mg_ref, w_ref, g2_ref, h1_ref, u2_ref):
    h1 = h0_ref[...] + jnp.dot(mg_ref[...], w_ref[...], preferred_element_type=F32)
    h1_ref[...] = h1
    r = lax.rsqrt(jnp.mean(h1 * h1, axis=-1, keepdims=True) + EPS)
    u2_ref[...] = ((h1 * r) * g2_ref[...]).astype(BF16)


def _outproj(h0, merged, w, g2):
    return pl.pallas_call(
        _outproj_kernel,
        grid=(R // TM5,),
        in_specs=[
            pl.BlockSpec((TM5, D_MODEL), lambda m: (m, 0)),
            pl.BlockSpec((TM5, D_MODEL), lambda m: (m, 0)),
            pl.BlockSpec((D_MODEL, D_MODEL), lambda m: (0, 0)),
            pl.BlockSpec((1, D_MODEL), lambda m: (0, 0)),
        ],
        out_specs=[
            pl.BlockSpec((TM5, D_MODEL), lambda m: (m, 0)),
            pl.BlockSpec((TM5, D_MODEL), lambda m: (m, 0)),
        ],
        out_shape=[
            jax.ShapeDtypeStruct((R, D_MODEL), F32),
            jax.ShapeDtypeStruct((R, D_MODEL), BF16),
        ],
        compiler_params=_cparams(("parallel",)),
        name="outproj",
    )(h0, merged, w, g2)


TM6 = 1056
TN6 = 512


def _mm_kernel(x_ref, w_ref, o_ref):
    o_ref[...] = jnp.dot(x_ref[...], w_ref[...], preferred_element_type=F32).astype(o_ref.dtype)


def _up_proj(u2, w):
    n_out = w.shape[1]
    return pl.pallas_call(
        _mm_kernel,
        grid=(R // TM6, n_out // TN6),
        in_specs=[
            pl.BlockSpec((TM6, D_MODEL), lambda m, n: (m, 0)),
            pl.BlockSpec((D_MODEL, TN6), lambda m, n: (0, n)),
        ],
        out_specs=pl.BlockSpec((TM6, TN6), lambda m, n: (m, n)),
        out_shape=jax.ShapeDtypeStruct((R, n_out), BF16),
        compiler_params=_cparams(("parallel", "arbitrary")),
        name="up_proj",
    )(u2, w)


TM7 = 512
TK7 = 512
HALO = 16
NK7 = FFN_DIM // TK7


def _ffn_conv(main_ref, prev_ref, next_ref, w_ref, s_ref, last_tile):
    s_ref[0:HALO, :] = prev_ref[...].astype(F32)
    s_ref[HALO:HALO + TM7, :] = main_ref[...].astype(F32)
    nxt = next_ref[...].astype(F32)
    s_ref[HALO + TM7:2 * HALO + TM7, :] = jnp.where(last_tile, jnp.zeros_like(nxt), nxt)
    w = w_ref[...]
    return (s_ref[HALO - 1:HALO - 1 + TM7, :] * w[0:1] + s_ref[HALO:HALO + TM7, :] * w[1:2]
            + s_ref[HALO + 1:HALO + 1 + TM7, :] * w[2:3])


def _down_kernel(gm_ref, gp_ref, gn_ref, vm_ref, vp_ref, vn_ref, wg_ref, wv_ref, wd_ref, h1_ref, gf_ref,
                 o_ref, acc_ref, sg_ref, sv_ref):
    i = pl.program_id(1)
    k = pl.program_id(2)
    last_tile = i == SEQ // TM7 - 1

    @pl.when(k == 0)
    def _():
        acc_ref[...] = jnp.zeros_like(acc_ref)

    gate = _ffn_conv(gm_ref, gp_ref, gn_ref, wg_ref, sg_ref, last_tile)
    val = _ffn_conv(vm_ref, vp_ref, vn_ref, wv_ref, sv_ref, last_tile)
    act = (gate * jax.nn.sigmoid(gate) * val).astype(BF16)
    acc_ref[...] += jnp.dot(act, wd_ref[...], preferred_element_type=F32)

    @pl.when(k == NK7 - 1)
    def _():
        h2 = h1_ref[...] + acc_ref[...]
        r = lax.rsqrt(jnp.mean(h2 * h2, axis=-1, keepdims=True) + EPS)
        o_ref[...] = (h2 * r) * gf_ref[...]


def _ffn_down(up3, conv_w, w_down, h1_3, gf):
    hb = TM7 // HALO
    meta_hblk = SEQ // HALO

    def prev_idx(i):
        return jnp.where(i == 0, meta_hblk, i * hb - 1)

    def specs(coff):
        return [
            pl.BlockSpec((None, TM7, TK7), lambda b, i, k: (b, i, coff + k)),
            pl.BlockSpec((None, HALO, TK7), lambda b, i, k: (b, prev_idx(i), coff + k)),
            pl.BlockSpec((None, HALO, TK7), lambda b, i, k: (b, (i + 1) * hb, coff + k)),
        ]
    return pl.pallas_call(
        _down_kernel,
        grid=(BATCH, SEQ // TM7, NK7),
        in_specs=specs(0) + specs(NK7) + [
            pl.BlockSpec((3, TK7), lambda b, i, k: (0, k)),
            pl.BlockSpec((3, TK7), lambda b, i, k: (0, NK7 + k)),
            pl.BlockSpec((TK7, D_MODEL), lambda b, i, k: (k, 0)),
            pl.BlockSpec((None, TM7, D_MODEL), lambda b, i, k: (b, i, 0)),
            pl.BlockSpec((1, D_MODEL), lambda b, i, k: (0, 0)),
        ],
        out_specs=pl.BlockSpec((None, TM7, D_MODEL), lambda b, i, k: (b, i, 0)),
        out_shape=jax.ShapeDtypeStruct((BATCH, SEQ, D_MODEL), F32),
        scratch_shapes=[
            pltpu.VMEM((TM7, D_MODEL), F32),
            pltpu.VMEM((TM7 + 2 * HALO, TK7), F32),
            pltpu.VMEM((TM7 + 2 * HALO, TK7), F32),
        ],
        compiler_params=_cparams(("parallel", "parallel", "arbitrary")),
        name="ffn_down",
    )(up3, up3, up3, up3, up3, up3, conv_w, conv_w, w_down, h1_3, gf)


def kernel(x, meta_tokens, norm1_g, w_in, mlstm_conv_w, mlstm_gate_bias, mlstm_norm_g, lambda_q1, lambda_k1,
           lambda_q2, lambda_k2, diff_subln_g, w_branch_m, w_branch_d, w_out, norm2_g, w_up, ffn_conv_w,
           w_down, norm_f_g):
    assert x.shape == (BATCH, SEQ, D_MODEL) and w_in.shape[0] == 1
    meta = jnp.broadcast_to(meta_tokens[None].astype(x.dtype), (BATCH, N_META, D_MODEL))
    pad = jnp.zeros((BATCH, LP - SEQ - N_META, D_MODEL), x.dtype)
    h0 = jnp.concatenate([x, meta, pad], axis=1)
    h0f = h0.reshape(R, D_MODEL)

    wi = w_in[0]
    g0 = 4 * M_WIDTH
    w_proj = jnp.concatenate([wi[:, :g0], wi[:, g0 + 16:]], axis=1).astype(BF16)
    wg = wi[:, g0:g0 + 16].T.reshape(4, 4, D_MODEL)
    wgt = jnp.pad(wg, ((0, 0), (0, 4), (0, 0))).reshape(GROWS, D_MODEL).astype(BF16)
    gbias = jnp.pad(mlstm_gate_bias[0].reshape(4, 4), ((0, 0), (0, 4))).reshape(GROWS, 1).astype(F32)
    g1 = norm1_g[0].reshape(1, D_MODEL)

    proj = _inproj(h0f, g1, w_proj, _rope_tables())
    proj3 = proj.reshape(BATCH, LP, PROJ_W)

    gt = _gates(h0, g1, wgt, gbias)
    arow, gcol = _scans(gt)

    kscale = jnp.concatenate([jnp.ones((1, M_WIDTH), F32), jnp.full((1, M_WIDTH), M_HEAD_DIM ** -0.5, F32)], axis=1)
    qk3 = _qkconv(proj3, mlstm_conv_w[0], kscale)
    hf, hb = _mlstm(qk3, proj3, gcol, arow)

    lam4 = jnp.stack([lambda_q1[0], lambda_k1[0], lambda_q2[0], lambda_k2[0]]).astype(F32)
    kbias = np.zeros((1, LP), np.float32)
    kbias[0, SEQ + N_META:] = NEG
    b_out = _attention(lam4, proj3, jnp.asarray(kbias), diff_subln_g[0].reshape(1, D_V_DIM))

    merged = _merge(hf.reshape(R, M_WIDTH), hb.reshape(R, M_WIDTH), proj, mlstm_norm_g[0].reshape(1, M_WIDTH),
                    b_out.reshape(R, D_WIDTH), w_branch_m[0].astype(BF16), w_branch_d[0].astype(BF16))
    h1, u2 = _outproj(h0f, merged, w_out[0].astype(BF16), norm2_g[0].reshape(1, D_MODEL))

    up = _up_proj(u2, w_up[0].astype(BF16))
    return _ffn_down(up.reshape(BATCH, LP, 2 * FFN_DIM), ffn_conv_w[0], w_down[0].astype(BF16),
                     h1.reshape(BATCH, LP, D_MODEL), norm_f_g.reshape(1, D_MODEL))
```

```python
import functools
import math

import numpy as np
import jax
import jax.numpy as jnp
from jax import lax
from jax.experimental import pallas as pl
from jax.experimental.pallas import tpu as pltpu

F32 = jnp.float32
BF16 = jnp.bfloat16

D_MODEL = 2048
BATCH = 2
SEQ = 4096
N_META = 16
EPS = 1e-6
M_HEADS = 4
M_WIDTH = 1024
M_HEAD_DIM = 256
D_HEADS = 8
D_V_DIM = 128
D_QK_DIM = 64
D_WIDTH = 1024
ROPE_THETA = 500000.0
ROPE_DIM = 16
FFN_DIM = 5632
LAM_INIT = 0.8 - 0.6 * math.exp(-0.3 * 0)

BLK = 128
LP = SEQ + BLK
NBLK = LP // BLK
META_BLK = SEQ // BLK
R = BATCH * LP
NEG = -1e30

C_MQK, C_MV, C_MO, C_DQ, C_DK, C_DV, C_GM, C_GD = 0, 2048, 3072, 4096, 5120, 6144, 7168, 9216
PROJ_W = 11264

VMEM_LIMIT = 56 * 1024 * 1024


def _cparams(sem):
    return pltpu.CompilerParams(dimension_semantics=sem, vmem_limit_bytes=VMEM_LIMIT)


TM1 = 1056
TN1 = 1024
NORM_CH = 96


def _rmsnorm_rows_to(h_ref, g_ref, u_ref, tm):
    def body(i, carry):
        rows = pl.ds(pl.multiple_of(i * NORM_CH, NORM_CH), NORM_CH)
        x = h_ref[rows, :]
        r = lax.rsqrt(jnp.mean(x * x, axis=-1, keepdims=True) + EPS)
        u_ref[rows, :] = ((x * r) * g_ref[...]).astype(BF16)
        return carry
    lax.fori_loop(0, tm // NORM_CH, body, 0)


def _rope_tile(acc, c, s):
    lane = lax.broadcasted_iota(jnp.int32, c.shape, 1)
    first = (lane % D_QK_DIM) < (ROPE_DIM // 2)
    outs = []
    for g in range(acc.shape[1] // 128):
        y = acc[:, g * 128:(g + 1) * 128]
        rot = jnp.where(first, pltpu.roll(y, 128 - ROPE_DIM // 2, 1), pltpu.roll(y, ROPE_DIM // 2, 1))
        outs.append(y * c + rot * s)
    return jnp.concatenate(outs, axis=1)


NA1 = C_DQ // TN1


def _inproj_kernel(h_ref, g_ref, wa_ref, wb_ref, cq_ref, sq_ref, ck_ref, sk_ref, o_ref, u_ref):
    n = pl.program_id(1)

    @pl.when(n == 0)
    def _():
        _rmsnorm_rows_to(h_ref, g_ref, u_ref, TM1)

    def mm(w_ref):
        return jnp.dot(u_ref[...], w_ref[...], preferred_element_type=F32)

    q0, k0, v0 = C_DQ // TN1, C_DK // TN1, C_DV // TN1

    @pl.when(n < NA1)
    def _():
        o_ref[...] = mm(wa_ref).astype(o_ref.dtype)

    @pl.when(jnp.logical_and(n >= q0, n < k0))
    def _():
        o_ref[...] = _rope_tile(mm(wb_ref), cq_ref[...], sq_ref[...]).astype(o_ref.dtype)

    @pl.when(jnp.logical_and(n >= k0, n < v0))
    def _():
        o_ref[...] = _rope_tile(mm(wb_ref), ck_ref[...], sk_ref[...]).astype(o_ref.dtype)

    @pl.when(n >= v0)
    def _():
        o_ref[...] = mm(wb_ref).astype(o_ref.dtype)


def _inproj(h, g, wa, wb, tabs):
    tiles_per_batch = LP // TM1
    tab_spec = pl.BlockSpec((TM1, 128), lambda m, n: (m % tiles_per_batch, 0))
    return pl.pallas_call(
        _inproj_kernel,
        grid=(R // TM1, PROJ_W // TN1),
        in_specs=[
            pl.BlockSpec((TM1, D_MODEL), lambda m, n: (m, 0)),
            pl.BlockSpec((1, D_MODEL), lambda m, n: (0, 0)),
            pl.BlockSpec((D_MODEL, TN1), lambda m, n: (0, jnp.minimum(n, NA1 - 1))),
            pl.BlockSpec((D_MODEL, TN1), lambda m, n: (0, jnp.maximum(n - NA1, 0))),
            tab_spec, tab_spec, tab_spec, tab_spec,
        ],
        out_specs=pl.BlockSpec((TM1, TN1), lambda m, n: (m, n)),
        out_shape=jax.ShapeDtypeStruct((R, PROJ_W), BF16),
        scratch_shapes=[pltpu.VMEM((TM1, D_MODEL), BF16)],
        compiler_params=_cparams(("parallel", "arbitrary")),
        name="inproj",
    )(h, g, wa, wb, *tabs)


def _rope_tables():
    pos = np.zeros((LP,), np.float64)
    pos[:SEQ] = np.arange(SEQ) + N_META
    pos[SEQ:SEQ + N_META] = np.arange(N_META)
    half = ROPE_DIM // 2
    inv = ROPE_THETA ** (-np.arange(0, ROPE_DIM, 2, dtype=np.float64) / ROPE_DIM)
    ang = pos[:, None] * inv[None, :]
    cos64 = np.ones((LP, D_QK_DIM)); sin64 = np.zeros((LP, D_QK_DIM))
    cos64[:, :half] = np.cos(ang); cos64[:, half:ROPE_DIM] = np.cos(ang)
    sin64[:, :half] = -np.sin(ang); sin64[:, half:ROPE_DIM] = np.sin(ang)
    cos = np.concatenate([cos64, cos64], axis=1)
    sin = np.concatenate([sin64, sin64], axis=1)
    qs = D_QK_DIM ** -0.5 * math.log2(math.e)
    return [jnp.asarray(t, F32) for t in (cos * qs, sin * qs, cos, sin)]


TMG = 384
GROWS = 32


def _gates_kernel(h_ref, g_ref, wg_ref, b_ref, o_ref):
    x = h_ref[...]
    r = lax.rsqrt(jnp.mean(x * x, axis=-1, keepdims=True) + EPS)
    u = ((x * r) * g_ref[...]).astype(BF16)
    gt = lax.dot_general(wg_ref[...], u, (((1,), (1,)), ((), ())), preferred_element_type=F32)
    o_ref[...] = gt + b_ref[...]


def _gates(h3, g, wgt, bias):
    return pl.pallas_call(
        _gates_kernel,
        grid=(BATCH, LP // TMG),
        in_specs=[
            pl.BlockSpec((None, TMG, D_MODEL), lambda b, j: (b, j, 0)),
            pl.BlockSpec((1, D_MODEL), lambda b, j: (0, 0)),
            pl.BlockSpec((GROWS, D_MODEL), lambda b, j: (0, 0)),
            pl.BlockSpec((GROWS, 1), lambda b, j: (0, 0)),
        ],
        out_specs=pl.BlockSpec((None, GROWS, TMG), lambda b, j: (b, 0, j)),
        out_shape=jax.ShapeDtypeStruct((BATCH, GROWS, LP), F32),
        compiler_params=_cparams(("parallel", "parallel")),
        name="gates",
    )(h3, g, wgt, bias)


def _lane_scan(x, op, ident, reverse):
    lane = lax.broadcasted_iota(jnp.int32, x.shape, 1)
    k = 1
    while k < BLK:
        if reverse:
            sh = pltpu.roll(x, BLK - k, 1)
            x = op(x, jnp.where(lane < BLK - k, sh, ident))
        else:
            sh = pltpu.roll(x, k, 1)
            x = op(x, jnp.where(lane >= k, sh, ident))
        k *= 2
    return x


def _log_sigmoid(x):
    return jnp.minimum(x, 0.0) - jnp.log1p(jnp.exp(-jnp.abs(x)))


def _scan_kernel(g_ref, arow_ref, gcol_ref, colt_ref):
    colt_ref[...] = jnp.zeros_like(colt_ref)
    lane = lax.broadcasted_iota(jnp.int32, (8, BLK), 1)
    for d in range(2):
        reverse = d == 1
        carry_b = jnp.zeros((8, BLK), F32)
        carry_m = jnp.full((8, BLK), NEG, F32)
        logical = [META_BLK] + list(range(META_BLK))
        order = logical[::-1] if reverse else logical
        edge = 0 if reverse else BLK - 1
        for j in order:
            cols = slice(j * BLK, (j + 1) * BLK)
            li = g_ref[d * 16:d * 16 + 8, cols]
            lf = _log_sigmoid(g_ref[d * 16 + 8:d * 16 + 16, cols])
            if j == META_BLK:
                valid = lane < N_META
                li = jnp.where(valid, li, NEG)
                lf = jnp.where(valid, lf, 0.0)
            bc = carry_b + _lane_scan(lf, jnp.add, 0.0, reverse)
            a = li - bc
            m = jnp.maximum(carry_m, _lane_scan(a, jnp.maximum, NEG, reverse))
            m_last = jnp.broadcast_to(m[:, edge:edge + 1], (8, BLK))
            arow_ref[d * 8:(d + 1) * 8, cols] = a
            base = d * 40
            colt_ref[j, base:base + 8, :] = a
            colt_ref[j, base + 8:base + 16, :] = m
            colt_ref[j, base + 16:base + 24, :] = bc + m
            colt_ref[j, base + 24:base + 32, :] = carry_m
            colt_ref[j, base + 32:base + 40, :] = m_last
            carry_b = jnp.broadcast_to(bc[:, edge:edge + 1], (8, BLK))
            carry_m = m_last
    for j in range(NBLK):
        gcol_ref[j * BLK:(j + 1) * BLK, :] = colt_ref[j].T


def _scans(gt):
    return pl.pallas_call(
        _scan_kernel,
        grid=(BATCH,),
        in_specs=[pl.BlockSpec((None, GROWS, LP), lambda b: (b, 0, 0))],
        out_specs=[
            pl.BlockSpec((None, 16, LP), lambda b: (b, 0, 0)),
            pl.BlockSpec((None, LP, BLK), lambda b: (b, 0, 0)),
        ],
        out_shape=[
            jax.ShapeDtypeStruct((BATCH, 16, LP), F32),
            jax.ShapeDtypeStruct((BATCH, LP, BLK), F32),
        ],
        scratch_shapes=[pltpu.VMEM((NBLK, BLK, BLK), F32)],
        compiler_params=_cparams(("parallel",)),
        name="gate_scans",
    )(gt)


TC_COLS = 256
CONV_CH = 256


def _seq_conv_rows(s_ref, w, lo, n):
    return (s_ref[lo - 1:lo - 1 + n, :] * w[0:1] + s_ref[lo:lo + n, :] * w[1:2]
            + s_ref[lo + 1:lo + 1 + n, :] * w[2:3])


def _qkconv_kernel(x_ref, w_ref, sc_ref, o_ref, s_ref):
    xo = 8 + N_META
    s_ref[0:8, :] = jnp.zeros((8, TC_COLS), F32)
    s_ref[xo + SEQ:xo + SEQ + 8, :] = jnp.zeros((8, TC_COLS), F32)
    s_ref[8:xo, :] = x_ref[SEQ:SEQ + N_META, :].astype(F32)
    for i in range(SEQ // CONV_CH):
        s_ref[xo + i * CONV_CH:xo + (i + 1) * CONV_CH, :] = x_ref[i * CONV_CH:(i + 1) * CONV_CH, :].astype(F32)
    w = w_ref[...]
    sc = sc_ref[...]

    def act(y):
        return (y * jax.nn.sigmoid(y) * sc).astype(o_ref.dtype)

    for i in range(SEQ // CONV_CH):
        o_ref[i * CONV_CH:(i + 1) * CONV_CH, :] = act(_seq_conv_rows(s_ref, w, xo + i * CONV_CH, CONV_CH))
    o_ref[SEQ:SEQ + N_META, :] = act(_seq_conv_rows(s_ref, w, 8, N_META))
    o_ref[SEQ + N_META:LP, :] = jnp.zeros((LP - SEQ - N_META, TC_COLS), o_ref.dtype)


def _qkconv(proj3, w, scale):
    return pl.pallas_call(
        _qkconv_kernel,
        grid=(BATCH, 2 * M_WIDTH // TC_COLS),
        in_specs=[
            pl.BlockSpec((None, LP, TC_COLS), lambda b, c: (b, 0, C_MQK // TC_COLS + c)),
            pl.BlockSpec((3, TC_COLS), lambda b, c: (0, c)),
            pl.BlockSpec((1, TC_COLS), lambda b, c: (0, c)),
        ],
        out_specs=pl.BlockSpec((None, LP, TC_COLS), lambda b, c: (b, 0, c)),
        out_shape=jax.ShapeDtypeStruct((BATCH, LP, 2 * M_WIDTH), BF16),
        scratch_shapes=[pltpu.VMEM((SEQ + N_META + 16, TC_COLS), F32)],
        compiler_params=_cparams(("parallel", "parallel")),
        name="qk_conv",
    )(proj3, w, scale)


def _mlstm_kernel(qf, kf, vf, gcf, arf, qb, kb, vb, gcb, arb, hf_ref, hb_ref, c_ref, n_ref):
    c = pl.program_id(1)

    @pl.when(c == 0)
    def _():
        c_ref[...] = jnp.zeros_like(c_ref)
        n_ref[...] = jnp.zeros_like(n_ref)

    row = lax.broadcasted_iota(jnp.int32, (BLK, BLK), 0)
    col = lax.broadcasted_iota(jnp.int32, (BLK, BLK), 1)
    dirs = ((qf, kf, vf, gcf, arf, hf_ref), (qb, kb, vb, gcb, arb, hb_ref))
    for d, (q_ref, k_ref, v_ref, gc_ref, ar_ref, o_ref) in enumerate(dirs):
        mask = (col <= row) if d == 0 else (col >= row)
        gc = gc_ref[...]
        for h in range(M_HEADS):
            def colq(qi):
                j = d * 40 + qi * 8 + h
                return gc[:, j:j + 1]
            a_col, m_col, e_col, mp_col, ml_col = (colq(i) for i in range(5))
            a_row = ar_ref[d * 8 + h:d * 8 + h + 1, :]
            hs = slice(h * M_HEAD_DIM, (h + 1) * M_HEAD_DIM)
            q = q_ref[:, hs]
            k = k_ref[:, hs]
            v = v_ref[:, hs]
            s = lax.dot_general(q, k, (((1,), (1,)), ((), ())), preferred_element_type=F32)
            p = jnp.where(mask, jnp.exp(a_row - m_col), 0.0)
            sp = s * p
            a = jnp.exp(mp_col - m_col)
            ct = c_ref[d * M_HEADS + h]
            nrow = n_ref[d * M_HEADS + h]
            inter = jnp.dot(q, ct.astype(BF16), preferred_element_type=F32)
            num = jnp.dot(sp.astype(BF16), v, preferred_element_type=F32) + a * inter
            qn = jnp.sum(q.astype(F32) * nrow, axis=1, keepdims=True)
            den = jnp.sum(sp, axis=1, keepdims=True) + a * qn
            o_ref[:, hs] = num / jnp.maximum(jnp.abs(den), jnp.exp(-e_col))
            w = jnp.exp(a_col - ml_col)
            dec = jnp.exp(mp_col[0:1, :] - ml_col[0:1, :])
            wv = (w * v.astype(F32)).astype(BF16)
            c_ref[d * M_HEADS + h] = dec * ct + lax.dot_general(
                k, wv, (((0,), (0,)), ((), ())), preferred_element_type=F32)
            n_ref[d * M_HEADS + h] = dec * nrow + jnp.sum(w * k.astype(F32), axis=0, keepdims=True)


def _fwd_blk(c):
    return (c + NBLK - 1) % NBLK


def _bwd_blk(c):
    return (2 * NBLK - 2 - c) % NBLK


def _mlstm(qk3, proj3, gcol, arow):
    def specs(blk):
        return [
            pl.BlockSpec((None, BLK, M_WIDTH), lambda b, c: (b, blk(c), 0)),
            pl.BlockSpec((None, BLK, M_WIDTH), lambda b, c: (b, blk(c), 1)),
            pl.BlockSpec((None, BLK, M_WIDTH), lambda b, c: (b, blk(c), C_MV // M_WIDTH)),
            pl.BlockSpec((None, BLK, BLK), lambda b, c: (b, blk(c), 0)),
            pl.BlockSpec((None, 16, BLK), lambda b, c: (b, 0, blk(c))),
        ]
    out_sd = jax.ShapeDtypeStruct((BATCH, LP, M_WIDTH), F32)
    return pl.pallas_call(
        _mlstm_kernel,
        grid=(BATCH, NBLK),
        in_specs=specs(_fwd_blk) + specs(_bwd_blk),
        out_specs=[
            pl.BlockSpec((None, BLK, M_WIDTH), lambda b, c: (b, _fwd_blk(c), 0)),
            pl.BlockSpec((None, BLK, M_WIDTH), lambda b, c: (b, _bwd_blk(c), 0)),
        ],
        out_shape=[out_sd, out_sd],
        scratch_shapes=[
            pltpu.VMEM((2 * M_HEADS, M_HEAD_DIM, M_HEAD_DIM), F32),
            pltpu.VMEM((2 * M_HEADS, 1, M_HEAD_DIM), F32),
        ],
        compiler_params=_cparams(("parallel", "arbitrary")),
        name="mlstm",
    )(qk3, qk3, proj3, gcol, arow, qk3, qk3, proj3, gcol, arow)


TQ = 384
KV_EDGES = (0, 1024, 2048, 3072, LP)


def _attn_kernel(lam_ref, q_ref, k_ref, v_ref, g_ref, o_ref, vaug_ref):
    @pl.when(pl.program_id(2) == 0)
    def _():
        vaug_ref[:, :D_V_DIM] = v_ref[...]
        vaug_ref[:, D_V_DIM:] = jnp.ones((LP, D_V_DIM), BF16)

    lq = lam_ref[...]
    l1 = jnp.sum(lq[0:1] * lq[1:2], axis=1, keepdims=True)
    l2 = jnp.sum(lq[2:3] * lq[3:4], axis=1, keepdims=True)
    lam = jnp.exp(l1) - jnp.exp(l2) + LAM_INIT

    q = q_ref[...]
    lane = lax.broadcasted_iota(jnp.int32, q.shape, 1)
    zero = jnp.zeros_like(q)
    q2 = jnp.concatenate([jnp.where(lane < D_QK_DIM, q, zero), jnp.where(lane >= D_QK_DIM, q, zero)], axis=0)

    m = jnp.full((2 * TQ, 1), -jnp.inf, F32)
    acc = jnp.zeros((2 * TQ, 2 * D_V_DIM), F32)
    nkv = len(KV_EDGES) - 1

    def scores(j):
        lo, hi = KV_EDGES[j], KV_EDGES[j + 1]
        s = lax.dot_general(q2, k_ref[lo:hi, :], (((1,), (1,)), ((), ())), preferred_element_type=F32)
        if j == nkv - 1:
            colid = lax.broadcasted_iota(jnp.int32, (1, BLK), 1)
            tail = jnp.where(colid < N_META, s[:, hi - lo - BLK:], -jnp.inf)
            s = jnp.concatenate([s[:, :hi - lo - BLK], tail], axis=1)
        return s

    s_next = scores(0)
    for j in range(nkv):
        s = s_next
        if j + 1 < nkv:
            s_next = scores(j + 1)
        m_new = jnp.maximum(m, jnp.max(s, axis=1, keepdims=True))
        p = jnp.exp2((s - m_new).astype(BF16))
        alpha = jnp.exp2(m - m_new)
        acc = alpha * acc + jnp.dot(p, vaug_ref[KV_EDGES[j]:KV_EDGES[j + 1], :], preferred_element_type=F32)
        m = m_new
    on = acc[:, :D_V_DIM] / acc[:, D_V_DIM:]
    o = on[:TQ] - lam * on[TQ:]
    r = lax.rsqrt(jnp.mean(o * o, axis=-1, keepdims=True) + EPS)
    o_ref[...] = (((o * r) * g_ref[...]) * (1.0 - LAM_INIT)).astype(o_ref.dtype)


def _attention(lam4, proj3, subln_g):
    return pl.pallas_call(
        _attn_kernel,
        grid=(BATCH, D_HEADS, LP // TQ),
        in_specs=[
            pl.BlockSpec((4, D_QK_DIM), lambda b, h, i: (0, 0)),
            pl.BlockSpec((None, TQ, 128), lambda b, h, i: (b, i, C_DQ // 128 + h)),
            pl.BlockSpec((None, LP, 128), lambda b, h, i: (b, 0, C_DK // 128 + h)),
            pl.BlockSpec((None, LP, 128), lambda b, h, i: (b, 0, C_DV // 128 + h)),
            pl.BlockSpec((1, D_V_DIM), lambda b, h, i: (0, 0)),
        ],
        out_specs=pl.BlockSpec((None, TQ, D_V_DIM), lambda b, h, i: (b, i, h)),
        out_shape=jax.ShapeDtypeStruct((BATCH, LP, D_WIDTH), BF16),
        scratch_shapes=[pltpu.VMEM((LP, 2 * D_V_DIM), BF16)],
        compiler_params=_cparams(("parallel", "parallel", "arbitrary")),
        name="diff_attn",
    )(lam4, proj3, proj3, proj3, subln_g)


TM4 = 1056
TN4 = 512
LN_CH = 96


def _merge_kernel(hf_ref, hb_ref, mo_ref, ng_ref, bo_ref, wa_ref, wb_ref, gm_ref, gd_ref, o_ref, a_ref):
    n = pl.program_id(1)

    @pl.when(n == 0)
    def _():
        def body(i, carry):
            rows = pl.ds(pl.multiple_of(i * LN_CH, LN_CH), LN_CH)
            for h in range(M_HEADS):
                hs = slice(h * M_HEAD_DIM, (h + 1) * M_HEAD_DIM)
                x = hf_ref[rows, hs] + hb_ref[rows, hs]
                mu = jnp.mean(x, axis=-1, keepdims=True)
                xc = x - mu
                var = jnp.mean(xc * xc, axis=-1, keepdims=True)
                hn = (xc * lax.rsqrt(var + EPS)) * ng_ref[:, hs]
                a_ref[rows, hs] = (jax.nn.sigmoid(mo_ref[rows, hs].astype(F32)) * hn).astype(BF16)
            return carry
        lax.fori_loop(0, TM4 // LN_CH, body, 0)

    ya = jnp.dot(a_ref[...], wa_ref[...], preferred_element_type=F32)
    yb = jnp.dot(bo_ref[...], wb_ref[...], preferred_element_type=F32)
    o_ref[...] = (jax.nn.sigmoid(gm_ref[...].astype(F32)) * ya
                  + jax.nn.sigmoid(gd_ref[...].astype(F32)) * yb).astype(o_ref.dtype)


def _merge(hf, hb, proj, norm_g, b_out, wa, wb):
    return pl.pallas_call(
        _merge_kernel,
        grid=(R // TM4, D_MODEL // TN4),
        in_specs=[
            pl.BlockSpec((TM4, M_WIDTH), lambda m, n: (m, 0)),
            pl.BlockSpec((TM4, M_WIDTH), lambda m, n: (m, 0)),
            pl.BlockSpec((TM4, M_WIDTH), lambda m, n: (m, C_MO // M_WIDTH)),
            pl.BlockSpec((1, M_WIDTH), lambda m, n: (0, 0)),
            pl.BlockSpec((TM4, D_WIDTH), lambda m, n: (m, 0)),
            pl.BlockSpec((M_WIDTH, TN4), lambda m, n: (0, n)),
            pl.BlockSpec((D_WIDTH, TN4), lambda m, n: (0, n)),
            pl.BlockSpec((TM4, TN4), lambda m, n: (m, C_GM // TN4 + n)),
            pl.BlockSpec((TM4, TN4), lambda m, n: (m, C_GD // TN4 + n)),
        ],
        out_specs=pl.BlockSpec((TM4, TN4), lambda m, n: (m, n)),
        out_shape=jax.ShapeDtypeStruct((R, D_MODEL), BF16),
        scratch_shapes=[pltpu.VMEM((TM4, M_WIDTH), BF16)],
        compiler_params=_cparams(("parallel", "arbitrary")),
        name="merge",
    )(hf, hb, proj, norm_g, b_out, wa, wb, proj, proj)


TM5 = 384


def _outproj_kernel(h0_ref, mg_ref, w_ref, g2_ref, h1_ref, u2_ref):
    h1 = h0_ref[...] + jnp.dot(mg_ref[...], w_ref[...], preferred_element_type=F32)
    h1_ref[...] = h1
    r = lax.rsqrt(jnp.mean(h1 * h1, axis=-1, keepdims=True) + EPS)
    u2_ref[...] = ((h1 * r) * g2_ref[...]).astype(BF16)


def _outproj(h0, merged, w, g2):
    return pl.pallas_call(
        _outproj_kernel,
        grid=(R // TM5,),
        in_specs=[
            pl.BlockSpec((TM5, D_MODEL), lambda m: (m, 0)),
            pl.BlockSpec((TM5, D_MODEL), lambda m: (m, 0)),
            pl.BlockSpec((D_MODEL, D_MODEL), lambda m: (0, 0)),
            pl.BlockSpec((1, D_MODEL), lambda m: (0, 0)),
        ],
        out_specs=[
            pl.BlockSpec((TM5, D_MODEL), lambda m: (m, 0)),
            pl.BlockSpec((TM5, D_MODEL), lambda m: (m, 0)),
        ],
        out_shape=[
            jax.ShapeDtypeStruct((R, D_MODEL), F32),
            jax.ShapeDtypeStruct((R, D_MODEL), BF16),
        ],
        compiler_params=_cparams(("parallel",)),
        name="outproj",
    )(h0, merged, w, g2)


TMU = 1024
TNU = 512
HALO = 16
NNU = FFN_DIM // TNU


def _ffn_up_kernel(x_ref, xp_ref, xn_ref, wg_ref, wv_ref, cg_ref, cv_ref, o_ref, xs_ref, sg_ref, sv_ref):
    @pl.when(pl.program_id(2) == 0)
    def _():
        last_tile = pl.program_id(1) == SEQ // TMU - 1
        xs_ref[0:HALO, :] = xp_ref[...]
        xs_ref[HALO:HALO + TMU, :] = x_ref[...]
        xn = xn_ref[...]
        xs_ref[HALO + TMU:, :] = jnp.where(last_tile, jnp.zeros_like(xn), xn)

    def branch(w_ref, c_ref, s_ref):
        s_ref[...] = jnp.dot(xs_ref[...], w_ref[...].astype(BF16), preferred_element_type=F32)
        c = c_ref[...]
        return (s_ref[HALO - 1:HALO - 1 + TMU, :] * c[0:1] + s_ref[HALO:HALO + TMU, :] * c[1:2]
                + s_ref[HALO + 1:HALO + 1 + TMU, :] * c[2:3])

    gate = branch(wg_ref, cg_ref, sg_ref)
    val = branch(wv_ref, cv_ref, sv_ref)
    o_ref[...] = (gate * jax.nn.sigmoid(gate) * val).astype(o_ref.dtype)


def _ffn_up(u2_3, w_up, conv_w):
    hb = TMU // HALO
    meta_hblk = SEQ // HALO
    return pl.pallas_call(
        _ffn_up_kernel,
        grid=(BATCH, SEQ // TMU, NNU),
        in_specs=[
            pl.BlockSpec((None, TMU, D_MODEL), lambda b, i, n: (b, i, 0)),
            pl.BlockSpec((None, HALO, D_MODEL), lambda b, i, n: (b, jnp.where(i == 0, meta_hblk, i * hb - 1), 0)),
            pl.BlockSpec((None, HALO, D_MODEL), lambda b, i, n: (b, (i + 1) * hb, 0)),
            pl.BlockSpec((D_MODEL, TNU), lambda b, i, n: (0, n)),
            pl.BlockSpec((D_MODEL, TNU), lambda b, i, n: (0, NNU + n)),
            pl.BlockSpec((3, TNU), lambda b, i, n: (0, n)),
            pl.BlockSpec((3, TNU), lambda b, i, n: (0, NNU + n)),
        ],
        out_specs=pl.BlockSpec((None, TMU, TNU), lambda b, i, n: (b, i, n)),
        out_shape=jax.ShapeDtypeStruct((BATCH, SEQ, FFN_DIM), BF16),
        scratch_shapes=[
            pltpu.VMEM((TMU + 2 * HALO, D_MODEL), BF16),
            pltpu.VMEM((TMU + 2 * HALO, TNU), F32),
            pltpu.VMEM((TMU + 2 * HALO, TNU), F32),
        ],
        compiler_params=_cparams(("parallel", "parallel", "arbitrary")),
        name="ffn_up",
    )(u2_3, u2_3, u2_3, w_up, w_up, conv_w, conv_w)


TM7 = 512
TK7 = 1408
NK7 = FFN_DIM // TK7


def _down_kernel(a_ref, wd_ref, h1_ref, gf_ref, o_ref, acc_ref):
    k = pl.program_id(2)

    def part():
        return jnp.dot(a_ref[...], wd_ref[...], preferred_element_type=F32)

    @pl.when(k == 0)
    def _():
        acc_ref[...] = part()

    @pl.when(jnp.logical_and(k > 0, k < NK7 - 1))
    def _():
        acc_ref[...] += part()

    @pl.when(k == NK7 - 1)
    def _():
        h2 = h1_ref[...] + (acc_ref[...] + part())
        r = lax.rsqrt(jnp.mean(h2 * h2, axis=-1, keepdims=True) + EPS)
        o_ref[...] = (h2 * r) * gf_ref[...]


def _ffn_down(act, w_down, h1_3, gf):
    return pl.pallas_call(
        _down_kernel,
        grid=(BATCH, SEQ // TM7, NK7),
        in_specs=[
            pl.BlockSpec((None, TM7, TK7), lambda b, i, k: (b, i, k)),
            pl.BlockSpec((TK7, D_MODEL), lambda b, i, k: (k, 0)),
            pl.BlockSpec((None, TM7, D_MODEL), lambda b, i, k: (b, i, 0)),
            pl.BlockSpec((1, D_MODEL), lambda b, i, k: (0, 0)),
        ],
        out_specs=pl.BlockSpec((None, TM7, D_MODEL), lambda b, i, k: (b, i, 0)),
        out_shape=jax.ShapeDtypeStruct((BATCH, SEQ, D_MODEL), F32),
        scratch_shapes=[pltpu.VMEM((TM7, D_MODEL), F32)],
        compiler_params=_cparams(("parallel", "parallel", "arbitrary")),
        name="ffn_down",
    )(act, w_down, h1_3, gf)


def kernel(x, meta_tokens, norm1_g, w_in, mlstm_conv_w, mlstm_gate_bias, mlstm_norm_g, lambda_q1, lambda_k1,
           lambda_q2, lambda_k2, diff_subln_g, w_branch_m, w_branch_d, w_out, norm2_g, w_up, ffn_conv_w,
           w_down, norm_f_g):
    assert x.shape == (BATCH, SEQ, D_MODEL) and w_in.shape[0] == 1
    meta = jnp.broadcast_to(meta_tokens[None].astype(x.dtype), (BATCH, N_META, D_MODEL))
    pad = jnp.zeros((BATCH, LP - SEQ - N_META, D_MODEL), x.dtype)
    h0 = jnp.concatenate([x, meta, pad], axis=1)
    h0f = h0.reshape(R, D_MODEL)

    wi = w_in[0]
    g0 = 4 * M_WIDTH
    w_pa = wi[:, :g0].astype(BF16)
    w_pb = wi[:, g0 + 16:].astype(BF16)
    wg = wi[:, g0:g0 + 16].T.reshape(4, 4, D_MODEL)
    wgt = jnp.pad(wg, ((0, 0), (0, 4), (0, 0))).reshape(GROWS, D_MODEL).astype(BF16)
    gbias = jnp.pad(mlstm_gate_bias[0].reshape(4, 4), ((0, 0), (0, 4))).reshape(GROWS, 1).astype(F32)
    g1 = norm1_g[0].reshape(1, D_MODEL)

    proj = _inproj(h0f, g1, w_pa, w_pb, _rope_tables())
    proj3 = proj.reshape(BATCH, LP, PROJ_W)

    gt = _gates(h0, g1, wgt, gbias)
    arow, gcol = _scans(gt)

    kscale = jnp.concatenate([jnp.ones((1, M_WIDTH), F32), jnp.full((1, M_WIDTH), M_HEAD_DIM ** -0.5, F32)], axis=1)
    qk3 = _qkconv(proj3, mlstm_conv_w[0], kscale)
    hf, hb = _mlstm(qk3, proj3, gcol, arow)

    lam4 = jnp.stack([lambda_q1[0], lambda_k1[0], lambda_q2[0], lambda_k2[0]]).astype(F32)
    b_out = _attention(lam4, proj3, diff_subln_g[0].reshape(1, D_V_DIM))

    merged = _merge(hf.reshape(R, M_WIDTH), hb.reshape(R, M_WIDTH), proj, mlstm_norm_g[0].reshape(1, M_WIDTH),
                    b_out.reshape(R, D_WIDTH), w_branch_m[0].astype(BF16), w_branch_d[0].astype(BF16))
    h1, u2 = _outproj(h0f, merged, w_out[0].astype(BF16), norm2_g[0].reshape(1, D_MODEL))

    act = _ffn_up(u2.reshape(BATCH, LP, D_MODEL), w_up[0], ffn_conv_w[0])
    return _ffn_down(act, w_down[0].astype(BF16), h1.reshape(BATCH, LP, D_MODEL), norm_f_g.reshape(1, D_MODEL))
```

```python
import functools
import math

import numpy as np
import jax
import jax.numpy as jnp
from jax import lax
from jax.experimental import pallas as pl
from jax.experimental.pallas import tpu as pltpu

F32 = jnp.float32
BF16 = jnp.bfloat16

D_MODEL = 2048
BATCH = 2
SEQ = 4096
N_META = 16
EPS = 1e-6
M_HEADS = 4
M_WIDTH = 1024
M_HEAD_DIM = 256
D_HEADS = 8
D_V_DIM = 128
D_QK_DIM = 64
D_WIDTH = 1024
ROPE_THETA = 500000.0
ROPE_DIM = 16
FFN_DIM = 5632
LAM_INIT = 0.8 - 0.6 * math.exp(-0.3 * 0)

BLK = 128
LP = SEQ + BLK
NBLK = LP // BLK
META_BLK = SEQ // BLK
R = BATCH * LP
NEG = -1e30

C_MQK, C_MV, C_MO, C_DQ, C_DK, C_DV, C_GM, C_GD = 0, 2048, 3072, 4096, 5120, 6144, 7168, 9216
PROJ_W = 11264

VMEM_LIMIT = 56 * 1024 * 1024


def _cparams(sem):
    return pltpu.CompilerParams(dimension_semantics=sem, vmem_limit_bytes=VMEM_LIMIT)


TM1 = 1056
TN1 = 1024
NORM_CH = 96


def _rmsnorm_rows_to(h_ref, g_ref, u_ref, tm):
    def body(i, carry):
        rows = pl.ds(pl.multiple_of(i * NORM_CH, NORM_CH), NORM_CH)
        x = h_ref[rows, :]
        r = lax.rsqrt(jnp.mean(x * x, axis=-1, keepdims=True) + EPS)
        u_ref[rows, :] = ((x * r) * g_ref[...]).astype(BF16)
        return carry
    lax.fori_loop(0, tm // NORM_CH, body, 0)


def _rope_tile(acc, c, s):
    lane = lax.broadcasted_iota(jnp.int32, c.shape, 1)
    first = (lane % D_QK_DIM) < (ROPE_DIM // 2)
    outs = []
    for g in range(acc.shape[1] // 128):
        y = acc[:, g * 128:(g + 1) * 128]
        rot = jnp.where(first, pltpu.roll(y, 128 - ROPE_DIM // 2, 1), pltpu.roll(y, ROPE_DIM // 2, 1))
        outs.append(y * c + rot * s)
    return jnp.concatenate(outs, axis=1)


NA1 = C_DQ // TN1


def _inproj_kernel(h_ref, g_ref, wa_ref, wb_ref, cq_ref, sq_ref, ck_ref, sk_ref, o_ref, u_ref):
    n = pl.program_id(1)

    @pl.when(n == 0)
    def _():
        _rmsnorm_rows_to(h_ref, g_ref, u_ref, TM1)

    def mm(wt_ref):
        return lax.dot_general(u_ref[...], wt_ref[...], (((1,), (1,)), ((), ())), preferred_element_type=F32)

    q0, k0, v0 = C_DQ // TN1, C_DK // TN1, C_DV // TN1

    @pl.when(n < NA1)
    def _():
        o_ref[...] = mm(wa_ref).astype(o_ref.dtype)

    @pl.when(jnp.logical_and(n >= q0, n < k0))
    def _():
        o_ref[...] = _rope_tile(mm(wb_ref), cq_ref[...], sq_ref[...]).astype(o_ref.dtype)

    @pl.when(jnp.logical_and(n >= k0, n < v0))
    def _():
        o_ref[...] = _rope_tile(mm(wb_ref), ck_ref[...], sk_ref[...]).astype(o_ref.dtype)

    @pl.when(n >= v0)
    def _():
        o_ref[...] = mm(wb_ref).astype(o_ref.dtype)


def _inproj(h, g, wa, wb, tabs):
    tiles_per_batch = LP // TM1
    tab_spec = pl.BlockSpec((TM1, 128), lambda m, n: (m % tiles_per_batch, 0))
    return pl.pallas_call(
        _inproj_kernel,
        grid=(R // TM1, PROJ_W // TN1),
        in_specs=[
            pl.BlockSpec((TM1, D_MODEL), lambda m, n: (m, 0)),
            pl.BlockSpec((1, D_MODEL), lambda m, n: (0, 0)),
            pl.BlockSpec((TN1, D_MODEL), lambda m, n: (jnp.minimum(n, NA1 - 1), 0)),
            pl.BlockSpec((TN1, D_MODEL), lambda m, n: (jnp.maximum(n - NA1, 0), 0)),
            tab_spec, tab_spec, tab_spec, tab_spec,
        ],
        out_specs=pl.BlockSpec((TM1, TN1), lambda m, n: (m, n)),
        out_shape=jax.ShapeDtypeStruct((R, PROJ_W), BF16),
        scratch_shapes=[pltpu.VMEM((TM1, D_MODEL), BF16)],
        compiler_params=_cparams(("parallel", "arbitrary")),
        name="inproj",
    )(h, g, wa, wb, *tabs)


def _rope_tables():
    pos = np.zeros((LP,), np.float64)
    pos[:SEQ] = np.arange(SEQ) + N_META
    pos[SEQ:SEQ + N_META] = np.arange(N_META)
    half = ROPE_DIM // 2
    inv = ROPE_THETA ** (-np.arange(0, ROPE_DIM, 2, dtype=np.float64) / ROPE_DIM)
    ang = pos[:, None] * inv[None, :]
    cos64 = np.ones((LP, D_QK_DIM)); sin64 = np.zeros((LP, D_QK_DIM))
    cos64[:, :half] = np.cos(ang); cos64[:, half:ROPE_DIM] = np.cos(ang)
    sin64[:, :half] = -np.sin(ang); sin64[:, half:ROPE_DIM] = np.sin(ang)
    cos = np.concatenate([cos64, cos64], axis=1)
    sin = np.concatenate([sin64, sin64], axis=1)
    qs = D_QK_DIM ** -0.5 * math.log2(math.e)
    return [jnp.asarray(t, F32) for t in (cos * qs, sin * qs, cos, sin)]


TMG = 384
GROWS = 16
GATE_COL0 = 4 * M_WIDTH


def _gates_kernel(h_ref, g_ref, wg_ref, b_ref, o_ref):
    x = h_ref[...]
    r = lax.rsqrt(jnp.mean(x * x, axis=-1, keepdims=True) + EPS)
    u = ((x * r) * g_ref[...]).astype(BF16)
    gt = lax.dot_general(wg_ref[...].astype(BF16), u, (((1,), (1,)), ((), ())), preferred_element_type=F32)
    o_ref[...] = gt + b_ref[...]


def _gates(h3, g, w_in_t, bias):
    return pl.pallas_call(
        _gates_kernel,
        grid=(BATCH, LP // TMG),
        in_specs=[
            pl.BlockSpec((None, TMG, D_MODEL), lambda b, j: (b, j, 0)),
            pl.BlockSpec((1, D_MODEL), lambda b, j: (0, 0)),
            pl.BlockSpec((GROWS, D_MODEL), lambda b, j: (GATE_COL0 // GROWS, 0)),
            pl.BlockSpec((GROWS, 1), lambda b, j: (0, 0)),
        ],
        out_specs=pl.BlockSpec((None, GROWS, TMG), lambda b, j: (b, 0, j)),
        out_shape=jax.ShapeDtypeStruct((BATCH, GROWS, LP), F32),
        compiler_params=_cparams(("parallel", "parallel")),
        name="gates",
    )(h3, g, w_in_t, bias)


def _lane_scan(x, op, ident, reverse):
    lane = lax.broadcasted_iota(jnp.int32, x.shape, 1)
    k = 1
    while k < BLK:
        if reverse:
            sh = pltpu.roll(x, BLK - k, 1)
            x = op(x, jnp.where(lane < BLK - k, sh, ident))
        else:
            sh = pltpu.roll(x, k, 1)
            x = op(x, jnp.where(lane >= k, sh, ident))
        k *= 2
    return x


def _log_sigmoid(x):
    return jnp.minimum(x, 0.0) - jnp.log1p(jnp.exp(-jnp.abs(x)))


def _scan_kernel(g_ref, arow_ref, gcol_ref, colt_ref):
    colt_ref[...] = jnp.zeros_like(colt_ref)
    lane = lax.broadcasted_iota(jnp.int32, (8, BLK), 1)
    for d in range(2):
        reverse = d == 1
        carry_b = jnp.zeros((8, BLK), F32)
        carry_m = jnp.full((8, BLK), NEG, F32)
        logical = [META_BLK] + list(range(META_BLK))
        order = logical[::-1] if reverse else logical
        edge = 0 if reverse else BLK - 1
        for j in order:
            cols = slice(j * BLK, (j + 1) * BLK)
            li = g_ref[d * 8:(d + 1) * 8, cols]
            lf = pltpu.roll(_log_sigmoid(li), M_HEADS, 0)
            if j == META_BLK:
                valid = lane < N_META
                li = jnp.where(valid, li, NEG)
                lf = jnp.where(valid, lf, 0.0)
            bc = carry_b + _lane_scan(lf, jnp.add, 0.0, reverse)
            a = li - bc
            m = jnp.maximum(carry_m, _lane_scan(a, jnp.maximum, NEG, reverse))
            m_last = jnp.broadcast_to(m[:, edge:edge + 1], (8, BLK))
            arow_ref[d * 8:(d + 1) * 8, cols] = a
            base = d * 40
            colt_ref[j, base:base + 8, :] = a
            colt_ref[j, base + 8:base + 16, :] = m
            colt_ref[j, base + 16:base + 24, :] = bc + m
            colt_ref[j, base + 24:base + 32, :] = carry_m
            colt_ref[j, base + 32:base + 40, :] = m_last
            carry_b = jnp.broadcast_to(bc[:, edge:edge + 1], (8, BLK))
            carry_m = m_last
    for j in range(NBLK):
        gcol_ref[j * BLK:(j + 1) * BLK, :] = colt_ref[j].T


def _scans(gt):
    return pl.pallas_call(
        _scan_kernel,
        grid=(BATCH,),
        in_specs=[pl.BlockSpec((None, GROWS, LP), lambda b: (b, 0, 0))],
        out_specs=[
            pl.BlockSpec((None, 16, LP), lambda b: (b, 0, 0)),
            pl.BlockSpec((None, LP, BLK), lambda b: (b, 0, 0)),
        ],
        out_shape=[
            jax.ShapeDtypeStruct((BATCH, 16, LP), F32),
            jax.ShapeDtypeStruct((BATCH, LP, BLK), F32),
        ],
        scratch_shapes=[pltpu.VMEM((NBLK, BLK, BLK), F32)],
        compiler_params=_cparams(("parallel",)),
        name="gate_scans",
    )(gt)


TC_COLS = 256
CONV_CH = 256


def _seq_conv_rows(s_ref, w, lo, n):
    return (s_ref[lo - 1:lo - 1 + n, :] * w[0:1] + s_ref[lo:lo + n, :] * w[1:2]
            + s_ref[lo + 1:lo + 1 + n, :] * w[2:3])


def _qkconv_kernel(x_ref, w_ref, sc_ref, o_ref, s_ref):
    xo = 8 + N_META
    s_ref[0:8, :] = jnp.zeros((8, TC_COLS), F32)
    s_ref[xo + SEQ:xo + SEQ + 8, :] = jnp.zeros((8, TC_COLS), F32)
    s_ref[8:xo, :] = x_ref[SEQ:SEQ + N_META, :].astype(F32)
    for i in range(SEQ // CONV_CH):
        s_ref[xo + i * CONV_CH:xo + (i + 1) * CONV_CH, :] = x_ref[i * CONV_CH:(i + 1) * CONV_CH, :].astype(F32)
    w = w_ref[...]
    sc = sc_ref[...]

    def act(y):
        return (y * jax.nn.sigmoid(y) * sc).astype(o_ref.dtype)

    for i in range(SEQ // CONV_CH):
        o_ref[i * CONV_CH:(i + 1) * CONV_CH, :] = act(_seq_conv_rows(s_ref, w, xo + i * CONV_CH, CONV_CH))
    o_ref[SEQ:SEQ + N_META, :] = act(_seq_conv_rows(s_ref, w, 8, N_META))
    o_ref[SEQ + N_META:LP, :] = jnp.zeros((LP - SEQ - N_META, TC_COLS), o_ref.dtype)


def _qkconv(proj3, w, scale):
    return pl.pallas_call(
        _qkconv_kernel,
        grid=(BATCH, 2 * M_WIDTH // TC_COLS),
        in_specs=[
            pl.BlockSpec((None, LP, TC_COLS), lambda b, c: (b, 0, C_MQK // TC_COLS + c)),
            pl.BlockSpec((3, TC_COLS), lambda b, c: (0, c)),
            pl.BlockSpec((1, TC_COLS), lambda b, c: (0, c)),
        ],
        out_specs=pl.BlockSpec((None, LP, TC_COLS), lambda b, c: (b, 0, c)),
        out_shape=jax.ShapeDtypeStruct((BATCH, LP, 2 * M_WIDTH), BF16),
        scratch_shapes=[pltpu.VMEM((SEQ + N_META + 16, TC_COLS), F32)],
        compiler_params=_cparams(("parallel", "parallel")),
        name="qk_conv",
    )(proj3, w, scale)


def _mlstm_kernel(qf, kf, vf, gcf, arf, qb, kb, vb, gcb, arb, hf_ref, hb_ref, c_ref, n_ref):
    c = pl.program_id(1)

    @pl.when(c == 0)
    def _():
        c_ref[...] = jnp.zeros_like(c_ref)
        n_ref[...] = jnp.zeros_like(n_ref)

    row = lax.broadcasted_iota(jnp.int32, (BLK, BLK), 0)
    col = lax.broadcasted_iota(jnp.int32, (BLK, BLK), 1)
    dirs = ((qf, kf, vf, gcf, arf, hf_ref), (qb, kb, vb, gcb, arb, hb_ref))
    for d, (q_ref, k_ref, v_ref, gc_ref, ar_ref, o_ref) in enumerate(dirs):
        mask = (col <= row) if d == 0 else (col >= row)
        gc = gc_ref[...]
        for h in range(M_HEADS):
            def colq(qi):
                j = d * 40 + qi * 8 + h
                return gc[:, j:j + 1]
            a_col, m_col, e_col, mp_col, ml_col = (colq(i) for i in range(5))
            a_row = ar_ref[d * 8 + h:d * 8 + h + 1, :]
            hs = slice(h * M_HEAD_DIM, (h + 1) * M_HEAD_DIM)
            q = q_ref[:, hs]
            k = k_ref[:, hs]
            v = v_ref[:, hs]
            s = lax.dot_general(q, k, (((1,), (1,)), ((), ())), preferred_element_type=F32)
            p = jnp.where(mask, jnp.exp(a_row - m_col), 0.0)
            sp = s * p
            a = jnp.exp(mp_col - m_col)
            ct = c_ref[d * M_HEADS + h]
            nrow = n_ref[d * M_HEADS + h]
            inter = jnp.dot(q, ct.astype(BF16), preferred_element_type=F32)
            num = jnp.dot(sp.astype(BF16), v, preferred_element_type=F32) + a * inter
            qn = jnp.sum(q.astype(F32) * nrow, axis=1, keepdims=True)
            den = jnp.sum(sp, axis=1, keepdims=True) + a * qn
            o_ref[:, hs] = num / jnp.maximum(jnp.abs(den), jnp.exp(-e_col))
            w = jnp.exp(a_col - ml_col)
            dec = jnp.exp(mp_col[0:1, :] - ml_col[0:1, :])
            wv = (w * v.astype(F32)).astype(BF16)
            c_ref[d * M_HEADS + h] = dec * ct + lax.dot_general(
                k, wv, (((0,), (0,)), ((), ())), preferred_element_type=F32)
            n_ref[d * M_HEADS + h] = dec * nrow + jnp.sum(w * k.astype(F32), axis=0, keepdims=True)


def _fwd_blk(c):
    return (c + NBLK - 1) % NBLK


def _bwd_blk(c):
    return (2 * NBLK - 2 - c) % NBLK


def _mlstm(qk3, proj3, gcol, arow):
    def specs(blk):
        return [
            pl.BlockSpec((None, BLK, M_WIDTH), lambda b, c: (b, blk(c), 0)),
            pl.BlockSpec((None, BLK, M_WIDTH), lambda b, c: (b, blk(c), 1)),
            pl.BlockSpec((None, BLK, M_WIDTH), lambda b, c: (b, blk(c), C_MV // M_WIDTH)),
            pl.BlockSpec((None, BLK, BLK), lambda b, c: (b, blk(c), 0)),
            pl.BlockSpec((None, 16, BLK), lambda b, c: (b, 0, blk(c))),
        ]
    out_sd = jax.ShapeDtypeStruct((BATCH, LP, M_WIDTH), F32)
    return pl.pallas_call(
        _mlstm_kernel,
        grid=(BATCH, NBLK),
        in_specs=specs(_fwd_blk) + specs(_bwd_blk),
        out_specs=[
            pl.BlockSpec((None, BLK, M_WIDTH), lambda b, c: (b, _fwd_blk(c), 0)),
            pl.BlockSpec((None, BLK, M_WIDTH), lambda b, c: (b, _bwd_blk(c), 0)),
        ],
        out_shape=[out_sd, out_sd],
        scratch_shapes=[
            pltpu.VMEM((2 * M_HEADS, M_HEAD_DIM, M_HEAD_DIM), F32),
            pltpu.VMEM((2 * M_HEADS, 1, M_HEAD_DIM), F32),
        ],
        compiler_params=_cparams(("parallel", "arbitrary")),
        name="mlstm",
    )(qk3, qk3, proj3, gcol, arow, qk3, qk3, proj3, gcol, arow)


TQ = 384
QG = 2
RG = TQ // QG


def _attn_kernel(lam_ref, q_ref, k_ref, v_ref, g_ref, o_ref):
    lq = lam_ref[...]
    la = jnp.sum(lq[0:1] * lq[1:2], axis=1, keepdims=True)
    lb = jnp.sum(lq[2:3] * lq[3:4], axis=1, keepdims=True)
    lam = jnp.exp(la) - jnp.exp(lb) + LAM_INIT

    lane = lax.broadcasted_iota(jnp.int32, (RG, 128), 1)
    colid = lax.broadcasted_iota(jnp.int32, (1, BLK), 1)

    def scores(g):
        qg = q_ref[g * RG:(g + 1) * RG, :]
        zero = jnp.zeros_like(qg)
        q2 = jnp.concatenate([jnp.where(lane < D_QK_DIM, qg, zero), jnp.where(lane >= D_QK_DIM, qg, zero)], axis=0)
        s = lax.dot_general(q2, k_ref[...], (((1,), (1,)), ((), ())), preferred_element_type=F32)
        tail = jnp.where(colid < N_META, s[:, LP - BLK:], -jnp.inf)
        return jnp.concatenate([s[:, :LP - BLK], tail], axis=1)

    def finish(g, s):
        m = jnp.max(s, axis=1, keepdims=True)
        p = jnp.exp2(s - m)
        l = jnp.sum(p, axis=1, keepdims=True)
        ratio = lam * l[:RG] / l[RG:]
        a = (p[:RG] - ratio * p[RG:]).astype(BF16)
        o = jnp.dot(a, v_ref[...], preferred_element_type=F32) / l[:RG]
        r = lax.rsqrt(jnp.mean(o * o, axis=-1, keepdims=True) + EPS)
        o_ref[g * RG:(g + 1) * RG, :] = (((o * r) * g_ref[...]) * (1.0 - LAM_INIT)).astype(o_ref.dtype)

    s_next = scores(0)
    for g in range(QG):
        s = s_next
        if g + 1 < QG:
            s_next = scores(g + 1)
        finish(g, s)


def _attention(lam4, proj3, subln_g):
    return pl.pallas_call(
        _attn_kernel,
        grid=(BATCH, D_HEADS, LP // TQ),
        in_specs=[
            pl.BlockSpec((4, D_QK_DIM), lambda b, h, i: (0, 0)),
            pl.BlockSpec((None, TQ, 128), lambda b, h, i: (b, i, C_DQ // 128 + h)),
            pl.BlockSpec((None, LP, 128), lambda b, h, i: (b, 0, C_DK // 128 + h)),
            pl.BlockSpec((None, LP, 128), lambda b, h, i: (b, 0, C_DV // 128 + h)),
            pl.BlockSpec((1, D_V_DIM), lambda b, h, i: (0, 0)),
        ],
        out_specs=pl.BlockSpec((None, TQ, D_V_DIM), lambda b, h, i: (b, i, h)),
        out_shape=jax.ShapeDtypeStruct((BATCH, LP, D_WIDTH), BF16),
        compiler_params=_cparams(("parallel", "parallel", "arbitrary")),
        name="diff_attn",
    )(lam4, proj3, proj3, proj3, subln_g)


TM4 = 1056
TN4 = 512
LN_CH = 96


def _merge_kernel(hf_ref, hb_ref, mo_ref, ng_ref, bo_ref, wa_ref, wb_ref, gm_ref, gd_ref, o_ref, a_ref):
    n = pl.program_id(1)

    @pl.when(n == 0)
    def _():
        def body(i, carry):
            rows = pl.ds(pl.multiple_of(i * LN_CH, LN_CH), LN_CH)
            for h in range(M_HEADS):
                hs = slice(h * M_HEAD_DIM, (h + 1) * M_HEAD_DIM)
                x = hf_ref[rows, hs] + hb_ref[rows, hs]
                mu = jnp.mean(x, axis=-1, keepdims=True)
                xc = x - mu
                var = jnp.mean(xc * xc, axis=-1, keepdims=True)
                hn = (xc * lax.rsqrt(var + EPS)) * ng_ref[:, hs]
                a_ref[rows, hs] = (jax.nn.sigmoid(mo_ref[rows, hs].astype(F32)) * hn).astype(BF16)
            return carry
        lax.fori_loop(0, TM4 // LN_CH, body, 0)

    ya = jnp.dot(a_ref[...], wa_ref[...], preferred_element_type=F32)
    yb = jnp.dot(bo_ref[...], wb_ref[...], preferred_element_type=F32)
    o_ref[...] = (jax.nn.sigmoid(gm_ref[...].astype(F32)) * ya
                  + jax.nn.sigmoid(gd_ref[...].astype(F32)) * yb).astype(o_ref.dtype)


def _merge(hf, hb, proj, norm_g, b_out, wa, wb):
    return pl.pallas_call(
        _merge_kernel,
        grid=(R // TM4, D_MODEL // TN4),
        in_specs=[
            pl.BlockSpec((TM4, M_WIDTH), lambda m, n: (m, 0)),
            pl.BlockSpec((TM4, M_WIDTH), lambda m, n: (m, 0)),
            pl.BlockSpec((TM4, M_WIDTH), lambda m, n: (m, C_MO // M_WIDTH)),
            pl.BlockSpec((1, M_WIDTH), lambda m, n: (0, 0)),
            pl.BlockSpec((TM4, D_WIDTH), lambda m, n: (m, 0)),
            pl.BlockSpec((M_WIDTH, TN4), lambda m, n: (0, n)),
            pl.BlockSpec((D_WIDTH, TN4), lambda m, n: (0, n)),
            pl.BlockSpec((TM4, TN4), lambda m, n: (m, C_GM // TN4 + n)),
            pl.BlockSpec((TM4, TN4), lambda m, n: (m, C_GD // TN4 + n)),
        ],
        out_specs=pl.BlockSpec((TM4, TN4), lambda m, n: (m, n)),
        out_shape=jax.ShapeDtypeStruct((R, D_MODEL), BF16),
        scratch_shapes=[pltpu.VMEM((TM4, M_WIDTH), BF16)],
        compiler_params=_cparams(("parallel", "arbitrary")),
        name="merge",
    )(hf, hb, proj, norm_g, b_out, wa, wb, proj, proj)


TM5 = 384


def _outproj_kernel(h0_ref, mg_ref, w_ref, g2_ref, h1_ref, u2_ref):
    h1 = h0_ref[...] + jnp.dot(mg_ref[...], w_ref[...], preferred_element_type=F32)
    h1_ref[...] = h1
    r = lax.rsqrt(jnp.mean(h1 * h1, axis=-1, keepdims=True) + EPS)
    u2_ref[...] = ((h1 * r) * g2_ref[...]).astype(BF16)


def _outproj(h0, merged, w, g2):
    return pl.pallas_call(
        _outproj_kernel,
        grid=(R // TM5,),
        in_specs=[
            pl.BlockSpec((TM5, D_MODEL), lambda m: (m, 0)),
            pl.BlockSpec((TM5, D_MODEL), lambda m: (m, 0)),
            pl.BlockSpec((D_MODEL, D_MODEL), lambda m: (0, 0)),
            pl.BlockSpec((1, D_MODEL), lambda m: (0, 0)),
        ],
        out_specs=[
            pl.BlockSpec((TM5, D_MODEL), lambda m: (m, 0)),
            pl.BlockSpec((TM5, D_MODEL), lambda m: (m, 0)),
        ],
        out_shape=[
            jax.ShapeDtypeStruct((R, D_MODEL), F32),
            jax.ShapeDtypeStruct((R, D_MODEL), BF16),
        ],
        compiler_params=_cparams(("parallel",)),
        name="outproj",
    )(h0, merged, w, g2)


TMU = 1024
TNU = 512
HALO = 16
NNU = FFN_DIM // TNU


def _ffn_up_kernel(x_ref, xp_ref, xn_ref, wg_ref, wv_ref, cg_ref, cv_ref, o_ref, xs_ref, sg_ref, sv_ref):
    @pl.when(pl.program_id(2) == 0)
    def _():
        last_tile = pl.program_id(1) == SEQ // TMU - 1
        xs_ref[0:HALO, :] = xp_ref[...]
        xs_ref[HALO:HALO + TMU, :] = x_ref[...]
        xn = xn_ref[...]
        xs_ref[HALO + TMU:, :] = jnp.where(last_tile, jnp.zeros_like(xn), xn)

    def branch(w_ref, c_ref, s_ref):
        s_ref[...] = jnp.dot(xs_ref[...], w_ref[...].astype(BF16), preferred_element_type=F32)
        c = c_ref[...]
        return (s_ref[HALO - 1:HALO - 1 + TMU, :] * c[0:1] + s_ref[HALO:HALO + TMU, :] * c[1:2]
                + s_ref[HALO + 1:HALO + 1 + TMU, :] * c[2:3])

    gate = branch(wg_ref, cg_ref, sg_ref)
    val = branch(wv_ref, cv_ref, sv_ref)
    o_ref[...] = (gate * jax.nn.sigmoid(gate) * val).astype(o_ref.dtype)


def _ffn_up(u2_3, w_up, conv_w):
    hb = TMU // HALO
    meta_hblk = SEQ // HALO
    return pl.pallas_call(
        _ffn_up_kernel,
        grid=(BATCH, SEQ // TMU, NNU),
        in_specs=[
            pl.BlockSpec((None, TMU, D_MODEL), lambda b, i, n: (b, i, 0)),
            pl.BlockSpec((None, HALO, D_MODEL), lambda b, i, n: (b, jnp.where(i == 0, meta_hblk, i * hb - 1), 0)),
            pl.BlockSpec((None, HALO, D_MODEL), lambda b, i, n: (b, (i + 1) * hb, 0)),
            pl.BlockSpec((D_MODEL, TNU), lambda b, i, n: (0, n)),
            pl.BlockSpec((D_MODEL, TNU), lambda b, i, n: (0, NNU + n)),
            pl.BlockSpec((3, TNU), lambda b, i, n: (0, n)),
            pl.BlockSpec((3, TNU), lambda b, i, n: (0, NNU + n)),
        ],
        out_specs=pl.BlockSpec((None, TMU, TNU), lambda b, i, n: (b, i, n)),
        out_shape=jax.ShapeDtypeStruct((BATCH, SEQ, FFN_DIM), BF16),
        scratch_shapes=[
            pltpu.VMEM((TMU + 2 * HALO, D_MODEL), BF16),
            pltpu.VMEM((TMU + 2 * HALO, TNU), F32),
            pltpu.VMEM((TMU + 2 * HALO, TNU), F32),
        ],
        compiler_params=_cparams(("parallel", "parallel", "arbitrary")),
        name="ffn_up",
    )(u2_3, u2_3, u2_3, w_up, w_up, conv_w, conv_w)


TM7 = 512
TK7 = 1408
NK7 = FFN_DIM // TK7


def _down_kernel(a_ref, wd_ref, h1_ref, gf_ref, o_ref, acc_ref):
    k = pl.program_id(2)

    def part():
        return jnp.dot(a_ref[...], wd_ref[...], preferred_element_type=F32)

    @pl.when(k == 0)
    def _():
        acc_ref[...] = part()

    @pl.when(jnp.logical_and(k > 0, k < NK7 - 1))
    def _():
        acc_ref[...] += part()

    @pl.when(k == NK7 - 1)
    def _():
        h2 = h1_ref[...] + (acc_ref[...] + part())
        r = lax.rsqrt(jnp.mean(h2 * h2, axis=-1, keepdims=True) + EPS)
        o_ref[...] = (h2 * r) * gf_ref[...]


def _ffn_down(act, w_down, h1_3, gf):
    return pl.pallas_call(
        _down_kernel,
        grid=(BATCH, SEQ // TM7, NK7),
        in_specs=[
            pl.BlockSpec((None, TM7, TK7), lambda b, i, k: (b, i, k)),
            pl.BlockSpec((TK7, D_MODEL), lambda b, i, k: (k, 0)),
            pl.BlockSpec((None, TM7, D_MODEL), lambda b, i, k: (b, i, 0)),
            pl.BlockSpec((1, D_MODEL), lambda b, i, k: (0, 0)),
        ],
        out_specs=pl.BlockSpec((None, TM7, D_MODEL), lambda b, i, k: (b, i, 0)),
        out_shape=jax.ShapeDtypeStruct((BATCH, SEQ, D_MODEL), F32),
        scratch_shapes=[pltpu.VMEM((TM7, D_MODEL), F32)],
        compiler_params=_cparams(("parallel", "parallel", "arbitrary")),
        name="ffn_down",
    )(act, w_down, h1_3, gf)


def kernel(x, meta_tokens, norm1_g, w_in, mlstm_conv_w, mlstm_gate_bias, mlstm_norm_g, lambda_q1, lambda_k1,
           lambda_q2, lambda_k2, diff_subln_g, w_branch_m, w_branch_d, w_out, norm2_g, w_up, ffn_conv_w,
           w_down, norm_f_g):
    assert x.shape == (BATCH, SEQ, D_MODEL) and w_in.shape[0] == 1
    meta = jnp.broadcast_to(meta_tokens[None].astype(x.dtype), (BATCH, N_META, D_MODEL))
    pad = jnp.zeros((BATCH, LP - SEQ - N_META, D_MODEL), x.dtype)
    h0 = jnp.concatenate([x, meta, pad], axis=1)
    h0f = h0.reshape(R, D_MODEL)

    wit = w_in[0].T
    w_pa = wit[:GATE_COL0].astype(BF16)
    w_pb = wit[GATE_COL0 + GROWS:].astype(BF16)
    gbias = mlstm_gate_bias[0].reshape(GROWS, 1).astype(F32)
    g1 = norm1_g[0].reshape(1, D_MODEL)

    proj = _inproj(h0f, g1, w_pa, w_pb, _rope_tables())
    proj3 = proj.reshape(BATCH, LP, PROJ_W)

    gt = _gates(h0, g1, wit, gbias)
    arow, gcol = _scans(gt)

    kscale = jnp.concatenate([jnp.ones((1, M_WIDTH), F32), jnp.full((1, M_WIDTH), M_HEAD_DIM ** -0.5, F32)], axis=1)
    qk3 = _qkconv(proj3, mlstm_conv_w[0], kscale)
    hf, hb = _mlstm(qk3, proj3, gcol, arow)

    lam4 = jnp.stack([lambda_q1[0], lambda_k1[0], lambda_q2[0], lambda_k2[0]]).astype(F32)
    b_out = _attention(lam4, proj3, diff_subln_g[0].reshape(1, D_V_DIM))

    merged = _merge(hf.reshape(R, M_WIDTH), hb.reshape(R, M_WIDTH), proj, mlstm_norm_g[0].reshape(1, M_WIDTH),
                    b_out.reshape(R, D_WIDTH), w_branch_m[0].astype(BF16), w_branch_d[0].astype(BF16))
    h1, u2 = _outproj(h0f, merged, w_out[0].astype(BF16), norm2_g[0].reshape(1, D_MODEL))

    act = _ffn_up(u2.reshape(BATCH, LP, D_MODEL), w_up[0], ffn_conv_w[0])
    return _ffn_down(act, w_down[0].astype(BF16), h1.reshape(BATCH, LP, D_MODEL), norm_f_g.reshape(1, D_MODEL))
```

```python
import functools
import math

import numpy as np
import jax
import jax.numpy as jnp
from jax import lax
from jax.experimental import pallas as pl
from jax.experimental.pallas import tpu as pltpu

F32 = jnp.float32
BF16 = jnp.bfloat16

D_MODEL = 2048
BATCH = 2
SEQ = 4096
N_META = 16
EPS = 1e-6
M_HEADS = 4
M_WIDTH = 1024
M_HEAD_DIM = 256
D_HEADS = 8
D_V_DIM = 128
D_QK_DIM = 64
D_WIDTH = 1024
ROPE_THETA = 500000.0
ROPE_DIM = 16
FFN_DIM = 5632
LAM_INIT = 0.8 - 0.6 * math.exp(-0.3 * 0)

BLK = 128
LP = SEQ + BLK
NBLK = LP // BLK
META_BLK = SEQ // BLK
R = BATCH * LP
NEG = -1e30

C_MQK, C_MV, C_MO, C_DQ, C_DK, C_DV, C_GM, C_GD = 0, 2048, 3072, 4096, 5120, 6144, 7168, 9216
PROJ_W = 11264

VMEM_LIMIT = 56 * 1024 * 1024


def _cparams(sem):
    return pltpu.CompilerParams(dimension_semantics=sem, vmem_limit_bytes=VMEM_LIMIT)


def _sigmoid(x):
    return 0.5 * jnp.tanh(0.5 * x) + 0.5


TM1 = 1056
TN1 = 1024
NORM_CH = 96


def _rmsnorm_rows_to(h_ref, g_ref, u_ref, tm):
    def body(i, carry):
        rows = pl.ds(pl.multiple_of(i * NORM_CH, NORM_CH), NORM_CH)
        x = h_ref[rows, :]
        r = lax.rsqrt(jnp.mean(x * x, axis=-1, keepdims=True) + EPS)
        u_ref[rows, :] = ((x * r) * g_ref[...]).astype(BF16)
        return carry
    lax.fori_loop(0, tm // NORM_CH, body, 0)


def _rope_tile(acc, c, s):
    lane = lax.broadcasted_iota(jnp.int32, c.shape, 1)
    first = (lane % D_QK_DIM) < (ROPE_DIM // 2)
    outs = []
    for g in range(acc.shape[1] // 128):
        y = acc[:, g * 128:(g + 1) * 128]
        rot = jnp.where(first, pltpu.roll(y, 128 - ROPE_DIM // 2, 1), pltpu.roll(y, ROPE_DIM // 2, 1))
        outs.append(y * c + rot * s)
    return jnp.concatenate(outs, axis=1)


NA1 = C_DQ // TN1
GROWS = 16
GATE_COL0 = 4 * M_WIDTH
GT_PAD = -(-TM1 // 128) * 128


def _inproj_kernel(h_ref, g_ref, wa_ref, wb_ref, wg_ref, gb_ref, cq_ref, sq_ref, ck_ref, sk_ref,
                   o_ref, gt_ref, u_ref):
    n = pl.program_id(1)

    @pl.when(n == 0)
    def _():
        _rmsnorm_rows_to(h_ref, g_ref, u_ref, TM1)
        gt = lax.dot_general(wg_ref[...].astype(BF16), u_ref[...], (((1,), (1,)), ((), ())),
                             preferred_element_type=F32)
        gt_ref[:, :TM1] = gt + gb_ref[...]
        gt_ref[:, TM1:] = jnp.zeros((GROWS, GT_PAD - TM1), F32)

    def mm(wt_ref):
        return lax.dot_general(u_ref[...], wt_ref[...], (((1,), (1,)), ((), ())), preferred_element_type=F32)

    q0, k0, v0 = C_DQ // TN1, C_DK // TN1, C_DV // TN1

    @pl.when(n < NA1)
    def _():
        o_ref[...] = mm(wa_ref).astype(o_ref.dtype)

    @pl.when(jnp.logical_and(n >= q0, n < k0))
    def _():
        o_ref[...] = _rope_tile(mm(wb_ref), cq_ref[...], sq_ref[...]).astype(o_ref.dtype)

    @pl.when(jnp.logical_and(n >= k0, n < v0))
    def _():
        o_ref[...] = _rope_tile(mm(wb_ref), ck_ref[...], sk_ref[...]).astype(o_ref.dtype)

    @pl.when(n >= v0)
    def _():
        o_ref[...] = mm(wb_ref).astype(o_ref.dtype)


def _inproj(h, g, wa, wb, w_in_t, gbias, tabs):
    tiles_per_batch = LP // TM1
    tab_spec = pl.BlockSpec((TM1, 128), lambda m, n: (m % tiles_per_batch, 0))
    return pl.pallas_call(
        _inproj_kernel,
        grid=(R // TM1, PROJ_W // TN1),
        in_specs=[
            pl.BlockSpec((TM1, D_MODEL), lambda m, n: (m, 0)),
            pl.BlockSpec((1, D_MODEL), lambda m, n: (0, 0)),
            pl.BlockSpec((TN1, D_MODEL), lambda m, n: (jnp.minimum(n, NA1 - 1), 0)),
            pl.BlockSpec((TN1, D_MODEL), lambda m, n: (jnp.maximum(n - NA1, 0), 0)),
            pl.BlockSpec((GROWS, D_MODEL), lambda m, n: (GATE_COL0 // GROWS, 0)),
            pl.BlockSpec((GROWS, 1), lambda m, n: (0, 0)),
            tab_spec, tab_spec, tab_spec, tab_spec,
        ],
        out_specs=[
            pl.BlockSpec((TM1, TN1), lambda m, n: (m, n)),
            pl.BlockSpec((None, GROWS, GT_PAD), lambda m, n: (m, 0, 0)),
        ],
        out_shape=[
            jax.ShapeDtypeStruct((R, PROJ_W), BF16),
            jax.ShapeDtypeStruct((R // TM1, GROWS, GT_PAD), F32),
        ],
        scratch_shapes=[pltpu.VMEM((TM1, D_MODEL), BF16)],
        compiler_params=_cparams(("parallel", "arbitrary")),
        name="inproj",
    )(h, g, wa, wb, w_in_t, gbias, *tabs)


def _rope_tables():
    pos = np.zeros((LP,), np.float64)
    pos[:SEQ] = np.arange(SEQ) + N_META
    pos[SEQ:SEQ + N_META] = np.arange(N_META)
    half = ROPE_DIM // 2
    inv = ROPE_THETA ** (-np.arange(0, ROPE_DIM, 2, dtype=np.float64) / ROPE_DIM)
    ang = pos[:, None] * inv[None, :]
    cos64 = np.ones((LP, D_QK_DIM)); sin64 = np.zeros((LP, D_QK_DIM))
    cos64[:, :half] = np.cos(ang); cos64[:, half:ROPE_DIM] = np.cos(ang)
    sin64[:, :half] = -np.sin(ang); sin64[:, half:ROPE_DIM] = np.sin(ang)
    cos = np.concatenate([cos64, cos64], axis=1)
    sin = np.concatenate([sin64, sin64], axis=1)
    qs = D_QK_DIM ** -0.5 * math.log2(math.e)
    return [jnp.asarray(t, F32) for t in (cos * qs, sin * qs, cos, sin)]


def _lane_scan(x, op, ident, reverse):
    lane = lax.broadcasted_iota(jnp.int32, x.shape, 1)
    k = 1
    while k < BLK:
        if reverse:
            sh = pltpu.roll(x, BLK - k, 1)
            x = op(x, jnp.where(lane < BLK - k, sh, ident))
        else:
            sh = pltpu.roll(x, k, 1)
            x = op(x, jnp.where(lane >= k, sh, ident))
        k *= 2
    return x


def _log_sigmoid(x):
    return jnp.minimum(x, 0.0) - jnp.log1p(jnp.exp(-jnp.abs(x)))


def _scan_kernel(g_ref, arow_ref, gcol_ref, colt_ref):
    colt_ref[...] = jnp.zeros_like(colt_ref)
    lane = lax.broadcasted_iota(jnp.int32, (8, BLK), 1)
    for d in range(2):
        reverse = d == 1
        carry_b = jnp.zeros((8, BLK), F32)
        carry_m = jnp.full((8, BLK), NEG, F32)
        logical = [META_BLK] + list(range(META_BLK))
        order = logical[::-1] if reverse else logical
        edge = 0 if reverse else BLK - 1
        for j in order:
            cols = slice(j * BLK, (j + 1) * BLK)
            li = g_ref[d * 8:(d + 1) * 8, cols]
            lf = pltpu.roll(_log_sigmoid(li), M_HEADS, 0)
            if j == META_BLK:
                valid = lane < N_META
                li = jnp.where(valid, li, NEG)
                lf = jnp.where(valid, lf, 0.0)
            bc = carry_b + _lane_scan(lf, jnp.add, 0.0, reverse)
            a = li - bc
            m = jnp.maximum(carry_m, _lane_scan(a, jnp.maximum, NEG, reverse))
            m_last = jnp.broadcast_to(m[:, edge:edge + 1], (8, BLK))
            arow_ref[d * 8:(d + 1) * 8, cols] = a
            base = d * 40
            colt_ref[j, base:base + 8, :] = a
            colt_ref[j, base + 8:base + 16, :] = m
            colt_ref[j, base + 16:base + 24, :] = bc + m
            colt_ref[j, base + 24:base + 32, :] = carry_m
            colt_ref[j, base + 32:base + 40, :] = m_last
            carry_b = jnp.broadcast_to(bc[:, edge:edge + 1], (8, BLK))
            carry_m = m_last
    for j in range(NBLK):
        gcol_ref[j * BLK:(j + 1) * BLK, :] = colt_ref[j].T


def _scans(gt):
    return pl.pallas_call(
        _scan_kernel,
        grid=(BATCH,),
        in_specs=[pl.BlockSpec((None, GROWS, LP), lambda b: (b, 0, 0))],
        out_specs=[
            pl.BlockSpec((None, 16, LP), lambda b: (b, 0, 0)),
            pl.BlockSpec((None, LP, BLK), lambda b: (b, 0, 0)),
        ],
        out_shape=[
            jax.ShapeDtypeStruct((BATCH, 16, LP), F32),
            jax.ShapeDtypeStruct((BATCH, LP, BLK), F32),
        ],
        scratch_shapes=[pltpu.VMEM((NBLK, BLK, BLK), F32)],
        compiler_params=_cparams(("parallel",)),
        name="gate_scans",
    )(gt)


TC_COLS = 256
CONV_CH = 256


def _seq_conv_rows(s_ref, w, lo, n):
    return (s_ref[lo - 1:lo - 1 + n, :] * w[0:1] + s_ref[lo:lo + n, :] * w[1:2]
            + s_ref[lo + 1:lo + 1 + n, :] * w[2:3])


def _qkconv_kernel(x_ref, w_ref, sc_ref, o_ref, s_ref):
    xo = 8 + N_META
    s_ref[0:8, :] = jnp.zeros((8, TC_COLS), F32)
    s_ref[xo + SEQ:xo + SEQ + 8, :] = jnp.zeros((8, TC_COLS), F32)
    s_ref[8:xo, :] = x_ref[SEQ:SEQ + N_META, :].astype(F32)
    for i in range(SEQ // CONV_CH):
        s_ref[xo + i * CONV_CH:xo + (i + 1) * CONV_CH, :] = x_ref[i * CONV_CH:(i + 1) * CONV_CH, :].astype(F32)
    w = w_ref[...]
    sc = sc_ref[...]

    def act(y):
        return (y * _sigmoid(y) * sc).astype(o_ref.dtype)

    for i in range(SEQ // CONV_CH):
        o_ref[i * CONV_CH:(i + 1) * CONV_CH, :] = act(_seq_conv_rows(s_ref, w, xo + i * CONV_CH, CONV_CH))
    o_ref[SEQ:SEQ + N_META, :] = act(_seq_conv_rows(s_ref, w, 8, N_META))
    o_ref[SEQ + N_META:LP, :] = jnp.zeros((LP - SEQ - N_META, TC_COLS), o_ref.dtype)


def _qkconv(proj3, w, scale):
    return pl.pallas_call(
        _qkconv_kernel,
        grid=(BATCH, 2 * M_WIDTH // TC_COLS),
        in_specs=[
            pl.BlockSpec((None, LP, TC_COLS), lambda b, c: (b, 0, C_MQK // TC_COLS + c)),
            pl.BlockSpec((3, TC_COLS), lambda b, c: (0, c)),
            pl.BlockSpec((1, TC_COLS), lambda b, c: (0, c)),
        ],
        out_specs=pl.BlockSpec((None, LP, TC_COLS), lambda b, c: (b, 0, c)),
        out_shape=jax.ShapeDtypeStruct((BATCH, LP, 2 * M_WIDTH), BF16),
        scratch_shapes=[pltpu.VMEM((SEQ + N_META + 16, TC_COLS), F32)],
        compiler_params=_cparams(("parallel", "parallel")),
        name="qk_conv",
    )(proj3, w, scale)


def _mlstm_kernel(qf, kf, vf, gcf, arf, qb, kb, vb, gcb, arb, hf_ref, hb_ref, c_ref, n_ref):
    c = pl.program_id(1)

    @pl.when(c == 0)
    def _():
        c_ref[...] = jnp.zeros_like(c_ref)
        n_ref[...] = jnp.zeros_like(n_ref)

    row = lax.broadcasted_iota(jnp.int32, (BLK, BLK), 0)
    col = lax.broadcasted_iota(jnp.int32, (BLK, BLK), 1)
    dirs = ((qf, kf, vf, gcf, arf, hf_ref), (qb, kb, vb, gcb, arb, hb_ref))
    for d, (q_ref, k_ref, v_ref, gc_ref, ar_ref, o_ref) in enumerate(dirs):
        mask = (col <= row) if d == 0 else (col >= row)
        gc = gc_ref[...]
        for h in range(M_HEADS):
            def colq(qi):
                j = d * 40 + qi * 8 + h
                return gc[:, j:j + 1]
            a_col, m_col, e_col, mp_col, ml_col = (colq(i) for i in range(5))
            a_row = ar_ref[d * 8 + h:d * 8 + h + 1, :]
            hs = slice(h * M_HEAD_DIM, (h + 1) * M_HEAD_DIM)
            q = q_ref[:, hs]
            k = k_ref[:, hs]
            v = v_ref[:, hs]
            s = lax.dot_general(q, k, (((1,), (1,)), ((), ())), preferred_element_type=F32)
            p = jnp.where(mask, jnp.exp(a_row - m_col), 0.0)
            sp = s * p
            a = jnp.exp(mp_col - m_col)
            ct = c_ref[d * M_HEADS + h]
            nrow = n_ref[d * M_HEADS + h]
            inter = jnp.dot(q, ct.astype(BF16), preferred_element_type=F32)
            num = jnp.dot(sp.astype(BF16), v, preferred_element_type=F32) + a * inter
            qn = jnp.sum(q.astype(F32) * nrow, axis=1, keepdims=True)
            den = jnp.sum(sp, axis=1, keepdims=True) + a * qn
            o_ref[:, hs] = num / jnp.maximum(jnp.abs(den), jnp.exp(-e_col))
            w = jnp.exp(a_col - ml_col)
            dec = jnp.exp(mp_col[0:1, :] - ml_col[0:1, :])
            wv = (w * v.astype(F32)).astype(BF16)
            c_ref[d * M_HEADS + h] = dec * ct + lax.dot_general(
                k, wv, (((0,), (0,)), ((), ())), preferred_element_type=F32)
            n_ref[d * M_HEADS + h] = dec * nrow + jnp.sum(w * k.astype(F32), axis=0, keepdims=True)


def _fwd_blk(c):
    return (c + NBLK - 1) % NBLK


def _bwd_blk(c):
    return (2 * NBLK - 2 - c) % NBLK


def _mlstm(qk3, proj3, gcol, arow):
    def specs(blk):
        return [
            pl.BlockSpec((None, BLK, M_WIDTH), lambda b, c: (b, blk(c), 0)),
            pl.BlockSpec((None, BLK, M_WIDTH), lambda b, c: (b, blk(c), 1)),
            pl.BlockSpec((None, BLK, M_WIDTH), lambda b, c: (b, blk(c), C_MV // M_WIDTH)),
            pl.BlockSpec((None, BLK, BLK), lambda b, c: (b, blk(c), 0)),
            pl.BlockSpec((None, 16, BLK), lambda b, c: (b, 0, blk(c))),
        ]
    out_sd = jax.ShapeDtypeStruct((BATCH, LP, M_WIDTH), F32)
    return pl.pallas_call(
        _mlstm_kernel,
        grid=(BATCH, NBLK),
        in_specs=specs(_fwd_blk) + specs(_bwd_blk),
        out_specs=[
            pl.BlockSpec((None, BLK, M_WIDTH), lambda b, c: (b, _fwd_blk(c), 0)),
            pl.BlockSpec((None, BLK, M_WIDTH), lambda b, c: (b, _bwd_blk(c), 0)),
        ],
        out_shape=[out_sd, out_sd],
        scratch_shapes=[
            pltpu.VMEM((2 * M_HEADS, M_HEAD_DIM, M_HEAD_DIM), F32),
            pltpu.VMEM((2 * M_HEADS, 1, M_HEAD_DIM), F32),
        ],
        compiler_params=_cparams(("parallel", "arbitrary")),
        name="mlstm",
    )(qk3, qk3, proj3, gcol, arow, qk3, qk3, proj3, gcol, arow)


TQ = 384
QG = 2
RG = TQ // QG


def _attn_kernel(lam_ref, q_ref, k_ref, v_ref, g_ref, o_ref):
    lq = lam_ref[...]
    la = jnp.sum(lq[0:1] * lq[1:2], axis=1, keepdims=True)
    lb = jnp.sum(lq[2:3] * lq[3:4], axis=1, keepdims=True)
    lam = jnp.exp(la) - jnp.exp(lb) + LAM_INIT

    lane = lax.broadcasted_iota(jnp.int32, (RG, 128), 1)
    colid = lax.broadcasted_iota(jnp.int32, (1, BLK), 1)

    def scores(g):
        qg = q_ref[g * RG:(g + 1) * RG, :]
        zero = jnp.zeros_like(qg)
        q2 = jnp.concatenate([jnp.where(lane < D_QK_DIM, qg, zero), jnp.where(lane >= D_QK_DIM, qg, zero)], axis=0)
        s = lax.dot_general(q2, k_ref[...], (((1,), (1,)), ((), ())), preferred_element_type=F32)
        tail = jnp.where(colid < N_META, s[:, LP - BLK:], -jnp.inf)
        return jnp.concatenate([s[:, :LP - BLK], tail], axis=1)

    def finish(g, s):
        m = jnp.max(s, axis=1, keepdims=True)
        p = jnp.exp2(s - m)
        l = jnp.sum(p, axis=1, keepdims=True)
        ratio = lam * l[:RG] / l[RG:]
        a = (p[:RG] - ratio * p[RG:]).astype(BF16)
        o = jnp.dot(a, v_ref[...], preferred_element_type=F32) / l[:RG]
        r = lax.rsqrt(jnp.mean(o * o, axis=-1, keepdims=True) + EPS)
        o_ref[g * RG:(g + 1) * RG, :] = (((o * r) * g_ref[...]) * (1.0 - LAM_INIT)).astype(o_ref.dtype)

    s_next = scores(0)
    for g in range(QG):
        s = s_next
        if g + 1 < QG:
            s_next = scores(g + 1)
        finish(g, s)


def _attention(lam4, proj3, subln_g):
    return pl.pallas_call(
        _attn_kernel,
        grid=(BATCH, D_HEADS, LP // TQ),
        in_specs=[
            pl.BlockSpec((4, D_QK_DIM), lambda b, h, i: (0, 0)),
            pl.BlockSpec((None, TQ, 128), lambda b, h, i: (b, i, C_DQ // 128 + h)),
            pl.BlockSpec((None, LP, 128), lambda b, h, i: (b, 0, C_DK // 128 + h)),
            pl.BlockSpec((None, LP, 128), lambda b, h, i: (b, 0, C_DV // 128 + h)),
            pl.BlockSpec((1, D_V_DIM), lambda b, h, i: (0, 0)),
        ],
        out_specs=pl.BlockSpec((None, TQ, D_V_DIM), lambda b, h, i: (b, i, h)),
        out_shape=jax.ShapeDtypeStruct((BATCH, LP, D_WIDTH), BF16),
        compiler_params=_cparams(("parallel", "parallel", "arbitrary")),
        name="diff_attn",
    )(lam4, proj3, proj3, proj3, subln_g)


TM4 = 1056
TN4 = 512
LN_CH = 96


def _merge_kernel(hf_ref, hb_ref, mo_ref, ng_ref, bo_ref, wa_ref, wb_ref, gm_ref, gd_ref, o_ref, a_ref):
    n = pl.program_id(1)

    @pl.when(n == 0)
    def _():
        def body(i, carry):
            rows = pl.ds(pl.multiple_of(i * LN_CH, LN_CH), LN_CH)
            for h in range(M_HEADS):
                hs = slice(h * M_HEAD_DIM, (h + 1) * M_HEAD_DIM)
                x = hf_ref[rows, hs] + hb_ref[rows, hs]
                mu = jnp.mean(x, axis=-1, keepdims=True)
                xc = x - mu
                var = jnp.mean(xc * xc, axis=-1, keepdims=True)
                hn = (xc * lax.rsqrt(var + EPS)) * ng_ref[:, hs]
                a_ref[rows, hs] = (_sigmoid(mo_ref[rows, hs].astype(F32)) * hn).astype(BF16)
            return carry
        lax.fori_loop(0, TM4 // LN_CH, body, 0)

    ya = jnp.dot(a_ref[...], wa_ref[...], preferred_element_type=F32)
    yb = jnp.dot(bo_ref[...], wb_ref[...], preferred_element_type=F32)
    o_ref[...] = (_sigmoid(gm_ref[...].astype(F32)) * ya
                  + _sigmoid(gd_ref[...].astype(F32)) * yb).astype(o_ref.dtype)


def _merge(hf, hb, proj, norm_g, b_out, wa, wb):
    return pl.pallas_call(
        _merge_kernel,
        grid=(R // TM4, D_MODEL // TN4),
        in_specs=[
            pl.BlockSpec((TM4, M_WIDTH), lambda m, n: (m, 0)),
            pl.BlockSpec((TM4, M_WIDTH), lambda m, n: (m, 0)),
            pl.BlockSpec((TM4, M_WIDTH), lambda m, n: (m, C_MO // M_WIDTH)),
            pl.BlockSpec((1, M_WIDTH), lambda m, n: (0, 0)),
            pl.BlockSpec((TM4, D_WIDTH), lambda m, n: (m, 0)),
            pl.BlockSpec((M_WIDTH, TN4), lambda m, n: (0, n)),
            pl.BlockSpec((D_WIDTH, TN4), lambda m, n: (0, n)),
            pl.BlockSpec((TM4, TN4), lambda m, n: (m, C_GM // TN4 + n)),
            pl.BlockSpec((TM4, TN4), lambda m, n: (m, C_GD // TN4 + n)),
        ],
        out_specs=pl.BlockSpec((TM4, TN4), lambda m, n: (m, n)),
        out_shape=jax.ShapeDtypeStruct((R, D_MODEL), BF16),
        scratch_shapes=[pltpu.VMEM((TM4, M_WIDTH), BF16)],
        compiler_params=_cparams(("parallel", "arbitrary")),
        name="merge",
    )(hf, hb, proj, norm_g, b_out, wa, wb, proj, proj)


TM5 = 384


def _outproj_kernel(h0_ref, mg_ref, w_ref, g2_ref, h1_ref, u2_ref):
    h1 = h0_ref[...] + jnp.dot(mg_ref[...], w_ref[...], preferred_element_type=F32)
    h1_ref[...] = h1
    r = lax.rsqrt(jnp.mean(h1 * h1, axis=-1, keepdims=True) + EPS)
    u2_ref[...] = ((h1 * r) * g2_ref[...]).astype(BF16)


def _outproj(h0, merged, w, g2):
    return pl.pallas_call(
        _outproj_kernel,
        grid=(R // TM5,),
        in_specs=[
            pl.BlockSpec((TM5, D_MODEL), lambda m: (m, 0)),
            pl.BlockSpec((TM5, D_MODEL), lambda m: (m, 0)),
            pl.BlockSpec((D_MODEL, D_MODEL), lambda m: (0, 0)),
            pl.BlockSpec((1, D_MODEL), lambda m: (0, 0)),
        ],
        out_specs=[
            pl.BlockSpec((TM5, D_MODEL), lambda m: (m, 0)),
            pl.BlockSpec((TM5, D_MODEL), lambda m: (m, 0)),
        ],
        out_shape=[
            jax.ShapeDtypeStruct((R, D_MODEL), F32),
            jax.ShapeDtypeStruct((R, D_MODEL), BF16),
        ],
        compiler_params=_cparams(("parallel",)),
        name="outproj",
    )(h0, merged, w, g2)


TMU = 1024
TNU = 512
HALO = 16
NNU = FFN_DIM // TNU


def _ffn_up_kernel(x_ref, xp_ref, xn_ref, wg_ref, wv_ref, cg_ref, cv_ref, o_ref, xs_ref, sg_ref, sv_ref):
    @pl.when(pl.program_id(2) == 0)
    def _():
        last_tile = pl.program_id(1) == SEQ // TMU - 1
        xs_ref[0:HALO, :] = xp_ref[...]
        xs_ref[HALO:HALO + TMU, :] = x_ref[...]
        xn = xn_ref[...]
        xs_ref[HALO + TMU:, :] = jnp.where(last_tile, jnp.zeros_like(xn), xn)

    def branch(w_ref, c_ref, s_ref):
        s_ref[...] = jnp.dot(xs_ref[...], w_ref[...].astype(BF16), preferred_element_type=F32)
        c = c_ref[...]
        return (s_ref[HALO - 1:HALO - 1 + TMU, :] * c[0:1] + s_ref[HALO:HALO + TMU, :] * c[1:2]
                + s_ref[HALO + 1:HALO + 1 + TMU, :] * c[2:3])

    gate = branch(wg_ref, cg_ref, sg_ref)
    val = branch(wv_ref, cv_ref, sv_ref)
    o_ref[...] = (gate * _sigmoid(gate) * val).astype(o_ref.dtype)


def _ffn_up(u2_3, w_up, conv_w):
    hb = TMU // HALO
    meta_hblk = SEQ // HALO
    return pl.pallas_call(
        _ffn_up_kernel,
        grid=(BATCH, SEQ // TMU, NNU),
        in_specs=[
            pl.BlockSpec((None, TMU, D_MODEL), lambda b, i, n: (b, i, 0)),
            pl.BlockSpec((None, HALO, D_MODEL), lambda b, i, n: (b, jnp.where(i == 0, meta_hblk, i * hb - 1), 0)),
            pl.BlockSpec((None, HALO, D_MODEL), lambda b, i, n: (b, (i + 1) * hb, 0)),
            pl.BlockSpec((D_MODEL, TNU), lambda b, i, n: (0, n)),
            pl.BlockSpec((D_MODEL, TNU), lambda b, i, n: (0, NNU + n)),
            pl.BlockSpec((3, TNU), lambda b, i, n: (0, n)),
            pl.BlockSpec((3, TNU), lambda b, i, n: (0, NNU + n)),
        ],
        out_specs=pl.BlockSpec((None, TMU, TNU), lambda b, i, n: (b, i, n)),
        out_shape=jax.ShapeDtypeStruct((BATCH, SEQ, FFN_DIM), BF16),
        scratch_shapes=[
            pltpu.VMEM((TMU + 2 * HALO, D_MODEL), BF16),
            pltpu.VMEM((TMU + 2 * HALO, TNU), F32),
            pltpu.VMEM((TMU + 2 * HALO, TNU), F32),
        ],
        compiler_params=_cparams(("parallel", "parallel", "arbitrary")),
        name="ffn_up",
    )(u2_3, u2_3, u2_3, w_up, w_up, conv_w, conv_w)


TM7 = 512
TK7 = 1408
NK7 = FFN_DIM // TK7


def _down_kernel(a_ref, wd_ref, h1_ref, gf_ref, o_ref, acc_ref):
    k = pl.program_id(2)

    def part():
        return jnp.dot(a_ref[...], wd_ref[...], preferred_element_type=F32)

    @pl.when(k == 0)
    def _():
        acc_ref[...] = part()

    @pl.when(jnp.logical_and(k > 0, k < NK7 - 1))
    def _():
        acc_ref[...] += part()

    @pl.when(k == NK7 - 1)
    def _():
        h2 = h1_ref[...] + (acc_ref[...] + part())
        r = lax.rsqrt(jnp.mean(h2 * h2, axis=-1, keepdims=True) + EPS)
        o_ref[...] = (h2 * r) * gf_ref[...]


def _ffn_down(act, w_down, h1_3, gf):
    return pl.pallas_call(
        _down_kernel,
        grid=(BATCH, SEQ // TM7, NK7),
        in_specs=[
            pl.BlockSpec((None, TM7, TK7), lambda b, i, k: (b, i, k)),
            pl.BlockSpec((TK7, D_MODEL), lambda b, i, k: (k, 0)),
            pl.BlockSpec((None, TM7, D_MODEL), lambda b, i, k: (b, i, 0)),
            pl.BlockSpec((1, D_MODEL), lambda b, i, k: (0, 0)),
        ],
        out_specs=pl.BlockSpec((None, TM7, D_MODEL), lambda b, i, k: (b, i, 0)),
        out_shape=jax.ShapeDtypeStruct((BATCH, SEQ, D_MODEL), F32),
        scratch_shapes=[pltpu.VMEM((TM7, D_MODEL), F32)],
        compiler_params=_cparams(("parallel", "parallel", "arbitrary")),
        name="ffn_down",
    )(act, w_down, h1_3, gf)


def kernel(x, meta_tokens, norm1_g, w_in, mlstm_conv_w, mlstm_gate_bias, mlstm_norm_g, lambda_q1, lambda_k1,
           lambda_q2, lambda_k2, diff_subln_g, w_branch_m, w_branch_d, w_out, norm2_g, w_up, ffn_conv_w,
           w_down, norm_f_g):
    assert x.shape == (BATCH, SEQ, D_MODEL) and w_in.shape[0] == 1
    meta = jnp.broadcast_to(meta_tokens[None].astype(x.dtype), (BATCH, N_META, D_MODEL))
    pad = jnp.zeros((BATCH, LP - SEQ - N_META, D_MODEL), x.dtype)
    h0 = jnp.concatenate([x, meta, pad], axis=1)
    h0f = h0.reshape(R, D_MODEL)

    wit = w_in[0].T
    w_pa = wit[:GATE_COL0].astype(BF16)
    w_pb = wit[GATE_COL0 + GROWS:].astype(BF16)
    gbias = mlstm_gate_bias[0].reshape(GROWS, 1).astype(F32)
    g1 = norm1_g[0].reshape(1, D_MODEL)

    proj, gt_tiles = _inproj(h0f, g1, w_pa, w_pb, wit, gbias, _rope_tables())
    proj3 = proj.reshape(BATCH, LP, PROJ_W)

    gt = gt_tiles[:, :, :TM1].reshape(BATCH, LP // TM1, GROWS, TM1).transpose(0, 2, 1, 3).reshape(BATCH, GROWS, LP)
    arow, gcol = _scans(gt)

    kscale = jnp.concatenate([jnp.ones((1, M_WIDTH), F32), jnp.full((1, M_WIDTH), M_HEAD_DIM ** -0.5, F32)], axis=1)
    qk3 = _qkconv(proj3, mlstm_conv_w[0], kscale)
    hf, hb = _mlstm(qk3, proj3, gcol, arow)

    lam4 = jnp.stack([lambda_q1[0], lambda_k1[0], lambda_q2[0], lambda_k2[0]]).astype(F32)
    b_out = _attention(lam4, proj3, diff_subln_g[0].reshape(1, D_V_DIM))

    merged = _merge(hf.reshape(R, M_WIDTH), hb.reshape(R, M_WIDTH), proj, mlstm_norm_g[0].reshape(1, M_WIDTH),
                    b_out.reshape(R, D_WIDTH), w_branch_m[0].astype(BF16), w_branch_d[0].astype(BF16))
    h1, u2 = _outproj(h0f, merged, w_out[0].astype(BF16), norm2_g[0].reshape(1, D_MODEL))

    act = _ffn_up(u2.reshape(BATCH, LP, D_MODEL), w_up[0], ffn_conv_w[0])
    return _ffn_down(act, w_down[0].astype(BF16), h1.reshape(BATCH, LP, D_MODEL), norm_f_g.reshape(1, D_MODEL))
```

```python
import functools
import math

import numpy as np
import jax
import jax.numpy as jnp
from jax import lax
from jax.experimental import pallas as pl
from jax.experimental.pallas import tpu as pltpu

F32 = jnp.float32
BF16 = jnp.bfloat16

D_MODEL = 2048
BATCH = 2
SEQ = 4096
N_META = 16
EPS = 1e-6
M_HEADS = 4
M_WIDTH = 1024
M_HEAD_DIM = 256
D_HEADS = 8
D_V_DIM = 128
D_QK_DIM = 64
D_WIDTH = 1024
ROPE_THETA = 500000.0
ROPE_DIM = 16
FFN_DIM = 5632
LAM_INIT = 0.8 - 0.6 * math.exp(-0.3 * 0)

BLK = 128
LP = SEQ + BLK
NBLK = LP // BLK
META_BLK = SEQ // BLK
R = BATCH * LP
NEG = -1e30

C_MQK, C_MV, C_MO, C_DQ, C_DK, C_DV, C_GM, C_GD = 0, 2048, 3072, 4096, 5120, 6144, 7168, 9216
PROJ_W = 11264

VMEM_LIMIT = 56 * 1024 * 1024


def _cparams(sem):
    return pltpu.CompilerParams(dimension_semantics=sem, vmem_limit_bytes=VMEM_LIMIT)


def _sigmoid(x):
    return 0.5 * jnp.tanh(0.5 * x) + 0.5


TM1 = 1056
TN1 = 1024
NORM_CH = 96


def _rmsnorm_rows_to(h_ref, g_ref, u_ref, tm):
    def body(i, carry):
        rows = pl.ds(pl.multiple_of(i * NORM_CH, NORM_CH), NORM_CH)
        x = h_ref[rows, :]
        r = lax.rsqrt(jnp.mean(x * x, axis=-1, keepdims=True) + EPS)
        u_ref[rows, :] = ((x * r) * g_ref[...]).astype(BF16)
        return carry
    lax.fori_loop(0, tm // NORM_CH, body, 0)


def _rope_tile(acc, c, s):
    lane = lax.broadcasted_iota(jnp.int32, c.shape, 1)
    first = (lane % D_QK_DIM) < (ROPE_DIM // 2)
    outs = []
    for g in range(acc.shape[1] // 128):
        y = acc[:, g * 128:(g + 1) * 128]
        rot = jnp.where(first, pltpu.roll(y, 128 - ROPE_DIM // 2, 1), pltpu.roll(y, ROPE_DIM // 2, 1))
        outs.append(y * c + rot * s)
    return jnp.concatenate(outs, axis=1)


NA1 = C_DQ // TN1
GROWS = 16
GATE_COL0 = 4 * M_WIDTH
GT_PAD = -(-TM1 // 128) * 128


def _inproj_kernel(h_ref, g_ref, wa_ref, wb_ref, wg_ref, gb_ref, cq_ref, sq_ref, ck_ref, sk_ref,
                   o_ref, gt_ref, u_ref):
    n = pl.program_id(1)

    @pl.when(n == 0)
    def _():
        _rmsnorm_rows_to(h_ref, g_ref, u_ref, TM1)
        gt = lax.dot_general(wg_ref[...].astype(BF16), u_ref[...], (((1,), (1,)), ((), ())),
                             preferred_element_type=F32)
        gt_ref[:, :TM1] = gt + gb_ref[...]
        gt_ref[:, TM1:] = jnp.zeros((GROWS, GT_PAD - TM1), F32)

    def mm(wt_ref):
        return lax.dot_general(u_ref[...], wt_ref[...], (((1,), (1,)), ((), ())), preferred_element_type=F32)

    q0, k0, v0 = C_DQ // TN1, C_DK // TN1, C_DV // TN1

    @pl.when(n < NA1)
    def _():
        o_ref[...] = mm(wa_ref).astype(o_ref.dtype)

    @pl.when(jnp.logical_and(n >= q0, n < k0))
    def _():
        o_ref[...] = _rope_tile(mm(wb_ref), cq_ref[...], sq_ref[...]).astype(o_ref.dtype)

    @pl.when(jnp.logical_and(n >= k0, n < v0))
    def _():
        o_ref[...] = _rope_tile(mm(wb_ref), ck_ref[...], sk_ref[...]).astype(o_ref.dtype)

    @pl.when(n >= v0)
    def _():
        o_ref[...] = mm(wb_ref).astype(o_ref.dtype)


def _inproj(h, g, wa, wb, w_in_t, gbias, tabs):
    tiles_per_batch = LP // TM1
    tab_spec = pl.BlockSpec((TM1, 128), lambda m, n: (m % tiles_per_batch, 0))
    return pl.pallas_call(
        _inproj_kernel,
        grid=(R // TM1, PROJ_W // TN1),
        in_specs=[
            pl.BlockSpec((TM1, D_MODEL), lambda m, n: (m, 0)),
            pl.BlockSpec((1, D_MODEL), lambda m, n: (0, 0)),
            pl.BlockSpec((TN1, D_MODEL), lambda m, n: (jnp.minimum(n, NA1 - 1), 0)),
            pl.BlockSpec((pl.Element(TN1), pl.Element(D_MODEL)),
                         lambda m, n: (pl.multiple_of(GATE_COL0 + GROWS + jnp.maximum(n - NA1, 0) * TN1, GROWS), 0)),
            pl.BlockSpec((GROWS, D_MODEL), lambda m, n: (GATE_COL0 // GROWS, 0)),
            pl.BlockSpec((GROWS, 1), lambda m, n: (0, 0)),
            tab_spec, tab_spec, tab_spec, tab_spec,
        ],
        out_specs=[
            pl.BlockSpec((TM1, TN1), lambda m, n: (m, n)),
            pl.BlockSpec((None, GROWS, GT_PAD), lambda m, n: (m, 0, 0)),
        ],
        out_shape=[
            jax.ShapeDtypeStruct((R, PROJ_W), BF16),
            jax.ShapeDtypeStruct((R // TM1, GROWS, GT_PAD), F32),
        ],
        scratch_shapes=[pltpu.VMEM((TM1, D_MODEL), BF16)],
        compiler_params=_cparams(("parallel", "arbitrary")),
        name="inproj",
    )(h, g, wa, wb, w_in_t, gbias, *tabs)


def _rope_tables():
    pos = np.zeros((LP,), np.float64)
    pos[:SEQ] = np.arange(SEQ) + N_META
    pos[SEQ:SEQ + N_META] = np.arange(N_META)
    half = ROPE_DIM // 2
    inv = ROPE_THETA ** (-np.arange(0, ROPE_DIM, 2, dtype=np.float64) / ROPE_DIM)
    ang = pos[:, None] * inv[None, :]
    cos64 = np.ones((LP, D_QK_DIM)); sin64 = np.zeros((LP, D_QK_DIM))
    cos64[:, :half] = np.cos(ang); cos64[:, half:ROPE_DIM] = np.cos(ang)
    sin64[:, :half] = -np.sin(ang); sin64[:, half:ROPE_DIM] = np.sin(ang)
    cos = np.concatenate([cos64, cos64], axis=1)
    sin = np.concatenate([sin64, sin64], axis=1)
    qs = D_QK_DIM ** -0.5 * math.log2(math.e)
    return [jnp.asarray(t, F32) for t in (cos * qs, sin * qs, cos, sin)]


def _lane_scan(x, op, ident, reverse):
    lane = lax.broadcasted_iota(jnp.int32, x.shape, 1)
    k = 1
    while k < BLK:
        if reverse:
            sh = pltpu.roll(x, BLK - k, 1)
            x = op(x, jnp.where(lane < BLK - k, sh, ident))
        else:
            sh = pltpu.roll(x, k, 1)
            x = op(x, jnp.where(lane >= k, sh, ident))
        k *= 2
    return x


def _log_sigmoid(x):
    return jnp.minimum(x, 0.0) - jnp.log1p(jnp.exp(-jnp.abs(x)))


def _scan_kernel(g_ref, arow_ref, gcol_ref, colt_ref):
    colt_ref[...] = jnp.zeros_like(colt_ref)
    lane = lax.broadcasted_iota(jnp.int32, (8, BLK), 1)
    for d in range(2):
        reverse = d == 1
        carry_b = jnp.zeros((8, BLK), F32)
        carry_m = jnp.full((8, BLK), NEG, F32)
        logical = [META_BLK] + list(range(META_BLK))
        order = logical[::-1] if reverse else logical
        edge = 0 if reverse else BLK - 1
        for j in order:
            cols = slice(j * BLK, (j + 1) * BLK)
            li = g_ref[d * 8:(d + 1) * 8, cols]
            lf = pltpu.roll(_log_sigmoid(li), M_HEADS, 0)
            if j == META_BLK:
                valid = lane < N_META
                li = jnp.where(valid, li, NEG)
                lf = jnp.where(valid, lf, 0.0)
            bc = carry_b + _lane_scan(lf, jnp.add, 0.0, reverse)
            a = li - bc
            m = jnp.maximum(carry_m, _lane_scan(a, jnp.maximum, NEG, reverse))
            m_last = jnp.broadcast_to(m[:, edge:edge + 1], (8, BLK))
            arow_ref[d * 8:(d + 1) * 8, cols] = a
            base = d * 40
            colt_ref[j, base:base + 8, :] = a
            colt_ref[j, base + 8:base + 16, :] = m
            colt_ref[j, base + 16:base + 24, :] = bc + m
            colt_ref[j, base + 24:base + 32, :] = carry_m
            colt_ref[j, base + 32:base + 40, :] = m_last
            carry_b = jnp.broadcast_to(bc[:, edge:edge + 1], (8, BLK))
            carry_m = m_last
    for j in range(NBLK):
        gcol_ref[j * BLK:(j + 1) * BLK, :] = colt_ref[j].T


def _scans(gt):
    return pl.pallas_call(
        _scan_kernel,
        grid=(BATCH,),
        in_specs=[pl.BlockSpec((None, GROWS, LP), lambda b: (b, 0, 0))],
        out_specs=[
            pl.BlockSpec((None, 16, LP), lambda b: (b, 0, 0)),
            pl.BlockSpec((None, LP, BLK), lambda b: (b, 0, 0)),
        ],
        out_shape=[
            jax.ShapeDtypeStruct((BATCH, 16, LP), F32),
            jax.ShapeDtypeStruct((BATCH, LP, BLK), F32),
        ],
        scratch_shapes=[pltpu.VMEM((NBLK, BLK, BLK), F32)],
        compiler_params=_cparams(("parallel",)),
        name="gate_scans",
    )(gt)


TC_COLS = 256
CONV_CH = 256


def _seq_conv_rows(s_ref, w, lo, n):
    return (s_ref[lo - 1:lo - 1 + n, :] * w[0:1] + s_ref[lo:lo + n, :] * w[1:2]
            + s_ref[lo + 1:lo + 1 + n, :] * w[2:3])


def _qkconv_kernel(x_ref, w_ref, sc_ref, o_ref, s_ref):
    xo = 8 + N_META
    s_ref[0:8, :] = jnp.zeros((8, TC_COLS), F32)
    s_ref[xo + SEQ:xo + SEQ + 8, :] = jnp.zeros((8, TC_COLS), F32)
    s_ref[8:xo, :] = x_ref[SEQ:SEQ + N_META, :].astype(F32)
    for i in range(SEQ // CONV_CH):
        s_ref[xo + i * CONV_CH:xo + (i + 1) * CONV_CH, :] = x_ref[i * CONV_CH:(i + 1) * CONV_CH, :].astype(F32)
    w = w_ref[...]
    sc = sc_ref[...]

    def act(y):
        return (y * _sigmoid(y) * sc).astype(o_ref.dtype)

    for i in range(SEQ // CONV_CH):
        o_ref[i * CONV_CH:(i + 1) * CONV_CH, :] = act(_seq_conv_rows(s_ref, w, xo + i * CONV_CH, CONV_CH))
    o_ref[SEQ:SEQ + N_META, :] = act(_seq_conv_rows(s_ref, w, 8, N_META))
    o_ref[SEQ + N_META:LP, :] = jnp.zeros((LP - SEQ - N_META, TC_COLS), o_ref.dtype)


def _qkconv(proj3, w, scale):
    return pl.pallas_call(
        _qkconv_kernel,
        grid=(BATCH, 2 * M_WIDTH // TC_COLS),
        in_specs=[
            pl.BlockSpec((None, LP, TC_COLS), lambda b, c: (b, 0, C_MQK // TC_COLS + c)),
            pl.BlockSpec((3, TC_COLS), lambda b, c: (0, c)),
            pl.BlockSpec((1, TC_COLS), lambda b, c: (0, c)),
        ],
        out_specs=pl.BlockSpec((None, LP, TC_COLS), lambda b, c: (b, 0, c)),
        out_shape=jax.ShapeDtypeStruct((BATCH, LP, 2 * M_WIDTH), BF16),
        scratch_shapes=[pltpu.VMEM((SEQ + N_META + 16, TC_COLS), F32)],
        compiler_params=_cparams(("parallel", "parallel")),
        name="qk_conv",
    )(proj3, w, scale)


def _mlstm_kernel(qf, kf, vf, gcf, arf, qb, kb, vb, gcb, arb, hf_ref, hb_ref, c_ref, n_ref):
    c = pl.program_id(1)

    @pl.when(c == 0)
    def _():
        c_ref[...] = jnp.zeros_like(c_ref)
        n_ref[...] = jnp.zeros_like(n_ref)

    row = lax.broadcasted_iota(jnp.int32, (BLK, BLK), 0)
    col = lax.broadcasted_iota(jnp.int32, (BLK, BLK), 1)
    dirs = ((qf, kf, vf, gcf, arf, hf_ref), (qb, kb, vb, gcb, arb, hb_ref))
    for d, (q_ref, k_ref, v_ref, gc_ref, ar_ref, o_ref) in enumerate(dirs):
        mask = (col <= row) if d == 0 else (col >= row)
        gc = gc_ref[...]
        for h in range(M_HEADS):
            def colq(qi):
                j = d * 40 + qi * 8 + h
                return gc[:, j:j + 1]
            a_col, m_col, e_col, mp_col, ml_col = (colq(i) for i in range(5))
            a_row = ar_ref[d * 8 + h:d * 8 + h + 1, :]
            hs = slice(h * M_HEAD_DIM, (h + 1) * M_HEAD_DIM)
            q = q_ref[:, hs]
            k = k_ref[:, hs]
            v = v_ref[:, hs]
            s = lax.dot_general(q, k, (((1,), (1,)), ((), ())), preferred_element_type=F32)
            p = jnp.where(mask, jnp.exp(a_row - m_col), 0.0)
            sp = s * p
            a = jnp.exp(mp_col - m_col)
            ct = c_ref[d * M_HEADS + h]
            nrow = n_ref[d * M_HEADS + h]
            inter = jnp.dot(q, ct.astype(BF16), preferred_element_type=F32)
            num = jnp.dot(sp.astype(BF16), v, preferred_element_type=F32) + a * inter
            qn = jnp.sum(q.astype(F32) * nrow, axis=1, keepdims=True)
            den = jnp.sum(sp, axis=1, keepdims=True) + a * qn
            o_ref[:, hs] = num / jnp.maximum(jnp.abs(den), jnp.exp(-e_col))
            w = jnp.exp(a_col - ml_col)
            dec = jnp.exp(mp_col[0:1, :] - ml_col[0:1, :])
            wv = (w * v.astype(F32)).astype(BF16)
            c_ref[d * M_HEADS + h] = dec * ct + lax.dot_general(
                k, wv, (((0,), (0,)), ((), ())), preferred_element_type=F32)
            n_ref[d * M_HEADS + h] = dec * nrow + jnp.sum(w * k.astype(F32), axis=0, keepdims=True)


def _fwd_blk(c):
    return (c + NBLK - 1) % NBLK


def _bwd_blk(c):
    return (2 * NBLK - 2 - c) % NBLK


def _mlstm(qk3, proj3, gcol, arow):
    def specs(blk):
        return [
            pl.BlockSpec((None, BLK, M_WIDTH), lambda b, c: (b, blk(c), 0)),
            pl.BlockSpec((None, BLK, M_WIDTH), lambda b, c: (b, blk(c), 1)),
            pl.BlockSpec((None, BLK, M_WIDTH), lambda b, c: (b, blk(c), C_MV // M_WIDTH)),
            pl.BlockSpec((None, BLK, BLK), lambda b, c: (b, blk(c), 0)),
            pl.BlockSpec((None, 16, BLK), lambda b, c: (b, 0, blk(c))),
        ]
    out_sd = jax.ShapeDtypeStruct((BATCH, LP, M_WIDTH), F32)
    return pl.pallas_call(
        _mlstm_kernel,
        grid=(BATCH, NBLK),
        in_specs=specs(_fwd_blk) + specs(_bwd_blk),
        out_specs=[
            pl.BlockSpec((None, BLK, M_WIDTH), lambda b, c: (b, _fwd_blk(c), 0)),
            pl.BlockSpec((None, BLK, M_WIDTH), lambda b, c: (b, _bwd_blk(c), 0)),
        ],
        out_shape=[out_sd, out_sd],
        scratch_shapes=[
            pltpu.VMEM((2 * M_HEADS, M_HEAD_DIM, M_HEAD_DIM), F32),
            pltpu.VMEM((2 * M_HEADS, 1, M_HEAD_DIM), F32),
        ],
        compiler_params=_cparams(("parallel", "arbitrary")),
        name="mlstm",
    )(qk3, qk3, proj3, gcol, arow, qk3, qk3, proj3, gcol, arow)


TQ = 384
QG = 2
RG = TQ // QG


def _attn_kernel(lam_ref, q_ref, k_ref, v_ref, g_ref, o_ref):
    lq = lam_ref[...]
    la = jnp.sum(lq[0:1] * lq[1:2], axis=1, keepdims=True)
    lb = jnp.sum(lq[2:3] * lq[3:4], axis=1, keepdims=True)
    lam = jnp.exp(la) - jnp.exp(lb) + LAM_INIT

    lane = lax.broadcasted_iota(jnp.int32, (RG, 128), 1)
    colid = lax.broadcasted_iota(jnp.int32, (1, BLK), 1)

    def scores(g):
        qg = q_ref[g * RG:(g + 1) * RG, :]
        zero = jnp.zeros_like(qg)
        q2 = jnp.concatenate([jnp.where(lane < D_QK_DIM, qg, zero), jnp.where(lane >= D_QK_DIM, qg, zero)], axis=0)
        s = lax.dot_general(q2, k_ref[...], (((1,), (1,)), ((), ())), preferred_element_type=F32)
        tail = jnp.where(colid < N_META, s[:, LP - BLK:], -jnp.inf)
        return jnp.concatenate([s[:, :LP - BLK], tail], axis=1)

    def finish(g, s):
        m = jnp.max(s, axis=1, keepdims=True)
        p = jnp.exp2(s - m)
        l = jnp.sum(p, axis=1, keepdims=True)
        ratio = lam * l[:RG] / l[RG:]
        a = (p[:RG] - ratio * p[RG:]).astype(BF16)
        o = jnp.dot(a, v_ref[...], preferred_element_type=F32) / l[:RG]
        r = lax.rsqrt(jnp.mean(o * o, axis=-1, keepdims=True) + EPS)
        o_ref[g * RG:(g + 1) * RG, :] = (((o * r) * g_ref[...]) * (1.0 - LAM_INIT)).astype(o_ref.dtype)

    s_next = scores(0)
    for g in range(QG):
        s = s_next
        if g + 1 < QG:
            s_next = scores(g + 1)
        finish(g, s)


def _attention(lam4, proj3, subln_g):
    return pl.pallas_call(
        _attn_kernel,
        grid=(BATCH, D_HEADS, LP // TQ),
        in_specs=[
            pl.BlockSpec((4, D_QK_DIM), lambda b, h, i: (0, 0)),
            pl.BlockSpec((None, TQ, 128), lambda b, h, i: (b, i, C_DQ // 128 + h)),
            pl.BlockSpec((None, LP, 128), lambda b, h, i: (b, 0, C_DK // 128 + h)),
            pl.BlockSpec((None, LP, 128), lambda b, h, i: (b, 0, C_DV // 128 + h)),
            pl.BlockSpec((1, D_V_DIM), lambda b, h, i: (0, 0)),
        ],
        out_specs=pl.BlockSpec((None, TQ, D_V_DIM), lambda b, h, i: (b, i, h)),
        out_shape=jax.ShapeDtypeStruct((BATCH, LP, D_WIDTH), BF16),
        compiler_params=_cparams(("parallel", "parallel", "arbitrary")),
        name="diff_attn",
    )(lam4, proj3, proj3, proj3, subln_g)


TM4 = 1056
TN4 = 512
LN_CH = 96


def _merge_kernel(hf_ref, hb_ref, mo_ref, ng_ref, bo_ref, wa_ref, wb_ref, gm_ref, gd_ref, o_ref, a_ref):
    n = pl.program_id(1)

    @pl.when(n == 0)
    def _():
        def body(i, carry):
            rows = pl.ds(pl.multiple_of(i * LN_CH, LN_CH), LN_CH)
            for h in range(M_HEADS):
                hs = slice(h * M_HEAD_DIM, (h + 1) * M_HEAD_DIM)
                x = hf_ref[rows, hs] + hb_ref[rows, hs]
                mu = jnp.mean(x, axis=-1, keepdims=True)
                xc = x - mu
                var = jnp.mean(xc * xc, axis=-1, keepdims=True)
                hn = (xc * lax.rsqrt(var + EPS)) * ng_ref[:, hs]
                a_ref[rows, hs] = (_sigmoid(mo_ref[rows, hs].astype(F32)) * hn).astype(BF16)
            return carry
        lax.fori_loop(0, TM4 // LN_CH, body, 0)

    ya = jnp.dot(a_ref[...], wa_ref[...], preferred_element_type=F32)
    yb = jnp.dot(bo_ref[...], wb_ref[...], preferred_element_type=F32)
    o_ref[...] = (_sigmoid(gm_ref[...].astype(F32)) * ya
                  + _sigmoid(gd_ref[...].astype(F32)) * yb).astype(o_ref.dtype)


def _merge(hf, hb, proj, norm_g, b_out, wa, wb):
    return pl.pallas_call(
        _merge_kernel,
        grid=(R // TM4, D_MODEL // TN4),
        in_specs=[
            pl.BlockSpec((TM4, M_WIDTH), lambda m, n: (m, 0)),
            pl.BlockSpec((TM4, M_WIDTH), lambda m, n: (m, 0)),
            pl.BlockSpec((TM4, M_WIDTH), lambda m, n: (m, C_MO // M_WIDTH)),
            pl.BlockSpec((1, M_WIDTH), lambda m, n: (0, 0)),
            pl.BlockSpec((TM4, D_WIDTH), lambda m, n: (m, 0)),
            pl.BlockSpec((M_WIDTH, TN4), lambda m, n: (0, n)),
            pl.BlockSpec((D_WIDTH, TN4), lambda m, n: (0, n)),
            pl.BlockSpec((TM4, TN4), lambda m, n: (m, C_GM // TN4 + n)),
            pl.BlockSpec((TM4, TN4), lambda m, n: (m, C_GD // TN4 + n)),
        ],
        out_specs=pl.BlockSpec((TM4, TN4), lambda m, n: (m, n)),
        out_shape=jax.ShapeDtypeStruct((R, D_MODEL), BF16),
        scratch_shapes=[pltpu.VMEM((TM4, M_WIDTH), BF16)],
        compiler_params=_cparams(("parallel", "arbitrary")),
        name="merge",
    )(hf, hb, proj, norm_g, b_out, wa, wb, proj, proj)


TM5 = 384


def _outproj_kernel(h0_ref, mg_ref, w_ref, g2_ref, h1_ref, u2_ref):
    h1 = h0_ref[...] + jnp.dot(mg_ref[...], w_ref[...], preferred_element_type=F32)
    h1_ref[...] = h1
    r = lax.rsqrt(jnp.mean(h1 * h1, axis=-1, keepdims=True) + EPS)
    u2_ref[...] = ((h1 * r) * g2_ref[...]).astype(BF16)


def _outproj(h0, merged, w, g2):
    return pl.pallas_call(
        _outproj_kernel,
        grid=(R // TM5,),
        in_specs=[
            pl.BlockSpec((TM5, D_MODEL), lambda m: (m, 0)),
            pl.BlockSpec((TM5, D_MODEL), lambda m: (m, 0)),
            pl.BlockSpec((D_MODEL, D_MODEL), lambda m: (0, 0)),
            pl.BlockSpec((1, D_MODEL), lambda m: (0, 0)),
        ],
        out_specs=[
            pl.BlockSpec((TM5, D_MODEL), lambda m: (m, 0)),
            pl.BlockSpec((TM5, D_MODEL), lambda m: (m, 0)),
        ],
        out_shape=[
            jax.ShapeDtypeStruct((R, D_MODEL), F32),
            jax.ShapeDtypeStruct((R, D_MODEL), BF16),
        ],
        compiler_params=_cparams(("parallel",)),
        name="outproj",
    )(h0, merged, w, g2)


TMU = 1024
TNU = 512
HALO = 16
NNU = FFN_DIM // TNU


def _ffn_up_kernel(x_ref, xp_ref, xn_ref, wg_ref, wv_ref, cg_ref, cv_ref, o_ref, xs_ref, sg_ref, sv_ref):
    @pl.when(pl.program_id(2) == 0)
    def _():
        last_tile = pl.program_id(1) == SEQ // TMU - 1
        xs_ref[0:HALO, :] = xp_ref[...]
        xs_ref[HALO:HALO + TMU, :] = x_ref[...]
        xn = xn_ref[...]
        xs_ref[HALO + TMU:, :] = jnp.where(last_tile, jnp.zeros_like(xn), xn)

    def branch(w_ref, c_ref, s_ref):
        s_ref[...] = jnp.dot(xs_ref[...], w_ref[...].astype(BF16), preferred_element_type=F32)
        c = c_ref[...]
        return (s_ref[HALO - 1:HALO - 1 + TMU, :] * c[0:1] + s_ref[HALO:HALO + TMU, :] * c[1:2]
                + s_ref[HALO + 1:HALO + 1 + TMU, :] * c[2:3])

    gate = branch(wg_ref, cg_ref, sg_ref)
    val = branch(wv_ref, cv_ref, sv_ref)
    o_ref[...] = (gate * _sigmoid(gate) * val).astype(o_ref.dtype)


def _ffn_up(u2_3, w_up, conv_w):
    hb = TMU // HALO
    meta_hblk = SEQ // HALO
    return pl.pallas_call(
        _ffn_up_kernel,
        grid=(BATCH, SEQ // TMU, NNU),
        in_specs=[
            pl.BlockSpec((None, TMU, D_MODEL), lambda b, i, n: (b, i, 0)),
            pl.BlockSpec((None, HALO, D_MODEL), lambda b, i, n: (b, jnp.where(i == 0, meta_hblk, i * hb - 1), 0)),
            pl.BlockSpec((None, HALO, D_MODEL), lambda b, i, n: (b, (i + 1) * hb, 0)),
            pl.BlockSpec((D_MODEL, TNU), lambda b, i, n: (0, n)),
            pl.BlockSpec((D_MODEL, TNU), lambda b, i, n: (0, NNU + n)),
            pl.BlockSpec((3, TNU), lambda b, i, n: (0, n)),
            pl.BlockSpec((3, TNU), lambda b, i, n: (0, NNU + n)),
        ],
        out_specs=pl.BlockSpec((None, TMU, TNU), lambda b, i, n: (b, i, n)),
        out_shape=jax.ShapeDtypeStruct((BATCH, SEQ, FFN_DIM), BF16),
        scratch_shapes=[
            pltpu.VMEM((TMU + 2 * HALO, D_MODEL), BF16),
            pltpu.VMEM((TMU + 2 * HALO, TNU), F32),
            pltpu.VMEM((TMU + 2 * HALO, TNU), F32),
        ],
        compiler_params=_cparams(("parallel", "parallel", "arbitrary")),
        name="ffn_up",
    )(u2_3, u2_3, u2_3, w_up, w_up, conv_w, conv_w)


TM7 = 1024
TK7 = 512
NK7 = FFN_DIM // TK7


def _down_kernel(a_ref, wd_ref, h1_ref, gf_ref, o_ref, acc_ref):
    k = pl.program_id(2)

    def part():
        return jnp.dot(a_ref[...], wd_ref[...], preferred_element_type=F32)

    @pl.when(k == 0)
    def _():
        acc_ref[...] = part()

    @pl.when(jnp.logical_and(k > 0, k < NK7 - 1))
    def _():
        acc_ref[...] += part()

    @pl.when(k == NK7 - 1)
    def _():
        h2 = h1_ref[...] + (acc_ref[...] + part())
        r = lax.rsqrt(jnp.mean(h2 * h2, axis=-1, keepdims=True) + EPS)
        o_ref[...] = (h2 * r) * gf_ref[...]


def _ffn_down(act, w_down, h1_3, gf):
    return pl.pallas_call(
        _down_kernel,
        grid=(BATCH, SEQ // TM7, NK7),
        in_specs=[
            pl.BlockSpec((None, TM7, TK7), lambda b, i, k: (b, i, k)),
            pl.BlockSpec((TK7, D_MODEL), lambda b, i, k: (k, 0)),
            pl.BlockSpec((None, TM7, D_MODEL), lambda b, i, k: (b, i, 0)),
            pl.BlockSpec((1, D_MODEL), lambda b, i, k: (0, 0)),
        ],
        out_specs=pl.BlockSpec((None, TM7, D_MODEL), lambda b, i, k: (b, i, 0)),
        out_shape=jax.ShapeDtypeStruct((BATCH, SEQ, D_MODEL), F32),
        scratch_shapes=[pltpu.VMEM((TM7, D_MODEL), F32)],
        compiler_params=_cparams(("parallel", "parallel", "arbitrary")),
        name="ffn_down",
    )(act, w_down, h1_3, gf)


def kernel(x, meta_tokens, norm1_g, w_in, mlstm_conv_w, mlstm_gate_bias, mlstm_norm_g, lambda_q1, lambda_k1,
           lambda_q2, lambda_k2, diff_subln_g, w_branch_m, w_branch_d, w_out, norm2_g, w_up, ffn_conv_w,
           w_down, norm_f_g):
    assert x.shape == (BATCH, SEQ, D_MODEL) and w_in.shape[0] == 1
    meta = jnp.broadcast_to(meta_tokens[None].astype(x.dtype), (BATCH, N_META, D_MODEL))
    pad = jnp.zeros((BATCH, LP - SEQ - N_META, D_MODEL), x.dtype)
    h0 = jnp.concatenate([x, meta, pad], axis=1)
    h0f = h0.reshape(R, D_MODEL)

    wit = w_in[0].T
    w_pa = w_pb = wit.astype(BF16)
    gbias = mlstm_gate_bias[0].reshape(GROWS, 1).astype(F32)
    g1 = norm1_g[0].reshape(1, D_MODEL)

    proj, gt_tiles = _inproj(h0f, g1, w_pa, w_pb, wit, gbias, _rope_tables())
    proj3 = proj.reshape(BATCH, LP, PROJ_W)

    gt = gt_tiles[:, :, :TM1].reshape(BATCH, LP // TM1, GROWS, TM1).transpose(0, 2, 1, 3).reshape(BATCH, GROWS, LP)
    arow, gcol = _scans(gt)

    kscale = jnp.concatenate([jnp.ones((1, M_WIDTH), F32), jnp.full((1, M_WIDTH), M_HEAD_DIM ** -0.5, F32)], axis=1)
    qk3 = _qkconv(proj3, mlstm_conv_w[0], kscale)
    hf, hb = _mlstm(qk3, proj3, gcol, arow)

    lam4 = jnp.stack([lambda_q1[0], lambda_k1[0], lambda_q2[0], lambda_k2[0]]).astype(F32)
    b_out = _attention(lam4, proj3, diff_subln_g[0].reshape(1, D_V_DIM))

    merged = _merge(hf.reshape(R, M_WIDTH), hb.reshape(R, M_WIDTH), proj, mlstm_norm_g[0].reshape(1, M_WIDTH),
                    b_out.reshape(R, D_WIDTH), w_branch_m[0].astype(BF16), w_branch_d[0].astype(BF16))
    h1, u2 = _outproj(h0f, merged, w_out[0].astype(BF16), norm2_g[0].reshape(1, D_MODEL))

    act = _ffn_up(u2.reshape(BATCH, LP, D_MODEL), w_up[0], ffn_conv_w[0])
    return _ffn_down(act, w_down[0].astype(BF16), h1.reshape(BATCH, LP, D_MODEL), norm_f_g.reshape(1, D_MODEL))
```

```python
import functools
import math

import numpy as np
import jax
import jax.numpy as jnp
from jax import lax
from jax.experimental import pallas as pl
from jax.experimental.pallas import tpu as pltpu

F32 = jnp.float32
BF16 = jnp.bfloat16

D_MODEL = 2048
BATCH = 2
SEQ = 4096
N_META = 16
EPS = 1e-6
M_HEADS = 4
M_WIDTH = 1024
M_HEAD_DIM = 256
D_HEADS = 8
D_V_DIM = 128
D_QK_DIM = 64
D_WIDTH = 1024
ROPE_THETA = 500000.0
ROPE_DIM = 16
FFN_DIM = 5632
LAM_INIT = 0.8 - 0.6 * math.exp(-0.3 * 0)

BLK = 128
LP = SEQ + BLK
NBLK = LP // BLK
META_BLK = SEQ // BLK
R = BATCH * LP
NEG = -1e30

C_MQK, C_MV, C_MO, C_DQ, C_DK, C_DV, C_GM, C_GD = 0, 2048, 3072, 4096, 5120, 6144, 7168, 9216
PROJ_W = 11264

VMEM_LIMIT = 56 * 1024 * 1024


def _cparams(sem):
    return pltpu.CompilerParams(dimension_semantics=sem, vmem_limit_bytes=VMEM_LIMIT)


def _sigmoid(x):
    return 0.5 * jnp.tanh(0.5 * x) + 0.5


TM1 = 1056
TN1 = 1024
NORM_CH = 96


def _rmsnorm_rows_to(h_ref, g_ref, u_ref, tm):
    def body(i, carry):
        rows = pl.ds(pl.multiple_of(i * NORM_CH, NORM_CH), NORM_CH)
        x = h_ref[rows, :]
        r = lax.rsqrt(jnp.mean(x * x, axis=-1, keepdims=True) + EPS)
        u_ref[rows, :] = ((x * r) * g_ref[...]).astype(BF16)
        return carry
    lax.fori_loop(0, tm // NORM_CH, body, 0)


def _rope_tile(acc, c, s):
    lane = lax.broadcasted_iota(jnp.int32, c.shape, 1)
    first = (lane % D_QK_DIM) < (ROPE_DIM // 2)
    outs = []
    for g in range(acc.shape[1] // 128):
        y = acc[:, g * 128:(g + 1) * 128]
        rot = jnp.where(first, pltpu.roll(y, 128 - ROPE_DIM // 2, 1), pltpu.roll(y, ROPE_DIM // 2, 1))
        outs.append(y * c + rot * s)
    return jnp.concatenate(outs, axis=1)


NA1 = C_DQ // TN1
GROWS = 16
GATE_COL0 = 4 * M_WIDTH
GT_PAD = -(-TM1 // 128) * 128


def _inproj_kernel(h_ref, g_ref, wa_ref, wb_ref, wg_ref, gb_ref, cq_ref, sq_ref, ck_ref, sk_ref,
                   o_ref, gt_ref, u_ref):
    n = pl.program_id(1)

    @pl.when(n == 0)
    def _():
        _rmsnorm_rows_to(h_ref, g_ref, u_ref, TM1)
        gt = lax.dot_general(wg_ref[...].astype(BF16), u_ref[...], (((1,), (1,)), ((), ())),
                             preferred_element_type=F32)
        gt_ref[:, :TM1] = gt + gb_ref[...]
        gt_ref[:, TM1:] = jnp.zeros((GROWS, GT_PAD - TM1), F32)

    def mm(wt_ref):
        return lax.dot_general(u_ref[...], wt_ref[...], (((1,), (1,)), ((), ())), preferred_element_type=F32)

    q0, k0, v0 = C_DQ // TN1, C_DK // TN1, C_DV // TN1

    @pl.when(n < NA1)
    def _():
        o_ref[...] = mm(wa_ref).astype(o_ref.dtype)

    @pl.when(jnp.logical_and(n >= q0, n < k0))
    def _():
        o_ref[...] = _rope_tile(mm(wb_ref), cq_ref[...], sq_ref[...]).astype(o_ref.dtype)

    @pl.when(jnp.logical_and(n >= k0, n < v0))
    def _():
        o_ref[...] = _rope_tile(mm(wb_ref), ck_ref[...], sk_ref[...]).astype(o_ref.dtype)

    @pl.when(n >= v0)
    def _():
        o_ref[...] = mm(wb_ref).astype(o_ref.dtype)


def _inproj(h, g, wa, wb, w_in_t, gbias, tabs):
    tiles_per_batch = LP // TM1
    tab_spec = pl.BlockSpec((TM1, 128), lambda m, n: (m % tiles_per_batch, 0))
    return pl.pallas_call(
        _inproj_kernel,
        grid=(R // TM1, PROJ_W // TN1),
        in_specs=[
            pl.BlockSpec((TM1, D_MODEL), lambda m, n: (m, 0)),
            pl.BlockSpec((1, D_MODEL), lambda m, n: (0, 0)),
            pl.BlockSpec((TN1, D_MODEL), lambda m, n: (jnp.minimum(n, NA1 - 1), 0)),
            pl.BlockSpec((pl.Element(TN1), pl.Element(D_MODEL)),
                         lambda m, n: (pl.multiple_of(GATE_COL0 + GROWS + jnp.maximum(n - NA1, 0) * TN1, GROWS), 0)),
            pl.BlockSpec((GROWS, D_MODEL), lambda m, n: (GATE_COL0 // GROWS, 0)),
            pl.BlockSpec((GROWS, 1), lambda m, n: (0, 0)),
            tab_spec, tab_spec, tab_spec, tab_spec,
        ],
        out_specs=[
            pl.BlockSpec((TM1, TN1), lambda m, n: (m, n)),
            pl.BlockSpec((None, GROWS, GT_PAD), lambda m, n: (m, 0, 0)),
        ],
        out_shape=[
            jax.ShapeDtypeStruct((R, PROJ_W), BF16),
            jax.ShapeDtypeStruct((R // TM1, GROWS, GT_PAD), F32),
        ],
        scratch_shapes=[pltpu.VMEM((TM1, D_MODEL), BF16)],
        compiler_params=_cparams(("parallel", "arbitrary")),
        name="inproj",
    )(h, g, wa, wb, w_in_t, gbias, *tabs)


def _rope_tables():
    pos = np.zeros((LP,), np.float64)
    pos[:SEQ] = np.arange(SEQ) + N_META
    pos[SEQ:SEQ + N_META] = np.arange(N_META)
    half = ROPE_DIM // 2
    inv = ROPE_THETA ** (-np.arange(0, ROPE_DIM, 2, dtype=np.float64) / ROPE_DIM)
    ang = pos[:, None] * inv[None, :]
    cos64 = np.ones((LP, D_QK_DIM)); sin64 = np.zeros((LP, D_QK_DIM))
    cos64[:, :half] = np.cos(ang); cos64[:, half:ROPE_DIM] = np.cos(ang)
    sin64[:, :half] = -np.sin(ang); sin64[:, half:ROPE_DIM] = np.sin(ang)
    cos = np.concatenate([cos64, cos64], axis=1)
    sin = np.concatenate([sin64, sin64], axis=1)
    qs = D_QK_DIM ** -0.5 * math.log2(math.e)
    return [jnp.asarray(t, F32) for t in (cos * qs, sin * qs, cos, sin)]


def _lane_scan(x, op, ident, reverse):
    lane = lax.broadcasted_iota(jnp.int32, x.shape, 1)
    k = 1
    while k < BLK:
        if reverse:
            sh = pltpu.roll(x, BLK - k, 1)
            x = op(x, jnp.where(lane < BLK - k, sh, ident))
        else:
            sh = pltpu.roll(x, k, 1)
            x = op(x, jnp.where(lane >= k, sh, ident))
        k *= 2
    return x


def _log_sigmoid(x):
    return jnp.minimum(x, 0.0) - jnp.log1p(jnp.exp(-jnp.abs(x)))


def _scan_kernel(g_ref, arow_ref, gcol_ref, colt_ref):
    colt_ref[...] = jnp.zeros_like(colt_ref)
    lane = lax.broadcasted_iota(jnp.int32, (8, BLK), 1)
    for d in range(2):
        reverse = d == 1
        carry_b = jnp.zeros((8, BLK), F32)
        carry_m = jnp.full((8, BLK), NEG, F32)
        logical = [META_BLK] + list(range(META_BLK))
        order = logical[::-1] if reverse else logical
        edge = 0 if reverse else BLK - 1
        for j in order:
            cols = slice(j * BLK, (j + 1) * BLK)
            li = g_ref[d * 8:(d + 1) * 8, cols]
            lf = pltpu.roll(_log_sigmoid(li), M_HEADS, 0)
            if j == META_BLK:
                valid = lane < N_META
                li = jnp.where(valid, li, NEG)
                lf = jnp.where(valid, lf, 0.0)
            bc = carry_b + _lane_scan(lf, jnp.add, 0.0, reverse)
            a = li - bc
            m = jnp.maximum(carry_m, _lane_scan(a, jnp.maximum, NEG, reverse))
            m_last = jnp.broadcast_to(m[:, edge:edge + 1], (8, BLK))
            arow_ref[d * 8:(d + 1) * 8, cols] = a
            base = d * 40
            colt_ref[j, base:base + 8, :] = a
            colt_ref[j, base + 8:base + 16, :] = m
            colt_ref[j, base + 16:base + 24, :] = bc + m
            colt_ref[j, base + 24:base + 32, :] = carry_m
            colt_ref[j, base + 32:base + 40, :] = m_last
            carry_b = jnp.broadcast_to(bc[:, edge:edge + 1], (8, BLK))
            carry_m = m_last
    for j in range(NBLK):
        gcol_ref[j * BLK:(j + 1) * BLK, :] = colt_ref[j].T


def _scans(gt):
    return pl.pallas_call(
        _scan_kernel,
        grid=(BATCH,),
        in_specs=[pl.BlockSpec((None, GROWS, LP), lambda b: (b, 0, 0))],
        out_specs=[
            pl.BlockSpec((None, 16, LP), lambda b: (b, 0, 0)),
            pl.BlockSpec((None, LP, BLK), lambda b: (b, 0, 0)),
        ],
        out_shape=[
            jax.ShapeDtypeStruct((BATCH, 16, LP), F32),
            jax.ShapeDtypeStruct((BATCH, LP, BLK), F32),
        ],
        scratch_shapes=[pltpu.VMEM((NBLK, BLK, BLK), F32)],
        compiler_params=_cparams(("parallel",)),
        name="gate_scans",
    )(gt)


TC_COLS = 256
CONV_CH = 256


def _seq_conv_rows(s_ref, w, lo, n):
    return (s_ref[lo - 1:lo - 1 + n, :] * w[0:1] + s_ref[lo:lo + n, :] * w[1:2]
            + s_ref[lo + 1:lo + 1 + n, :] * w[2:3])


def _qkconv_kernel(x_ref, w_ref, sc_ref, o_ref, s_ref):
    xo = 8 + N_META
    s_ref[0:8, :] = jnp.zeros((8, TC_COLS), F32)
    s_ref[xo + SEQ:xo + SEQ + 8, :] = jnp.zeros((8, TC_COLS), F32)
    s_ref[8:xo, :] = x_ref[SEQ:SEQ + N_META, :].astype(F32)
    for i in range(SEQ // CONV_CH):
        s_ref[xo + i * CONV_CH:xo + (i + 1) * CONV_CH, :] = x_ref[i * CONV_CH:(i + 1) * CONV_CH, :].astype(F32)
    w = w_ref[...]
    sc = sc_ref[...]

    def act(y):
        return (y * _sigmoid(y) * sc).astype(o_ref.dtype)

    for i in range(SEQ // CONV_CH):
        o_ref[i * CONV_CH:(i + 1) * CONV_CH, :] = act(_seq_conv_rows(s_ref, w, xo + i * CONV_CH, CONV_CH))
    o_ref[SEQ:SEQ + N_META, :] = act(_seq_conv_rows(s_ref, w, 8, N_META))
    o_ref[SEQ + N_META:LP, :] = jnp.zeros((LP - SEQ - N_META, TC_COLS), o_ref.dtype)


def _qkconv(proj3, w, scale):
    return pl.pallas_call(
        _qkconv_kernel,
        grid=(BATCH, 2 * M_WIDTH // TC_COLS),
        in_specs=[
            pl.BlockSpec((None, LP, TC_COLS), lambda b, c: (b, 0, C_MQK // TC_COLS + c)),
            pl.BlockSpec((3, TC_COLS), lambda b, c: (0, c)),
            pl.BlockSpec((1, TC_COLS), lambda b, c: (0, c)),
        ],
        out_specs=pl.BlockSpec((None, LP, TC_COLS), lambda b, c: (b, 0, c)),
        out_shape=jax.ShapeDtypeStruct((BATCH, LP, 2 * M_WIDTH), BF16),
        scratch_shapes=[pltpu.VMEM((SEQ + N_META + 16, TC_COLS), F32)],
        compiler_params=_cparams(("parallel", "parallel")),
        name="qk_conv",
    )(proj3, w, scale)


def _mlstm_kernel(qf, kf, vf, gcf, arf, qb, kb, vb, gcb, arb, hf_ref, hb_ref, c_ref, n_ref):
    c = pl.program_id(1)

    @pl.when(c == 0)
    def _():
        c_ref[...] = jnp.zeros_like(c_ref)
        n_ref[...] = jnp.zeros_like(n_ref)

    row = lax.broadcasted_iota(jnp.int32, (BLK, BLK), 0)
    col = lax.broadcasted_iota(jnp.int32, (BLK, BLK), 1)
    dirs = ((qf, kf, vf, gcf, arf, hf_ref), (qb, kb, vb, gcb, arb, hb_ref))
    for d, (q_ref, k_ref, v_ref, gc_ref, ar_ref, o_ref) in enumerate(dirs):
        mask = (col <= row) if d == 0 else (col >= row)
        gc = gc_ref[...]
        for h in range(M_HEADS):
            def colq(qi):
                j = d * 40 + qi * 8 + h
                return gc[:, j:j + 1]
            a_col, m_col, e_col, mp_col, ml_col = (colq(i) for i in range(5))
            a_row = ar_ref[d * 8 + h:d * 8 + h + 1, :]
            hs = slice(h * M_HEAD_DIM, (h + 1) * M_HEAD_DIM)
            q = q_ref[:, hs]
            k = k_ref[:, hs]
            v = v_ref[:, hs]
            s = lax.dot_general(q, k, (((1,), (1,)), ((), ())), preferred_element_type=F32)
            p = jnp.where(mask, jnp.exp(a_row - m_col), 0.0)
            sp = s * p
            a = jnp.exp(mp_col - m_col)
            ct = c_ref[d * M_HEADS + h]
            nrow = n_ref[d * M_HEADS + h]
            inter = jnp.dot(q, ct.astype(BF16), preferred_element_type=F32)
            num = jnp.dot(sp.astype(BF16), v, preferred_element_type=F32) + a * inter
            qn = jnp.sum(q.astype(F32) * nrow, axis=1, keepdims=True)
            den = jnp.sum(sp, axis=1, keepdims=True) + a * qn
            o_ref[:, hs] = num / jnp.maximum(jnp.abs(den), jnp.exp(-e_col))
            w = jnp.exp(a_col - ml_col)
            dec = jnp.exp(mp_col[0:1, :] - ml_col[0:1, :])
            wv = (w * v.astype(F32)).astype(BF16)
            c_ref[d * M_HEADS + h] = dec * ct + lax.dot_general(
                k, wv, (((0,), (0,)), ((), ())), preferred_element_type=F32)
            n_ref[d * M_HEADS + h] = dec * nrow + jnp.sum(w * k.astype(F32), axis=0, keepdims=True)


def _fwd_blk(c):
    return (c + NBLK - 1) % NBLK


def _bwd_blk(c):
    return (2 * NBLK - 2 - c) % NBLK


def _mlstm(qk3, proj3, gcol, arow):
    def specs(blk):
        return [
            pl.BlockSpec((None, BLK, M_WIDTH), lambda b, c: (b, blk(c), 0)),
            pl.BlockSpec((None, BLK, M_WIDTH), lambda b, c: (b, blk(c), 1)),
            pl.BlockSpec((None, BLK, M_WIDTH), lambda b, c: (b, blk(c), C_MV // M_WIDTH)),
            pl.BlockSpec((None, BLK, BLK), lambda b, c: (b, blk(c), 0)),
            pl.BlockSpec((None, 16, BLK), lambda b, c: (b, 0, blk(c))),
        ]
    out_sd = jax.ShapeDtypeStruct((BATCH, LP, M_WIDTH), F32)
    return pl.pallas_call(
        _mlstm_kernel,
        grid=(BATCH, NBLK),
        in_specs=specs(_fwd_blk) + specs(_bwd_blk),
        out_specs=[
            pl.BlockSpec((None, BLK, M_WIDTH), lambda b, c: (b, _fwd_blk(c), 0)),
            pl.BlockSpec((None, BLK, M_WIDTH), lambda b, c: (b, _bwd_blk(c), 0)),
        ],
        out_shape=[out_sd, out_sd],
        scratch_shapes=[
            pltpu.VMEM((2 * M_HEADS, M_HEAD_DIM, M_HEAD_DIM), F32),
            pltpu.VMEM((2 * M_HEADS, 1, M_HEAD_DIM), F32),
        ],
        compiler_params=_cparams(("parallel", "arbitrary")),
        name="mlstm",
    )(qk3, qk3, proj3, gcol, arow, qk3, qk3, proj3, gcol, arow)


RQ = 128
NGRP = LP // RQ
EXP_CH = 512


def _attn_kernel(lam_ref, q_ref, k_ref, v_ref, g_ref, o_ref,
                 s0_ref, s1_ref, p0_ref, p1_ref, l0_ref, l1_ref, m0_ref, m1_ref):
    lq = lam_ref[...]
    la = jnp.sum(lq[0:1] * lq[1:2], axis=1, keepdims=True)
    lb = jnp.sum(lq[2:3] * lq[3:4], axis=1, keepdims=True)
    lam = jnp.exp(la) - jnp.exp(lb) + LAM_INIT

    lane = lax.broadcasted_iota(jnp.int32, (RQ, 128), 1)
    colid = lax.broadcasted_iota(jnp.int32, (1, BLK), 1)

    def rows(g):
        return pl.ds(pl.multiple_of(g * RQ, RQ), RQ)

    def stage_scores(g, s_ref, m_ref):
        qg = q_ref[rows(g), :]
        zero = jnp.zeros_like(qg)
        q2 = jnp.concatenate([jnp.where(lane < D_QK_DIM, qg, zero), jnp.where(lane >= D_QK_DIM, qg, zero)], axis=0)
        s = lax.dot_general(q2, k_ref[...], (((1,), (1,)), ((), ())), preferred_element_type=F32)
        tail = jnp.where(colid < N_META, s[:, LP - BLK:], -jnp.inf)
        s_ref[:, :LP - BLK] = s[:, :LP - BLK]
        s_ref[:, LP - BLK:] = tail
        m = jnp.maximum(jnp.max(s[:, :LP - BLK], axis=1, keepdims=True), jnp.max(tail, axis=1, keepdims=True))
        m_ref[...] = jnp.broadcast_to(m, m_ref.shape)

    def stage_exp(s_ref, m_ref, p_ref, l_ref):
        m = m_ref[:, 0:1]
        part = jnp.zeros((2 * RQ, 128), F32)
        for c0 in range(0, LP, EXP_CH):
            c1 = min(c0 + EXP_CH, LP)
            pc = jnp.exp2(s_ref[:, c0:c1] - m)
            for j in range(0, c1 - c0, 128):
                part = part + pc[:, j:j + 128]
            p_ref[:, c0:c1] = pc.astype(BF16)
        l_ref[...] = jnp.broadcast_to(jnp.sum(part, axis=1, keepdims=True), l_ref.shape)

    def stage_pv(g, p_ref, l_ref):
        l = l_ref[:, 0:1]
        ratio = (lam * l[:RQ] / l[RQ:]).astype(BF16)
        a = p_ref[:RQ, :] - ratio * p_ref[RQ:, :]
        o = jnp.dot(a, v_ref[...], preferred_element_type=F32) / l[:RQ]
        r = lax.rsqrt(jnp.mean(o * o, axis=-1, keepdims=True) + EPS)
        o_ref[rows(g), :] = (((o * r) * g_ref[...]) * (1.0 - LAM_INIT)).astype(o_ref.dtype)

    sbuf = ((s0_ref, m0_ref), (s1_ref, m1_ref))
    pbuf = ((p0_ref, l0_ref), (p1_ref, l1_ref))

    def tick(t, static):
        par = static % 2
        if static < NGRP:
            stage_scores(t, *sbuf[par])
        if 1 <= static <= NGRP:
            stage_exp(*sbuf[1 - par], *pbuf[1 - par])
        if 2 <= static <= NGRP + 1:
            stage_pv(t - 2, *pbuf[par])

    tick(0, 0)
    tick(1, 1)
    n_pairs = (NGRP - 2) // 2

    def body(i, carry):
        t = 2 * i + 2
        tick(t, 2)
        tick(t + 1, 3)
        return carry

    lax.fori_loop(0, n_pairs, body, 0)
    for t in range(2 + 2 * n_pairs, NGRP + 2):
        tick(t, t)


def _attention(lam4, proj3, subln_g):
    return pl.pallas_call(
        _attn_kernel,
        grid=(BATCH, D_HEADS),
        in_specs=[
            pl.BlockSpec((4, D_QK_DIM), lambda b, h: (0, 0)),
            pl.BlockSpec((None, LP, 128), lambda b, h: (b, 0, C_DQ // 128 + h)),
            pl.BlockSpec((None, LP, 128), lambda b, h: (b, 0, C_DK // 128 + h)),
            pl.BlockSpec((None, LP, 128), lambda b, h: (b, 0, C_DV // 128 + h)),
            pl.BlockSpec((1, D_V_DIM), lambda b, h: (0, 0)),
        ],
        out_specs=pl.BlockSpec((None, LP, D_V_DIM), lambda b, h: (b, 0, h)),
        out_shape=jax.ShapeDtypeStruct((BATCH, LP, D_WIDTH), BF16),
        scratch_shapes=[
            pltpu.VMEM((2 * RQ, LP), F32), pltpu.VMEM((2 * RQ, LP), F32),
            pltpu.VMEM((2 * RQ, LP), BF16), pltpu.VMEM((2 * RQ, LP), BF16),
            pltpu.VMEM((2 * RQ, 128), F32), pltpu.VMEM((2 * RQ, 128), F32),
            pltpu.VMEM((2 * RQ, 128), F32), pltpu.VMEM((2 * RQ, 128), F32),
        ],
        compiler_params=_cparams(("parallel", "parallel")),
        name="diff_attn",
    )(lam4, proj3, proj3, proj3, subln_g)


TM4 = 1056
TN4 = 512
LN_CH = 96


def _merge_kernel(hf_ref, hb_ref, mo_ref, ng_ref, bo_ref, wa_ref, wb_ref, gm_ref, gd_ref, o_ref, a_ref):
    n = pl.program_id(1)

    @pl.when(n == 0)
    def _():
        def body(i, carry):
            rows = pl.ds(pl.multiple_of(i * LN_CH, LN_CH), LN_CH)
            for h in range(M_HEADS):
                hs = slice(h * M_HEAD_DIM, (h + 1) * M_HEAD_DIM)
                x = hf_ref[rows, hs] + hb_ref[rows, hs]
                mu = jnp.mean(x, axis=-1, keepdims=True)
                xc = x - mu
                var = jnp.mean(xc * xc, axis=-1, keepdims=True)
                hn = (xc * lax.rsqrt(var + EPS)) * ng_ref[:, hs]
                a_ref[rows, hs] = (_sigmoid(mo_ref[rows, hs].astype(F32)) * hn).astype(BF16)
            return carry
        lax.fori_loop(0, TM4 // LN_CH, body, 0)

    ya = jnp.dot(a_ref[...], wa_ref[...], preferred_element_type=F32)
    yb = jnp.dot(bo_ref[...], wb_ref[...], preferred_element_type=F32)
    o_ref[...] = (_sigmoid(gm_ref[...].astype(F32)) * ya
                  + _sigmoid(gd_ref[...].astype(F32)) * yb).astype(o_ref.dtype)


def _merge(hf, hb, proj, norm_g, b_out, wa, wb):
    return pl.pallas_call(
        _merge_kernel,
        grid=(R // TM4, D_MODEL // TN4),
        in_specs=[
            pl.BlockSpec((TM4, M_WIDTH), lambda m, n: (m, 0)),
            pl.BlockSpec((TM4, M_WIDTH), lambda m, n: (m, 0)),
            pl.BlockSpec((TM4, M_WIDTH), lambda m, n: (m, C_MO // M_WIDTH)),
            pl.BlockSpec((1, M_WIDTH), lambda m, n: (0, 0)),
            pl.BlockSpec((TM4, D_WIDTH), lambda m, n: (m, 0)),
            pl.BlockSpec((M_WIDTH, TN4), lambda m, n: (0, n)),
            pl.BlockSpec((D_WIDTH, TN4), lambda m, n: (0, n)),
            pl.BlockSpec((TM4, TN4), lambda m, n: (m, C_GM // TN4 + n)),
            pl.BlockSpec((TM4, TN4), lambda m, n: (m, C_GD // TN4 + n)),
        ],
        out_specs=pl.BlockSpec((TM4, TN4), lambda m, n: (m, n)),
        out_shape=jax.ShapeDtypeStruct((R, D_MODEL), BF16),
        scratch_shapes=[pltpu.VMEM((TM4, M_WIDTH), BF16)],
        compiler_params=_cparams(("parallel", "arbitrary")),
        name="merge",
    )(hf, hb, proj, norm_g, b_out, wa, wb, proj, proj)


TM5 = 384


def _outproj_kernel(h0_ref, mg_ref, w_ref, g2_ref, h1_ref, u2_ref):
    h1 = h0_ref[...] + jnp.dot(mg_ref[...], w_ref[...], preferred_element_type=F32)
    h1_ref[...] = h1
    r = lax.rsqrt(jnp.mean(h1 * h1, axis=-1, keepdims=True) + EPS)
    u2_ref[...] = ((h1 * r) * g2_ref[...]).astype(BF16)


def _outproj(h0, merged, w, g2):
    return pl.pallas_call(
        _outproj_kernel,
        grid=(R // TM5,),
        in_specs=[
            pl.BlockSpec((TM5, D_MODEL), lambda m: (m, 0)),
            pl.BlockSpec((TM5, D_MODEL), lambda m: (m, 0)),
            pl.BlockSpec((D_MODEL, D_MODEL), lambda m: (0, 0)),
            pl.BlockSpec((1, D_MODEL), lambda m: (0, 0)),
        ],
        out_specs=[
            pl.BlockSpec((TM5, D_MODEL), lambda m: (m, 0)),
            pl.BlockSpec((TM5, D_MODEL), lambda m: (m, 0)),
        ],
        out_shape=[
            jax.ShapeDtypeStruct((R, D_MODEL), F32),
            jax.ShapeDtypeStruct((R, D_MODEL), BF16),
        ],
        compiler_params=_cparams(("parallel",)),
        name="outproj",
    )(h0, merged, w, g2)


TMU = 1024
TNU = 512
HALO = 16
NNU = FFN_DIM // TNU


def _ffn_up_kernel(x_ref, xp_ref, xn_ref, wg_ref, wv_ref, cg_ref, cv_ref, o_ref, xs_ref, sg_ref, sv_ref):
    @pl.when(pl.program_id(2) == 0)
    def _():
        last_tile = pl.program_id(1) == SEQ // TMU - 1
        xs_ref[0:HALO, :] = xp_ref[...]
        xs_ref[HALO:HALO + TMU, :] = x_ref[...]
        xn = xn_ref[...]
        xs_ref[HALO + TMU:, :] = jnp.where(last_tile, jnp.zeros_like(xn), xn)

    def branch(w_ref, c_ref, s_ref):
        s_ref[...] = jnp.dot(xs_ref[...], w_ref[...].astype(BF16), preferred_element_type=F32)
        c = c_ref[...]
        return (s_ref[HALO - 1:HALO - 1 + TMU, :] * c[0:1] + s_ref[HALO:HALO + TMU, :] * c[1:2]
                + s_ref[HALO + 1:HALO + 1 + TMU, :] * c[2:3])

    gate = branch(wg_ref, cg_ref, sg_ref)
    val = branch(wv_ref, cv_ref, sv_ref)
    o_ref[...] = (gate * _sigmoid(gate) * val).astype(o_ref.dtype)


def _ffn_up(u2_3, w_up, conv_w):
    hb = TMU // HALO
    meta_hblk = SEQ // HALO
    return pl.pallas_call(
        _ffn_up_kernel,
        grid=(BATCH, SEQ // TMU, NNU),
        in_specs=[
            pl.BlockSpec((None, TMU, D_MODEL), lambda b, i, n: (b, i, 0)),
            pl.BlockSpec((None, HALO, D_MODEL), lambda b, i, n: (b, jnp.where(i == 0, meta_hblk, i * hb - 1), 0)),
            pl.BlockSpec((None, HALO, D_MODEL), lambda b, i, n: (b, (i + 1) * hb, 0)),
            pl.BlockSpec((D_MODEL, TNU), lambda b, i, n: (0, n)),
            pl.BlockSpec((D_MODEL, TNU), lambda b, i, n: (0, NNU + n)),
            pl.BlockSpec((3, TNU), lambda b, i, n: (0, n)),
            pl.BlockSpec((3, TNU), lambda b, i, n: (0, NNU + n)),
        ],
        out_specs=pl.BlockSpec((None, TMU, TNU), lambda b, i, n: (b, i, n)),
        out_shape=jax.ShapeDtypeStruct((BATCH, SEQ, FFN_DIM), BF16),
        scratch_shapes=[
            pltpu.VMEM((TMU + 2 * HALO, D_MODEL), BF16),
            pltpu.VMEM((TMU + 2 * HALO, TNU), F32),
            pltpu.VMEM((TMU + 2 * HALO, TNU), F32),
        ],
        compiler_params=_cparams(("parallel", "parallel", "arbitrary")),
        name="ffn_up",
    )(u2_3, u2_3, u2_3, w_up, w_up, conv_w, conv_w)


TM7 = 1024
TK7 = 512
NK7 = FFN_DIM // TK7


def _down_kernel(a_ref, wd_ref, h1_ref, gf_ref, o_ref, acc_ref):
    k = pl.program_id(2)

    def part():
        return jnp.dot(a_ref[...], wd_ref[...], preferred_element_type=F32)

    @pl.when(k == 0)
    def _():
        acc_ref[...] = part()

    @pl.when(jnp.logical_and(k > 0, k < NK7 - 1))
    def _():
        acc_ref[...] += part()

    @pl.when(k == NK7 - 1)
    def _():
        h2 = h1_ref[...] + (acc_ref[...] + part())
        r = lax.rsqrt(jnp.mean(h2 * h2, axis=-1, keepdims=True) + EPS)
        o_ref[...] = (h2 * r) * gf_ref[...]


def _ffn_down(act, w_down, h1_3, gf):
    return pl.pallas_call(
        _down_kernel,
        grid=(BATCH, SEQ // TM7, NK7),
        in_specs=[
            pl.BlockSpec((None, TM7, TK7), lambda b, i, k: (b, i, k)),
            pl.BlockSpec((TK7, D_MODEL), lambda b, i, k: (k, 0)),
            pl.BlockSpec((None, TM7, D_MODEL), lambda b, i, k: (b, i, 0)),
            pl.BlockSpec((1, D_MODEL), lambda b, i, k: (0, 0)),
        ],
        out_specs=pl.BlockSpec((None, TM7, D_MODEL), lambda b, i, k: (b, i, 0)),
        out_shape=jax.ShapeDtypeStruct((BATCH, SEQ, D_MODEL), F32),
        scratch_shapes=[pltpu.VMEM((TM7, D_MODEL), F32)],
        compiler_params=_cparams(("parallel", "parallel", "arbitrary")),
        name="ffn_down",
    )(act, w_down, h1_3, gf)


def kernel(x, meta_tokens, norm1_g, w_in, mlstm_conv_w, mlstm_gate_bias, mlstm_norm_g, lambda_q1, lambda_k1,
           lambda_q2, lambda_k2, diff_subln_g, w_branch_m, w_branch_d, w_out, norm2_g, w_up, ffn_conv_w,
           w_down, norm_f_g):
    assert x.shape == (BATCH, SEQ, D_MODEL) and w_in.shape[0] == 1
    meta = jnp.broadcast_to(meta_tokens[None].astype(x.dtype), (BATCH, N_META, D_MODEL))
    pad = jnp.zeros((BATCH, LP - SEQ - N_META, D_MODEL), x.dtype)
    h0 = jnp.concatenate([x, meta, pad], axis=1)
    h0f = h0.reshape(R, D_MODEL)

    wit = w_in[0].T
    w_pa = w_pb = wit.astype(BF16)
    gbias = mlstm_gate_bias[0].reshape(GROWS, 1).astype(F32)
    g1 = norm1_g[0].reshape(1, D_MODEL)

    proj, gt_tiles = _inproj(h0f, g1, w_pa, w_pb, wit, gbias, _rope_tables())
    proj3 = proj.reshape(BATCH, LP, PROJ_W)

    gt = gt_tiles[:, :, :TM1].reshape(BATCH, LP // TM1, GROWS, TM1).transpose(0, 2, 1, 3).reshape(BATCH, GROWS, LP)
    arow, gcol = _scans(gt)

    kscale = jnp.concatenate([jnp.ones((1, M_WIDTH), F32), jnp.full((1, M_WIDTH), M_HEAD_DIM ** -0.5, F32)], axis=1)
    qk3 = _qkconv(proj3, mlstm_conv_w[0], kscale)
    hf, hb = _mlstm(qk3, proj3, gcol, arow)

    lam4 = jnp.stack([lambda_q1[0], lambda_k1[0], lambda_q2[0], lambda_k2[0]]).astype(F32)
    b_out = _attention(lam4, proj3, diff_subln_g[0].reshape(1, D_V_DIM))

    merged = _merge(hf.reshape(R, M_WIDTH), hb.reshape(R, M_WIDTH), proj, mlstm_norm_g[0].reshape(1, M_WIDTH),
                    b_out.reshape(R, D_WIDTH), w_branch_m[0].astype(BF16), w_branch_d[0].astype(BF16))
    h1, u2 = _outproj(h0f, merged, w_out[0].astype(BF16), norm2_g[0].reshape(1, D_MODEL))

    act = _ffn_up(u2.reshape(BATCH, LP, D_MODEL), w_up[0], ffn_conv_w[0])
    return _ffn_down(act, w_down[0].astype(BF16), h1.reshape(BATCH, LP, D_MODEL), norm_f_g.reshape(1, D_MODEL))
```

```python
import math

import numpy as np
import jax
import jax.numpy as jnp
from jax import lax
from jax.experimental import pallas as pl
from jax.experimental.pallas import tpu as pltpu

F32 = jnp.float32
BF16 = jnp.bfloat16

D_MODEL = 2048
BATCH = 2
SEQ = 4096
N_META = 16
EPS = 1e-6
M_HEADS = 4
M_WIDTH = 1024
M_HEAD_DIM = 256
D_HEADS = 8
D_V_DIM = 128
D_QK_DIM = 64
D_WIDTH = 1024
ROPE_THETA = 500000.0
ROPE_DIM = 16
FFN_DIM = 5632
LAM_INIT = 0.8 - 0.6 * math.exp(-0.3 * 0)

BLK = 128
LP = SEQ + BLK
NBLK = LP // BLK
META_BLK = SEQ // BLK
R = BATCH * LP
NEG = -1e30

C_MQK, C_MV, C_MO, C_DQ, C_DK, C_DV, C_GM, C_GD = 0, 2048, 3072, 4096, 5120, 6144, 7168, 9216
PROJ_W = 11264

VMEM_LIMIT = 56 * 1024 * 1024


def _cparams(sem):
    return pltpu.CompilerParams(dimension_semantics=sem, vmem_limit_bytes=VMEM_LIMIT)


def _sigmoid(x):
    return 0.5 * jnp.tanh(0.5 * x) + 0.5


def _silu_of_half(yh):
    return yh * (jnp.tanh(yh) + 1.0)


TM1 = 1056
TN1 = 1024
NORM_CH = 96


def _rmsnorm_rows_to(h_ref, g_ref, u_ref, tm):
    def body(i, carry):
        rows = pl.ds(pl.multiple_of(i * NORM_CH, NORM_CH), NORM_CH)
        x = h_ref[rows, :]
        r = lax.rsqrt(jnp.mean(x * x, axis=-1, keepdims=True) + EPS)
        u_ref[rows, :] = ((x * r) * g_ref[...]).astype(BF16)
        return carry
    lax.fori_loop(0, tm // NORM_CH, body, 0)


def _rope_tile(acc, c, s):
    lane = lax.broadcasted_iota(jnp.int32, c.shape, 1)
    first = (lane % D_QK_DIM) < (ROPE_DIM // 2)
    outs = []
    for g in range(acc.shape[1] // 128):
        y = acc[:, g * 128:(g + 1) * 128]
        rot = jnp.where(first, pltpu.roll(y, 128 - ROPE_DIM // 2, 1), pltpu.roll(y, ROPE_DIM // 2, 1))
        outs.append(y * c + rot * s)
    return jnp.concatenate(outs, axis=1)


NA1 = C_DQ // TN1
GROWS = 16
GATE_COL0 = 4 * M_WIDTH
GT_PAD = -(-TM1 // 128) * 128


def _inproj_kernel(h_ref, g_ref, wa_ref, wb_ref, wg_ref, gb_ref, cq_ref, sq_ref, ck_ref, sk_ref,
                   o_ref, gt_ref, u_ref):
    n = pl.program_id(1)

    @pl.when(n == 0)
    def _():
        _rmsnorm_rows_to(h_ref, g_ref, u_ref, TM1)
        gt = lax.dot_general(wg_ref[...].astype(BF16), u_ref[...], (((1,), (1,)), ((), ())),
                             preferred_element_type=F32)
        gt_ref[:, :TM1] = gt + gb_ref[...]
        gt_ref[:, TM1:] = jnp.zeros((GROWS, GT_PAD - TM1), F32)

    def mm(wt_ref):
        return lax.dot_general(u_ref[...], wt_ref[...], (((1,), (1,)), ((), ())), preferred_element_type=F32)

    q0, k0, v0 = C_DQ // TN1, C_DK // TN1, C_DV // TN1

    @pl.when(n < NA1)
    def _():
        o_ref[...] = mm(wa_ref).astype(o_ref.dtype)

    @pl.when(jnp.logical_and(n >= q0, n < k0))
    def _():
        o_ref[...] = _rope_tile(mm(wb_ref), cq_ref[...], sq_ref[...]).astype(o_ref.dtype)

    @pl.when(jnp.logical_and(n >= k0, n < v0))
    def _():
        o_ref[...] = _rope_tile(mm(wb_ref), ck_ref[...], sk_ref[...]).astype(o_ref.dtype)

    @pl.when(n >= v0)
    def _():
        o_ref[...] = mm(wb_ref).astype(o_ref.dtype)


def _inproj(h, g, wa, wb, w_in_t, gbias, tabs):
    tiles_per_batch = LP // TM1
    tab_spec = pl.BlockSpec((TM1, 128), lambda m, n: (m % tiles_per_batch, 0))
    return pl.pallas_call(
        _inproj_kernel,
        grid=(R // TM1, PROJ_W // TN1),
        in_specs=[
            pl.BlockSpec((TM1, D_MODEL), lambda m, n: (m, 0)),
            pl.BlockSpec((1, D_MODEL), lambda m, n: (0, 0)),
            pl.BlockSpec((TN1, D_MODEL), lambda m, n: (jnp.minimum(n, NA1 - 1), 0)),
            pl.BlockSpec((pl.Element(TN1), pl.Element(D_MODEL)),
                         lambda m, n: (pl.multiple_of(GATE_COL0 + GROWS + jnp.maximum(n - NA1, 0) * TN1, GROWS), 0)),
            pl.BlockSpec((GROWS, D_MODEL), lambda m, n: (GATE_COL0 // GROWS, 0)),
            pl.BlockSpec((GROWS, 1), lambda m, n: (0, 0)),
            tab_spec, tab_spec, tab_spec, tab_spec,
        ],
        out_specs=[
            pl.BlockSpec((TM1, TN1), lambda m, n: (m, n)),
            pl.BlockSpec((None, GROWS, GT_PAD), lambda m, n: (m, 0, 0)),
        ],
        out_shape=[
            jax.ShapeDtypeStruct((R, PROJ_W), BF16),
            jax.ShapeDtypeStruct((R // TM1, GROWS, GT_PAD), F32),
        ],
        scratch_shapes=[pltpu.VMEM((TM1, D_MODEL), BF16)],
        compiler_params=_cparams(("parallel", "arbitrary")),
        name="inproj",
    )(h, g, wa, wb, w_in_t, gbias, *tabs)


def _rope_tables():
    pos = np.zeros((LP,), np.float64)
    pos[:SEQ] = np.arange(SEQ) + N_META
    pos[SEQ:SEQ + N_META] = np.arange(N_META)
    half = ROPE_DIM // 2
    inv = ROPE_THETA ** (-np.arange(0, ROPE_DIM, 2, dtype=np.float64) / ROPE_DIM)
    ang = pos[:, None] * inv[None, :]
    cos64 = np.ones((LP, D_QK_DIM)); sin64 = np.zeros((LP, D_QK_DIM))
    cos64[:, :half] = np.cos(ang); cos64[:, half:ROPE_DIM] = np.cos(ang)
    sin64[:, :half] = -np.sin(ang); sin64[:, half:ROPE_DIM] = np.sin(ang)
    cos = np.concatenate([cos64, cos64], axis=1)
    sin = np.concatenate([sin64, sin64], axis=1)
    qs = D_QK_DIM ** -0.5 * math.log2(math.e)
    return [jnp.asarray(t, F32) for t in (cos * qs, sin * qs, cos, sin)]


def _lane_scan(x, op, ident, reverse):
    lane = lax.broadcasted_iota(jnp.int32, x.shape, 1)
    k = 1
    while k < BLK:
        if reverse:
            sh = pltpu.roll(x, BLK - k, 1)
            x = op(x, jnp.where(lane < BLK - k, sh, ident))
        else:
            sh = pltpu.roll(x, k, 1)
            x = op(x, jnp.where(lane >= k, sh, ident))
        k *= 2
    return x


def _log_sigmoid(x):
    return jnp.minimum(x, 0.0) - jnp.log1p(jnp.exp(-jnp.abs(x)))


def _scan_kernel(g_ref, arow_ref, gcol_ref, colt_ref):
    colt_ref[...] = jnp.zeros_like(colt_ref)
    lane = lax.broadcasted_iota(jnp.int32, (8, BLK), 1)
    for d in range(2):
        reverse = d == 1
        carry_b = jnp.zeros((8, BLK), F32)
        carry_m = jnp.full((8, BLK), NEG, F32)
        logical = [META_BLK] + list(range(META_BLK))
        order = logical[::-1] if reverse else logical
        edge = 0 if reverse else BLK - 1
        for j in order:
            cols = slice(j * BLK, (j + 1) * BLK)
            li = g_ref[d * 8:(d + 1) * 8, cols]
            lf = pltpu.roll(_log_sigmoid(li), M_HEADS, 0)
            if j == META_BLK:
                valid = lane < N_META
                li = jnp.where(valid, li, NEG)
                lf = jnp.where(valid, lf, 0.0)
            bc = carry_b + _lane_scan(lf, jnp.add, 0.0, reverse)
            a = li - bc
            m = jnp.maximum(carry_m, _lane_scan(a, jnp.maximum, NEG, reverse))
            m_last = jnp.broadcast_to(m[:, edge:edge + 1], (8, BLK))
            arow_ref[d * 8:(d + 1) * 8, cols] = a
            base = d * 40
            colt_ref[j, base:base + 8, :] = a
            colt_ref[j, base + 8:base + 16, :] = m
            colt_ref[j, base + 16:base + 24, :] = bc + m
            colt_ref[j, base + 24:base + 32, :] = carry_m
            colt_ref[j, base + 32:base + 40, :] = m_last
            carry_b = jnp.broadcast_to(bc[:, edge:edge + 1], (8, BLK))
            carry_m = m_last
    for j in range(NBLK):
        gcol_ref[j * BLK:(j + 1) * BLK, :] = colt_ref[j].T


def _scans(gt):
    return pl.pallas_call(
        _scan_kernel,
        grid=(BATCH,),
        in_specs=[pl.BlockSpec((None, GROWS, LP), lambda b: (b, 0, 0))],
        out_specs=[
            pl.BlockSpec((None, 16, LP), lambda b: (b, 0, 0)),
            pl.BlockSpec((None, LP, BLK), lambda b: (b, 0, 0)),
        ],
        out_shape=[
            jax.ShapeDtypeStruct((BATCH, 16, LP), F32),
            jax.ShapeDtypeStruct((BATCH, LP, BLK), F32),
        ],
        scratch_shapes=[pltpu.VMEM((NBLK, BLK, BLK), F32)],
        compiler_params=_cparams(("parallel",)),
        name="gate_scans",
    )(gt)


TC_COLS = 256
CONV_CH = 256


def _seq_conv_rows(s_ref, w, lo, n):
    return (s_ref[lo - 1:lo - 1 + n, :] * w[0:1] + s_ref[lo:lo + n, :] * w[1:2]
            + s_ref[lo + 1:lo + 1 + n, :] * w[2:3])


def _qkconv_kernel(x_ref, w_ref, sc_ref, o_ref, s_ref):
    xo = 8 + N_META
    s_ref[0:8, :] = jnp.zeros((8, TC_COLS), F32)
    s_ref[xo + SEQ:xo + SEQ + 8, :] = jnp.zeros((8, TC_COLS), F32)
    s_ref[8:xo, :] = x_ref[SEQ:SEQ + N_META, :].astype(F32)
    for i in range(SEQ // CONV_CH):
        s_ref[xo + i * CONV_CH:xo + (i + 1) * CONV_CH, :] = x_ref[i * CONV_CH:(i + 1) * CONV_CH, :].astype(F32)
    w = 0.5 * w_ref[...]
    sc = sc_ref[...]

    def act(yh):
        return (_silu_of_half(yh) * sc).astype(o_ref.dtype)

    for i in range(SEQ // CONV_CH):
        o_ref[i * CONV_CH:(i + 1) * CONV_CH, :] = act(_seq_conv_rows(s_ref, w, xo + i * CONV_CH, CONV_CH))
    o_ref[SEQ:SEQ + N_META, :] = act(_seq_conv_rows(s_ref, w, 8, N_META))
    o_ref[SEQ + N_META:LP, :] = jnp.zeros((LP - SEQ - N_META, TC_COLS), o_ref.dtype)


def _qkconv(proj3, w, scale):
    return pl.pallas_call(
        _qkconv_kernel,
        grid=(BATCH, 2 * M_WIDTH // TC_COLS),
        in_specs=[
            pl.BlockSpec((None, LP, TC_COLS), lambda b, c: (b, 0, C_MQK // TC_COLS + c)),
            pl.BlockSpec((3, TC_COLS), lambda b, c: (0, c)),
            pl.BlockSpec((1, TC_COLS), lambda b, c: (0, c)),
        ],
        out_specs=pl.BlockSpec((None, LP, TC_COLS), lambda b, c: (b, 0, c)),
        out_shape=jax.ShapeDtypeStruct((BATCH, LP, 2 * M_WIDTH), BF16),
        scratch_shapes=[pltpu.VMEM((SEQ + N_META + 16, TC_COLS), F32)],
        compiler_params=_cparams(("parallel", "parallel")),
        name="qk_conv",
    )(proj3, w, scale)


def _mlstm_kernel(qf, kf, vf, gcf, arf, qb, kb, vb, gcb, arb, hf_ref, hb_ref, c_ref, n_ref):
    c = pl.program_id(1)

    @pl.when(c == 0)
    def _():
        c_ref[...] = jnp.zeros_like(c_ref)
        n_ref[...] = jnp.zeros_like(n_ref)

    row = lax.broadcasted_iota(jnp.int32, (BLK, BLK), 0)
    col = lax.broadcasted_iota(jnp.int32, (BLK, BLK), 1)
    dirs = ((qf, kf, vf, gcf, arf, hf_ref), (qb, kb, vb, gcb, arb, hb_ref))
    for d, (q_ref, k_ref, v_ref, gc_ref, ar_ref, o_ref) in enumerate(dirs):
        mask = (col <= row) if d == 0 else (col >= row)
        gc = gc_ref[...]
        for h in range(M_HEADS):
            st = d * M_HEADS + h
            def colq(qi):
                j = d * 40 + qi * 8 + h
                return gc[:, j:j + 1]
            a_col, m_col, e_col, mp_col, ml_col = (colq(i) for i in range(5))
            a_row = ar_ref[d * 8 + h:d * 8 + h + 1, :]
            hs = slice(h * M_HEAD_DIM, (h + 1) * M_HEAD_DIM)
            q = q_ref[:, hs]
            k = k_ref[:, hs]
            v = v_ref[:, hs]
            s = lax.dot_general(q, k, (((1,), (1,)), ((), ())), preferred_element_type=F32)
            p = jnp.where(mask, jnp.exp(a_row - m_col), 0.0)
            sp = s * p
            a = jnp.exp(mp_col - m_col)
            ct = c_ref[st]
            nrow = n_ref[st]
            inter = jnp.dot(q, ct.astype(BF16), preferred_element_type=F32)
            num = jnp.dot(sp.astype(BF16), v, preferred_element_type=F32) + a * inter
            qn = jnp.sum(q.astype(F32) * nrow, axis=1, keepdims=True)
            den = jnp.sum(sp, axis=1, keepdims=True) + a * qn
            o_ref[:, hs] = num / jnp.maximum(jnp.abs(den), jnp.exp(-e_col))
            w = jnp.exp(a_col - ml_col)
            dec = jnp.exp(mp_col[0:1, :] - ml_col[0:1, :])
            wv = (w * v.astype(F32)).astype(BF16)
            c_ref[st] = dec * ct + lax.dot_general(
                k, wv, (((0,), (0,)), ((), ())), preferred_element_type=F32)
            n_ref[st] = dec * nrow + jnp.sum(w * k.astype(F32), axis=0, keepdims=True)


def _fwd_blk(c):
    return (c + NBLK - 1) % NBLK


def _bwd_blk(c):
    return (2 * NBLK - 2 - c) % NBLK


def _mlstm(qk3, proj3, gcol, arow):
    def specs(blk):
        return [
            pl.BlockSpec((None, BLK, M_WIDTH), lambda b, c: (b, blk(c), 0)),
            pl.BlockSpec((None, BLK, M_WIDTH), lambda b, c: (b, blk(c), 1)),
            pl.BlockSpec((None, BLK, M_WIDTH), lambda b, c: (b, blk(c), C_MV // M_WIDTH)),
            pl.BlockSpec((None, BLK, BLK), lambda b, c: (b, blk(c), 0)),
            pl.BlockSpec((None, 16, BLK), lambda b, c: (b, 0, blk(c))),
        ]
    out_sd = jax.ShapeDtypeStruct((BATCH, LP, M_WIDTH), F32)
    return pl.pallas_call(
        _mlstm_kernel,
        grid=(BATCH, NBLK),
        in_specs=specs(_fwd_blk) + specs(_bwd_blk),
        out_specs=[
            pl.BlockSpec((None, BLK, M_WIDTH), lambda b, c: (b, _fwd_blk(c), 0)),
            pl.BlockSpec((None, BLK, M_WIDTH), lambda b, c: (b, _bwd_blk(c), 0)),
        ],
        out_shape=[out_sd, out_sd],
        scratch_shapes=[
            pltpu.VMEM((2 * M_HEADS, M_HEAD_DIM, M_HEAD_DIM), F32),
            pltpu.VMEM((2 * M_HEADS, 1, M_HEAD_DIM), F32),
        ],
        compiler_params=_cparams(("parallel", "arbitrary")),
        name="mlstm",
    )(qk3, qk3, proj3, gcol, arow, qk3, qk3, proj3, gcol, arow)


RQ = 128
NGRP = LP // RQ
EXP_CH = 512
ATTN_NBUF = 2


def _attn_kernel(lam_ref, q_ref, k_ref, v_ref, g_ref, o_ref, *scratch):
    s_refs, p_refs, l_refs, m_refs = (scratch[i * ATTN_NBUF:(i + 1) * ATTN_NBUF] for i in range(4))
    lq = lam_ref[...]
    la = jnp.sum(lq[0:1] * lq[1:2], axis=1, keepdims=True)
    lb = jnp.sum(lq[2:3] * lq[3:4], axis=1, keepdims=True)
    lam = jnp.exp(la) - jnp.exp(lb) + LAM_INIT

    lane = lax.broadcasted_iota(jnp.int32, (RQ, 128), 1)
    colid = lax.broadcasted_iota(jnp.int32, (1, BLK), 1)

    def rows(g):
        return pl.ds(pl.multiple_of(g * RQ, RQ), RQ)

    def stage_scores(g, s_ref, m_ref):
        qg = q_ref[rows(g), :]
        zero = jnp.zeros_like(qg)
        q2 = jnp.concatenate([jnp.where(lane < D_QK_DIM, qg, zero), jnp.where(lane >= D_QK_DIM, qg, zero)], axis=0)
        s = lax.dot_general(q2, k_ref[...], (((1,), (1,)), ((), ())), preferred_element_type=F32)
        tail = jnp.where(colid < N_META, s[:, LP - BLK:], -jnp.inf)
        s_ref[:, :LP - BLK] = s[:, :LP - BLK]
        s_ref[:, LP - BLK:] = tail
        m = jnp.maximum(jnp.max(s[:, :LP - BLK], axis=1, keepdims=True), jnp.max(tail, axis=1, keepdims=True))
        m_ref[...] = jnp.broadcast_to(m, m_ref.shape)

    def stage_exp(s_ref, m_ref, p_ref, l_ref):
        m = m_ref[:, 0:1]
        part = jnp.zeros((2 * RQ, 128), F32)
        for c0 in range(0, LP, EXP_CH):
            c1 = min(c0 + EXP_CH, LP)
            pc = jnp.exp2(s_ref[:, c0:c1] - m)
            for j in range(0, c1 - c0, 128):
                part = part + pc[:, j:j + 128]
            p_ref[:, c0:c1] = pc.astype(BF16)
        l_ref[...] = jnp.broadcast_to(jnp.sum(part, axis=1, keepdims=True), l_ref.shape)

    def stage_pv(g, p_ref, l_ref):
        l = l_ref[:, 0:1]
        ratio = (lam * l[:RQ] / l[RQ:]).astype(BF16)
        a = p_ref[:RQ, :] - ratio * p_ref[RQ:, :]
        o = jnp.dot(a, v_ref[...], preferred_element_type=F32) / l[:RQ]
        r = lax.rsqrt(jnp.mean(o * o, axis=-1, keepdims=True) + EPS)
        o_ref[rows(g), :] = (((o * r) * g_ref[...]) * (1.0 - LAM_INIT)).astype(o_ref.dtype)

    nb = len(s_refs)
    sbuf = tuple(zip(s_refs, m_refs))
    pbuf = tuple(zip(p_refs, l_refs))

    def tick(t, static):
        if static < NGRP:
            stage_scores(t, *sbuf[static % nb])
        if 1 <= static <= NGRP:
            stage_exp(*sbuf[(static - 1) % nb], *pbuf[(static - 1) % nb])
        if 2 <= static <= NGRP + 1:
            stage_pv(t - 2, *pbuf[(static - 2) % nb])

    tick(0, 0)
    tick(1, 1)
    n_rounds = (NGRP - 2) // nb

    def body(i, carry):
        t = nb * i + 2
        for j in range(nb):
            tick(t + j, 2 + j)
        return carry

    lax.fori_loop(0, n_rounds, body, 0)
    for t in range(2 + nb * n_rounds, NGRP + 2):
        tick(t, t)


def _attention(lam4, proj3, subln_g):
    return pl.pallas_call(
        _attn_kernel,
        grid=(BATCH, D_HEADS),
        in_specs=[
            pl.BlockSpec((4, D_QK_DIM), lambda b, h: (0, 0)),
            pl.BlockSpec((None, LP, 128), lambda b, h: (b, 0, C_DQ // 128 + h)),
            pl.BlockSpec((None, LP, 128), lambda b, h: (b, 0, C_DK // 128 + h)),
            pl.BlockSpec((None, LP, 128), lambda b, h: (b, 0, C_DV // 128 + h)),
            pl.BlockSpec((1, D_V_DIM), lambda b, h: (0, 0)),
        ],
        out_specs=pl.BlockSpec((None, LP, D_V_DIM), lambda b, h: (b, 0, h)),
        out_shape=jax.ShapeDtypeStruct((BATCH, LP, D_WIDTH), BF16),
        scratch_shapes=(
            [pltpu.VMEM((2 * RQ, LP), F32)] * ATTN_NBUF
            + [pltpu.VMEM((2 * RQ, LP), BF16)] * ATTN_NBUF
            + [pltpu.VMEM((2 * RQ, 128), F32)] * ATTN_NBUF
            + [pltpu.VMEM((2 * RQ, 128), F32)] * ATTN_NBUF
        ),
        compiler_params=_cparams(("parallel", "parallel")),
        name="diff_attn",
    )(lam4, proj3, proj3, proj3, subln_g)


TM4 = 1056
TN4 = 512
LN_CH = 96


def _merge_kernel(hf_ref, hb_ref, mo_ref, ng_ref, bo_ref, wa_ref, wb_ref, gm_ref, gd_ref, o_ref, a_ref):
    n = pl.program_id(1)

    @pl.when(n == 0)
    def _():
        def body(i, carry):
            rows = pl.ds(pl.multiple_of(i * LN_CH, LN_CH), LN_CH)
            for h in range(M_HEADS):
                hs = slice(h * M_HEAD_DIM, (h + 1) * M_HEAD_DIM)
                x = hf_ref[rows, hs] + hb_ref[rows, hs]
                mu = jnp.mean(x, axis=-1, keepdims=True)
                xc = x - mu
                var = jnp.mean(xc * xc, axis=-1, keepdims=True)
                hn = (xc * lax.rsqrt(var + EPS)) * ng_ref[:, hs]
                a_ref[rows, hs] = (_sigmoid(mo_ref[rows, hs].astype(F32)) * hn).astype(BF16)
            return carry
        lax.fori_loop(0, TM4 // LN_CH, body, 0)

    ya = jnp.dot(a_ref[...], wa_ref[...], preferred_element_type=F32)
    yb = jnp.dot(bo_ref[...], wb_ref[...], preferred_element_type=F32)
    o_ref[...] = (_sigmoid(gm_ref[...].astype(F32)) * ya
                  + _sigmoid(gd_ref[...].astype(F32)) * yb).astype(o_ref.dtype)


def _merge(hf, hb, proj, norm_g, b_out, wa, wb):
    return pl.pallas_call(
        _merge_kernel,
        grid=(R // TM4, D_MODEL // TN4),
        in_specs=[
            pl.BlockSpec((TM4, M_WIDTH), lambda m, n: (m, 0)),
            pl.BlockSpec((TM4, M_WIDTH), lambda m, n: (m, 0)),
            pl.BlockSpec((TM4, M_WIDTH), lambda m, n: (m, C_MO // M_WIDTH)),
            pl.BlockSpec((1, M_WIDTH), lambda m, n: (0, 0)),
            pl.BlockSpec((TM4, D_WIDTH), lambda m, n: (m, 0)),
            pl.BlockSpec((M_WIDTH, TN4), lambda m, n: (0, n)),
            pl.BlockSpec((D_WIDTH, TN4), lambda m, n: (0, n)),
            pl.BlockSpec((TM4, TN4), lambda m, n: (m, C_GM // TN4 + n)),
            pl.BlockSpec((TM4, TN4), lambda m, n: (m, C_GD // TN4 + n)),
        ],
        out_specs=pl.BlockSpec((TM4, TN4), lambda m, n: (m, n)),
        out_shape=jax.ShapeDtypeStruct((R, D_MODEL), BF16),
        scratch_shapes=[pltpu.VMEM((TM4, M_WIDTH), BF16)],
        compiler_params=_cparams(("parallel", "arbitrary")),
        name="merge",
    )(hf, hb, proj, norm_g, b_out, wa, wb, proj, proj)


TM5 = 384


def _outproj_kernel(h0_ref, mg_ref, w_ref, g2_ref, h1_ref, u2_ref):
    h1 = h0_ref[...] + jnp.dot(mg_ref[...], w_ref[...], preferred_element_type=F32)
    h1_ref[...] = h1
    r = lax.rsqrt(jnp.mean(h1 * h1, axis=-1, keepdims=True) + EPS)
    u2_ref[...] = ((h1 * r) * g2_ref[...]).astype(BF16)


def _outproj(h0, merged, w, g2):
    return pl.pallas_call(
        _outproj_kernel,
        grid=(R // TM5,),
        in_specs=[
            pl.BlockSpec((TM5, D_MODEL), lambda m: (m, 0)),
            pl.BlockSpec((TM5, D_MODEL), lambda m: (m, 0)),
            pl.BlockSpec((D_MODEL, D_MODEL), lambda m: (0, 0)),
            pl.BlockSpec((1, D_MODEL), lambda m: (0, 0)),
        ],
        out_specs=[
            pl.BlockSpec((TM5, D_MODEL), lambda m: (m, 0)),
            pl.BlockSpec((TM5, D_MODEL), lambda m: (m, 0)),
        ],
        out_shape=[
            jax.ShapeDtypeStruct((R, D_MODEL), F32),
            jax.ShapeDtypeStruct((R, D_MODEL), BF16),
        ],
        compiler_params=_cparams(("parallel",)),
        name="outproj",
    )(h0, merged, w, g2)


TMU = 1024
TNU = 512
HALO = 16
NNU = FFN_DIM // TNU


def _ffn_up_kernel(x_ref, xp_ref, xn_ref, wg_ref, wv_ref, cg_ref, cv_ref, o_ref, xs_ref, sg_ref, sv_ref):
    @pl.when(pl.program_id(2) == 0)
    def _():
        last_tile = pl.program_id(1) == SEQ // TMU - 1
        xs_ref[0:HALO, :] = xp_ref[...]
        xs_ref[HALO:HALO + TMU, :] = x_ref[...]
        xn = xn_ref[...]
        xs_ref[HALO + TMU:, :] = jnp.where(last_tile, jnp.zeros_like(xn), xn)

    def branch(w_ref, c, s_ref):
        s_ref[...] = jnp.dot(xs_ref[...], w_ref[...].astype(BF16), preferred_element_type=F32)
        return (s_ref[HALO - 1:HALO - 1 + TMU, :] * c[0:1] + s_ref[HALO:HALO + TMU, :] * c[1:2]
                + s_ref[HALO + 1:HALO + 1 + TMU, :] * c[2:3])

    gate_half = branch(wg_ref, 0.5 * cg_ref[...], sg_ref)
    val = branch(wv_ref, cv_ref[...], sv_ref)
    o_ref[...] = (_silu_of_half(gate_half) * val).astype(o_ref.dtype)


def _ffn_up(u2_3, w_up, conv_w):
    hb = TMU // HALO
    meta_hblk = SEQ // HALO
    return pl.pallas_call(
        _ffn_up_kernel,
        grid=(BATCH, SEQ // TMU, NNU),
        in_specs=[
            pl.BlockSpec((None, TMU, D_MODEL), lambda b, i, n: (b, i, 0)),
            pl.BlockSpec((None, HALO, D_MODEL), lambda b, i, n: (b, jnp.where(i == 0, meta_hblk, i * hb - 1), 0)),
            pl.BlockSpec((None, HALO, D_MODEL), lambda b, i, n: (b, (i + 1) * hb, 0)),
            pl.BlockSpec((D_MODEL, TNU), lambda b, i, n: (0, n)),
            pl.BlockSpec((D_MODEL, TNU), lambda b, i, n: (0, NNU + n)),
            pl.BlockSpec((3, TNU), lambda b, i, n: (0, n)),
            pl.BlockSpec((3, TNU), lambda b, i, n: (0, NNU + n)),
        ],
        out_specs=pl.BlockSpec((None, TMU, TNU), lambda b, i, n: (b, i, n)),
        out_shape=jax.ShapeDtypeStruct((BATCH, SEQ, FFN_DIM), BF16),
        scratch_shapes=[
            pltpu.VMEM((TMU + 2 * HALO, D_MODEL), BF16),
            pltpu.VMEM((TMU + 2 * HALO, TNU), F32),
            pltpu.VMEM((TMU + 2 * HALO, TNU), F32),
        ],
        compiler_params=_cparams(("parallel", "parallel", "arbitrary")),
        name="ffn_up",
    )(u2_3, u2_3, u2_3, w_up, w_up, conv_w, conv_w)


TM7 = 1024
TK7 = 512
NK7 = FFN_DIM // TK7


def _down_kernel(a_ref, wd_ref, h1_ref, gf_ref, o_ref, acc_ref):
    k = pl.program_id(2)

    def part():
        return jnp.dot(a_ref[...], wd_ref[...], preferred_element_type=F32)

    @pl.when(k == 0)
    def _():
        acc_ref[...] = part()

    @pl.when(jnp.logical_and(k > 0, k < NK7 - 1))
    def _():
        acc_ref[...] += part()

    @pl.when(k == NK7 - 1)
    def _():
        h2 = h1_ref[...] + (acc_ref[...] + part())
        r = lax.rsqrt(jnp.mean(h2 * h2, axis=-1, keepdims=True) + EPS)
        o_ref[...] = (h2 * r) * gf_ref[...]


def _ffn_down(act, w_down, h1_3, gf):
    return pl.pallas_call(
        _down_kernel,
        grid=(BATCH, SEQ // TM7, NK7),
        in_specs=[
            pl.BlockSpec((None, TM7, TK7), lambda b, i, k: (b, i, k)),
            pl.BlockSpec((TK7, D_MODEL), lambda b, i, k: (k, 0)),
            pl.BlockSpec((None, TM7, D_MODEL), lambda b, i, k: (b, i, 0)),
            pl.BlockSpec((1, D_MODEL), lambda b, i, k: (0, 0)),
        ],
        out_specs=pl.BlockSpec((None, TM7, D_MODEL), lambda b, i, k: (b, i, 0)),
        out_shape=jax.ShapeDtypeStruct((BATCH, SEQ, D_MODEL), F32),
        scratch_shapes=[pltpu.VMEM((TM7, D_MODEL), F32)],
        compiler_params=_cparams(("parallel", "parallel", "arbitrary")),
        name="ffn_down",
    )(act, w_down, h1_3, gf)


def kernel(x, meta_tokens, norm1_g, w_in, mlstm_conv_w, mlstm_gate_bias, mlstm_norm_g, lambda_q1, lambda_k1,
           lambda_q2, lambda_k2, diff_subln_g, w_branch_m, w_branch_d, w_out, norm2_g, w_up, ffn_conv_w,
           w_down, norm_f_g):
    assert x.shape == (BATCH, SEQ, D_MODEL) and w_in.shape[0] == 1
    meta = jnp.broadcast_to(meta_tokens[None].astype(x.dtype), (BATCH, N_META, D_MODEL))
    pad = jnp.zeros((BATCH, LP - SEQ - N_META, D_MODEL), x.dtype)
    h0 = jnp.concatenate([x, meta, pad], axis=1)
    h0f = h0.reshape(R, D_MODEL)

    wit = w_in[0].T
    w_pa = w_pb = wit.astype(BF16)
    gbias = mlstm_gate_bias[0].reshape(GROWS, 1).astype(F32)
    g1 = norm1_g[0].reshape(1, D_MODEL)

    proj, gt_tiles = _inproj(h0f, g1, w_pa, w_pb, wit, gbias, _rope_tables())
    proj3 = proj.reshape(BATCH, LP, PROJ_W)

    gt = gt_tiles[:, :, :TM1].reshape(BATCH, LP // TM1, GROWS, TM1).transpose(0, 2, 1, 3).reshape(BATCH, GROWS, LP)
    arow, gcol = _scans(gt)

    kscale = jnp.concatenate([jnp.ones((1, M_WIDTH), F32), jnp.full((1, M_WIDTH), M_HEAD_DIM ** -0.5, F32)], axis=1)
    qk3 = _qkconv(proj3, mlstm_conv_w[0], kscale)
    hf, hb = _mlstm(qk3, proj3, gcol, arow)

    lam4 = jnp.stack([lambda_q1[0], lambda_k1[0], lambda_q2[0], lambda_k2[0]]).astype(F32)
    b_out = _attention(lam4, proj3, diff_subln_g[0].reshape(1, D_V_DIM))

    merged = _merge(hf.reshape(R, M_WIDTH), hb.reshape(R, M_WIDTH), proj, mlstm_norm_g[0].reshape(1, M_WIDTH),
                    b_out.reshape(R, D_WIDTH), w_branch_m[0].astype(BF16), w_branch_d[0].astype(BF16))
    h1, u2 = _outproj(h0f, merged, w_out[0].astype(BF16), norm2_g[0].reshape(1, D_MODEL))

    act = _ffn_up(u2.reshape(BATCH, LP, D_MODEL), w_up[0], ffn_conv_w[0])
    return _ffn_down(act, w_down[0].astype(BF16), h1.reshape(BATCH, LP, D_MODEL), norm_f_g.reshape(1, D_MODEL))
```

```python
import math

import numpy as np
import jax
import jax.numpy as jnp
from jax import lax
from jax.experimental import pallas as pl
from jax.experimental.pallas import tpu as pltpu

F32 = jnp.float32
BF16 = jnp.bfloat16

D_MODEL = 2048
BATCH = 2
SEQ = 4096
N_META = 16
EPS = 1e-6
M_HEADS = 4
M_WIDTH = 1024
M_HEAD_DIM = 256
D_HEADS = 8
D_V_DIM = 128
D_QK_DIM = 64
D_WIDTH = 1024
ROPE_THETA = 500000.0
ROPE_DIM = 16
FFN_DIM = 5632
LAM_INIT = 0.8 - 0.6 * math.exp(-0.3 * 0)

BLK = 128
LP = SEQ + BLK
NBLK = LP // BLK
META_BLK = SEQ // BLK
R = BATCH * LP
NEG = -1e30

C_MQK, C_MV, C_MO, C_DQ, C_DK, C_DV, C_GM, C_GD = 0, 2048, 3072, 4096, 5120, 6144, 7168, 9216
PROJ_W = 11264

VMEM_LIMIT = 56 * 1024 * 1024


def _cparams(sem):
    return pltpu.CompilerParams(dimension_semantics=sem, vmem_limit_bytes=VMEM_LIMIT)


def _twice_sigmoid_of_half(xh):
    return jnp.tanh(xh) + 1.0


def _silu_of_half(yh):
    return yh * (jnp.tanh(yh) + 1.0)


TM1 = 1056
TN1 = 1024
NORM_CH = 96


def _rmsnorm_rows_to(h_ref, g_ref, u_ref, tm):
    def body(i, carry):
        rows = pl.ds(pl.multiple_of(i * NORM_CH, NORM_CH), NORM_CH)
        x = h_ref[rows, :]
        r = lax.rsqrt(jnp.mean(x * x, axis=-1, keepdims=True) + EPS)
        u_ref[rows, :] = ((x * r) * g_ref[...]).astype(BF16)
        return carry
    lax.fori_loop(0, tm // NORM_CH, body, 0)


def _rope_tile(acc, c, s):
    lane = lax.broadcasted_iota(jnp.int32, c.shape, 1)
    first = (lane % D_QK_DIM) < (ROPE_DIM // 2)
    outs = []
    for g in range(acc.shape[1] // 128):
        y = acc[:, g * 128:(g + 1) * 128]
        rot = jnp.where(first, pltpu.roll(y, 128 - ROPE_DIM // 2, 1), pltpu.roll(y, ROPE_DIM // 2, 1))
        outs.append(y * c + rot * s)
    return jnp.concatenate(outs, axis=1)


NA1 = C_DQ // TN1
GROWS = 16
GATE_COL0 = 4 * M_WIDTH
GT_PAD = -(-TM1 // 128) * 128


def _inproj_kernel(h_ref, g_ref, wa_ref, wb_ref, wg_ref, gb_ref, cq_ref, sq_ref, ck_ref, sk_ref,
                   o_ref, gt_ref, u_ref):
    n = pl.program_id(1)

    @pl.when(n == 0)
    def _():
        _rmsnorm_rows_to(h_ref, g_ref, u_ref, TM1)
        gt = lax.dot_general(wg_ref[...].astype(BF16), u_ref[...], (((1,), (1,)), ((), ())),
                             preferred_element_type=F32)
        gt_ref[:, :TM1] = gt + gb_ref[...]
        gt_ref[:, TM1:] = jnp.zeros((GROWS, GT_PAD - TM1), F32)

    def mm(wt_ref):
        return lax.dot_general(u_ref[...], wt_ref[...], (((1,), (1,)), ((), ())), preferred_element_type=F32)

    q0, k0, v0 = C_DQ // TN1, C_DK // TN1, C_DV // TN1

    @pl.when(n < NA1)
    def _():
        o_ref[...] = mm(wa_ref).astype(o_ref.dtype)

    @pl.when(jnp.logical_and(n >= q0, n < k0))
    def _():
        o_ref[...] = _rope_tile(mm(wb_ref), cq_ref[...], sq_ref[...]).astype(o_ref.dtype)

    @pl.when(jnp.logical_and(n >= k0, n < v0))
    def _():
        o_ref[...] = _rope_tile(mm(wb_ref), ck_ref[...], sk_ref[...]).astype(o_ref.dtype)

    @pl.when(n >= v0)
    def _():
        o_ref[...] = mm(wb_ref).astype(o_ref.dtype)


def _inproj(h, g, wa, wb, w_in_t, gbias, tabs):
    tiles_per_batch = LP // TM1
    tab_spec = pl.BlockSpec((TM1, 128), lambda m, n: (m % tiles_per_batch, 0))
    return pl.pallas_call(
        _inproj_kernel,
        grid=(R // TM1, PROJ_W // TN1),
        in_specs=[
            pl.BlockSpec((TM1, D_MODEL), lambda m, n: (m, 0)),
            pl.BlockSpec((1, D_MODEL), lambda m, n: (0, 0)),
            pl.BlockSpec((TN1, D_MODEL), lambda m, n: (jnp.minimum(n, NA1 - 1), 0)),
            pl.BlockSpec((pl.Element(TN1), pl.Element(D_MODEL)),
                         lambda m, n: (pl.multiple_of(GATE_COL0 + GROWS + jnp.maximum(n - NA1, 0) * TN1, GROWS), 0)),
            pl.BlockSpec((GROWS, D_MODEL), lambda m, n: (GATE_COL0 // GROWS, 0)),
            pl.BlockSpec((GROWS, 1), lambda m, n: (0, 0)),
            tab_spec, tab_spec, tab_spec, tab_spec,
        ],
        out_specs=[
            pl.BlockSpec((TM1, TN1), lambda m, n: (m, n)),
            pl.BlockSpec((None, GROWS, GT_PAD), lambda m, n: (m, 0, 0)),
        ],
        out_shape=[
            jax.ShapeDtypeStruct((R, PROJ_W), BF16),
            jax.ShapeDtypeStruct((R // TM1, GROWS, GT_PAD), F32),
        ],
        scratch_shapes=[pltpu.VMEM((TM1, D_MODEL), BF16)],
        compiler_params=_cparams(("parallel", "arbitrary")),
        name="inproj",
    )(h, g, wa, wb, w_in_t, gbias, *tabs)


def _rope_tables():
    pos = np.zeros((LP,), np.float64)
    pos[:SEQ] = np.arange(SEQ) + N_META
    pos[SEQ:SEQ + N_META] = np.arange(N_META)
    half = ROPE_DIM // 2
    inv = ROPE_THETA ** (-np.arange(0, ROPE_DIM, 2, dtype=np.float64) / ROPE_DIM)
    ang = pos[:, None] * inv[None, :]
    cos64 = np.ones((LP, D_QK_DIM)); sin64 = np.zeros((LP, D_QK_DIM))
    cos64[:, :half] = np.cos(ang); cos64[:, half:ROPE_DIM] = np.cos(ang)
    sin64[:, :half] = -np.sin(ang); sin64[:, half:ROPE_DIM] = np.sin(ang)
    cos = np.concatenate([cos64, cos64], axis=1)
    sin = np.concatenate([sin64, sin64], axis=1)
    qs = D_QK_DIM ** -0.5 * math.log2(math.e)
    return [jnp.asarray(t, F32) for t in (cos * qs, sin * qs, cos, sin)]


def _lane_scan(x, op, ident, reverse):
    lane = lax.broadcasted_iota(jnp.int32, x.shape, 1)
    k = 1
    while k < BLK:
        if reverse:
            sh = pltpu.roll(x, BLK - k, 1)
            x = op(x, jnp.where(lane < BLK - k, sh, ident))
        else:
            sh = pltpu.roll(x, k, 1)
            x = op(x, jnp.where(lane >= k, sh, ident))
        k *= 2
    return x


def _log_sigmoid(x):
    return jnp.minimum(x, 0.0) - jnp.log1p(jnp.exp(-jnp.abs(x)))


def _scan_kernel(g_ref, arow_ref, gcol_ref, colt_ref):
    colt_ref[...] = jnp.zeros_like(colt_ref)
    lane = lax.broadcasted_iota(jnp.int32, (8, BLK), 1)
    for d in range(2):
        reverse = d == 1
        carry_b = jnp.zeros((8, BLK), F32)
        carry_m = jnp.full((8, BLK), NEG, F32)
        logical = [META_BLK] + list(range(META_BLK))
        order = logical[::-1] if reverse else logical
        edge = 0 if reverse else BLK - 1
        for j in order:
            cols = slice(j * BLK, (j + 1) * BLK)
            li = g_ref[d * 8:(d + 1) * 8, cols]
            lf = pltpu.roll(_log_sigmoid(li), M_HEADS, 0)
            if j == META_BLK:
                valid = lane < N_META
                li = jnp.where(valid, li, NEG)
                lf = jnp.where(valid, lf, 0.0)
            bc = carry_b + _lane_scan(lf, jnp.add, 0.0, reverse)
            a = li - bc
            m = jnp.maximum(carry_m, _lane_scan(a, jnp.maximum, NEG, reverse))
            m_last = jnp.broadcast_to(m[:, edge:edge + 1], (8, BLK))
            arow_ref[d * 8:(d + 1) * 8, cols] = a
            base = d * 40
            colt_ref[j, base:base + 8, :] = a
            colt_ref[j, base + 8:base + 16, :] = m
            colt_ref[j, base + 16:base + 24, :] = bc + m
            colt_ref[j, base + 24:base + 32, :] = carry_m
            colt_ref[j, base + 32:base + 40, :] = m_last
            carry_b = jnp.broadcast_to(bc[:, edge:edge + 1], (8, BLK))
            carry_m = m_last
    for j in range(NBLK):
        gcol_ref[j * BLK:(j + 1) * BLK, :] = colt_ref[j].T


def _scans(gt):
    return pl.pallas_call(
        _scan_kernel,
        grid=(BATCH,),
        in_specs=[pl.BlockSpec((None, GROWS, LP), lambda b: (b, 0, 0))],
        out_specs=[
            pl.BlockSpec((None, 16, LP), lambda b: (b, 0, 0)),
            pl.BlockSpec((None, LP, BLK), lambda b: (b, 0, 0)),
        ],
        out_shape=[
            jax.ShapeDtypeStruct((BATCH, 16, LP), F32),
            jax.ShapeDtypeStruct((BATCH, LP, BLK), F32),
        ],
        scratch_shapes=[pltpu.VMEM((NBLK, BLK, BLK), F32)],
        compiler_params=_cparams(("parallel",)),
        name="gate_scans",
    )(gt)


TC_COLS = 256
CONV_CH = 256


def _conv3_rows(s_ref, w, lo, n):
    x = s_ref[lo:lo + n, :]
    prev = jnp.concatenate([s_ref[lo - 1:lo + 7, :], pltpu.roll(x, 1, 0)[8:]], axis=0)
    nxt = jnp.concatenate([pltpu.roll(x, n - 1, 0)[:n - 8], s_ref[lo + n - 7:lo + n + 1, :]], axis=0)
    return prev * w[0:1] + x * w[1:2] + nxt * w[2:3]


def _qkconv_kernel(x_ref, w_ref, sc_ref, o_ref, s_ref):
    xo = 8 + N_META
    s_ref[0:8, :] = jnp.zeros((8, TC_COLS), F32)
    s_ref[xo + SEQ:xo + SEQ + 8, :] = jnp.zeros((8, TC_COLS), F32)
    s_ref[8:xo, :] = x_ref[SEQ:SEQ + N_META, :].astype(F32)
    for i in range(SEQ // CONV_CH):
        s_ref[xo + i * CONV_CH:xo + (i + 1) * CONV_CH, :] = x_ref[i * CONV_CH:(i + 1) * CONV_CH, :].astype(F32)
    w = 0.5 * w_ref[...]
    sc = sc_ref[...]

    def act(yh):
        return (_silu_of_half(yh) * sc).astype(o_ref.dtype)

    for i in range(SEQ // CONV_CH):
        o_ref[i * CONV_CH:(i + 1) * CONV_CH, :] = act(_conv3_rows(s_ref, w, xo + i * CONV_CH, CONV_CH))
    o_ref[SEQ:SEQ + N_META, :] = act(_conv3_rows(s_ref, w, 8, N_META))
    o_ref[SEQ + N_META:LP, :] = jnp.zeros((LP - SEQ - N_META, TC_COLS), o_ref.dtype)


def _qkconv(proj3, w, scale):
    return pl.pallas_call(
        _qkconv_kernel,
        grid=(BATCH, 2 * M_WIDTH // TC_COLS),
        in_specs=[
            pl.BlockSpec((None, LP, TC_COLS), lambda b, c: (b, 0, C_MQK // TC_COLS + c)),
            pl.BlockSpec((3, TC_COLS), lambda b, c: (0, c)),
            pl.BlockSpec((1, TC_COLS), lambda b, c: (0, c)),
        ],
        out_specs=pl.BlockSpec((None, LP, TC_COLS), lambda b, c: (b, 0, c)),
        out_shape=jax.ShapeDtypeStruct((BATCH, LP, 2 * M_WIDTH), BF16),
        scratch_shapes=[pltpu.VMEM((SEQ + N_META + 16, TC_COLS), F32)],
        compiler_params=_cparams(("parallel", "parallel")),
        name="qk_conv",
    )(proj3, w, scale)


def _mlstm_kernel(qf, kf, vf, gcf, arf, qb, kb, vb, gcb, arb, hf_ref, hb_ref, c_ref, n_ref):
    c = pl.program_id(1)

    @pl.when(c == 0)
    def _():
        c_ref[...] = jnp.zeros_like(c_ref)
        n_ref[...] = jnp.zeros_like(n_ref)

    row = lax.broadcasted_iota(jnp.int32, (BLK, BLK), 0)
    col = lax.broadcasted_iota(jnp.int32, (BLK, BLK), 1)
    dirs = ((qf, kf, vf, gcf, arf, hf_ref), (qb, kb, vb, gcb, arb, hb_ref))
    for d, (q_ref, k_ref, v_ref, gc_ref, ar_ref, o_ref) in enumerate(dirs):
        mask = (col <= row) if d == 0 else (col >= row)
        gc = gc_ref[...]
        for h in range(M_HEADS):
            st = d * M_HEADS + h
            def colq(qi):
                j = d * 40 + qi * 8 + h
                return gc[:, j:j + 1]
            a_col, m_col, e_col, mp_col, ml_col = (colq(i) for i in range(5))
            a_row = ar_ref[d * 8 + h:d * 8 + h + 1, :]
            hs = slice(h * M_HEAD_DIM, (h + 1) * M_HEAD_DIM)
            q = q_ref[:, hs]
            k = k_ref[:, hs]
            v = v_ref[:, hs]
            s = lax.dot_general(q, k, (((1,), (1,)), ((), ())), preferred_element_type=F32)
            p = jnp.where(mask, jnp.exp(a_row - m_col), 0.0)
            sp = s * p
            a = jnp.exp(mp_col - m_col)
            ct = c_ref[st]
            nrow = n_ref[st]
            inter = jnp.dot(q, ct.astype(BF16), preferred_element_type=F32)
            num = jnp.dot(sp.astype(BF16), v, preferred_element_type=F32) + a * inter
            qn = jnp.sum(q.astype(F32) * nrow, axis=1, keepdims=True)
            den = jnp.sum(sp, axis=1, keepdims=True) + a * qn
            o_ref[:, hs] = num / jnp.maximum(jnp.abs(den), jnp.exp(-e_col))
            w = jnp.exp(a_col - ml_col)
            dec = jnp.exp(mp_col[0:1, :] - ml_col[0:1, :])
            wv = (w * v.astype(F32)).astype(BF16)
            c_ref[st] = dec * ct + lax.dot_general(
                k, wv, (((0,), (0,)), ((), ())), preferred_element_type=F32)
            n_ref[st] = dec * nrow + jnp.sum(w * k.astype(F32), axis=0, keepdims=True)


def _fwd_blk(c):
    return (c + NBLK - 1) % NBLK


def _bwd_blk(c):
    return (2 * NBLK - 2 - c) % NBLK


def _mlstm(qk3, proj3, gcol, arow):
    def specs(blk):
        return [
            pl.BlockSpec((None, BLK, M_WIDTH), lambda b, c: (b, blk(c), 0)),
            pl.BlockSpec((None, BLK, M_WIDTH), lambda b, c: (b, blk(c), 1)),
            pl.BlockSpec((None, BLK, M_WIDTH), lambda b, c: (b, blk(c), C_MV // M_WIDTH)),
            pl.BlockSpec((None, BLK, BLK), lambda b, c: (b, blk(c), 0)),
            pl.BlockSpec((None, 16, BLK), lambda b, c: (b, 0, blk(c))),
        ]
    out_sd = jax.ShapeDtypeStruct((BATCH, LP, M_WIDTH), F32)
    return pl.pallas_call(
        _mlstm_kernel,
        grid=(BATCH, NBLK),
        in_specs=specs(_fwd_blk) + specs(_bwd_blk),
        out_specs=[
            pl.BlockSpec((None, BLK, M_WIDTH), lambda b, c: (b, _fwd_blk(c), 0)),
            pl.BlockSpec((None, BLK, M_WIDTH), lambda b, c: (b, _bwd_blk(c), 0)),
        ],
        out_shape=[out_sd, out_sd],
        scratch_shapes=[
            pltpu.VMEM((2 * M_HEADS, M_HEAD_DIM, M_HEAD_DIM), F32),
            pltpu.VMEM((2 * M_HEADS, 1, M_HEAD_DIM), F32),
        ],
        compiler_params=_cparams(("parallel", "arbitrary")),
        name="mlstm",
    )(qk3, qk3, proj3, gcol, arow, qk3, qk3, proj3, gcol, arow)


RQ = 128
NGRP = LP // RQ
EXP_CH = 512
ATTN_NBUF = 2


def _attn_kernel(lam_ref, q_ref, k_ref, v_ref, g_ref, o_ref, *scratch):
    s_refs, p_refs, l_refs, m_refs = (scratch[i * ATTN_NBUF:(i + 1) * ATTN_NBUF] for i in range(4))
    lq = lam_ref[...]
    la = jnp.sum(lq[0:1] * lq[1:2], axis=1, keepdims=True)
    lb = jnp.sum(lq[2:3] * lq[3:4], axis=1, keepdims=True)
    lam = jnp.exp(la) - jnp.exp(lb) + LAM_INIT

    lane = lax.broadcasted_iota(jnp.int32, (RQ, 128), 1)
    colid = lax.broadcasted_iota(jnp.int32, (1, BLK), 1)

    def rows(g):
        return pl.ds(pl.multiple_of(g * RQ, RQ), RQ)

    def stage_scores(g, s_ref, m_ref):
        qg = q_ref[rows(g), :]
        zero = jnp.zeros_like(qg)
        q2 = jnp.concatenate([jnp.where(lane < D_QK_DIM, qg, zero), jnp.where(lane >= D_QK_DIM, qg, zero)], axis=0)
        s = lax.dot_general(q2, k_ref[...], (((1,), (1,)), ((), ())), preferred_element_type=F32)
        tail = jnp.where(colid < N_META, s[:, LP - BLK:], -jnp.inf)
        s_ref[:, :LP - BLK] = s[:, :LP - BLK]
        s_ref[:, LP - BLK:] = tail
        m = jnp.maximum(jnp.max(s[:, :LP - BLK], axis=1, keepdims=True), jnp.max(tail, axis=1, keepdims=True))
        m_ref[...] = jnp.broadcast_to(m, m_ref.shape)

    def stage_exp(s_ref, m_ref, p_ref, l_ref):
        m = m_ref[:, 0:1]
        part = jnp.zeros((2 * RQ, 128), F32)
        for c0 in range(0, LP, EXP_CH):
            c1 = min(c0 + EXP_CH, LP)
            pc = jnp.exp2(s_ref[:, c0:c1] - m)
            for j in range(0, c1 - c0, 128):
                part = part + pc[:, j:j + 128]
            p_ref[:, c0:c1] = pc.astype(BF16)
        l_ref[...] = jnp.broadcast_to(jnp.sum(part, axis=1, keepdims=True), l_ref.shape)

    def stage_pv(g, p_ref, l_ref):
        l = l_ref[:, 0:1]
        ratio = (lam * l[:RQ] / l[RQ:]).astype(BF16)
        a = p_ref[:RQ, :] - ratio * p_ref[RQ:, :]
        o = jnp.dot(a, v_ref[...], preferred_element_type=F32) / l[:RQ]
        r = lax.rsqrt(jnp.mean(o * o, axis=-1, keepdims=True) + EPS)
        o_ref[rows(g), :] = (((o * r) * g_ref[...]) * (1.0 - LAM_INIT)).astype(o_ref.dtype)

    nb = len(s_refs)
    sbuf = tuple(zip(s_refs, m_refs))
    pbuf = tuple(zip(p_refs, l_refs))

    def tick(t, static):
        if static < NGRP:
            stage_scores(t, *sbuf[static % nb])
        if 1 <= static <= NGRP:
            stage_exp(*sbuf[(static - 1) % nb], *pbuf[(static - 1) % nb])
        if 2 <= static <= NGRP + 1:
            stage_pv(t - 2, *pbuf[(static - 2) % nb])

    tick(0, 0)
    tick(1, 1)
    n_rounds = (NGRP - 2) // nb

    def body(i, carry):
        t = nb * i + 2
        for j in range(nb):
            tick(t + j, 2 + j)
        return carry

    lax.fori_loop(0, n_rounds, body, 0)
    for t in range(2 + nb * n_rounds, NGRP + 2):
        tick(t, t)


def _attention(lam4, proj3, subln_g):
    return pl.pallas_call(
        _attn_kernel,
        grid=(BATCH, D_HEADS),
        in_specs=[
            pl.BlockSpec((4, D_QK_DIM), lambda b, h: (0, 0)),
            pl.BlockSpec((None, LP, 128), lambda b, h: (b, 0, C_DQ // 128 + h)),
            pl.BlockSpec((None, LP, 128), lambda b, h: (b, 0, C_DK // 128 + h)),
            pl.BlockSpec((None, LP, 128), lambda b, h: (b, 0, C_DV // 128 + h)),
            pl.BlockSpec((1, D_V_DIM), lambda b, h: (0, 0)),
        ],
        out_specs=pl.BlockSpec((None, LP, D_V_DIM), lambda b, h: (b, 0, h)),
        out_shape=jax.ShapeDtypeStruct((BATCH, LP, D_WIDTH), BF16),
        scratch_shapes=(
            [pltpu.VMEM((2 * RQ, LP), F32)] * ATTN_NBUF
            + [pltpu.VMEM((2 * RQ, LP), BF16)] * ATTN_NBUF
            + [pltpu.VMEM((2 * RQ, 128), F32)] * ATTN_NBUF
            + [pltpu.VMEM((2 * RQ, 128), F32)] * ATTN_NBUF
        ),
        compiler_params=_cparams(("parallel", "parallel")),
        name="diff_attn",
    )(lam4, proj3, proj3, proj3, subln_g)


TM4 = 1056
TN4 = 512
LN_CH = 96


def _merge_kernel(hf_ref, hb_ref, mo_ref, ng_ref, bo_ref, wa_ref, wb_ref, gm_ref, gd_ref, o_ref, a_ref):
    n = pl.program_id(1)

    @pl.when(n == 0)
    def _():
        def body(i, carry):
            rows = pl.ds(pl.multiple_of(i * LN_CH, LN_CH), LN_CH)
            for h in range(M_HEADS):
                hs = slice(h * M_HEAD_DIM, (h + 1) * M_HEAD_DIM)
                x = hf_ref[rows, hs] + hb_ref[rows, hs]
                mu = jnp.mean(x, axis=-1, keepdims=True)
                xc = x - mu
                var = jnp.mean(xc * xc, axis=-1, keepdims=True)
                hn = (xc * lax.rsqrt(var + EPS)) * ng_ref[:, hs]
                a_ref[rows, hs] = (_twice_sigmoid_of_half(mo_ref[rows, hs].astype(F32)) * hn).astype(BF16)
            return carry
        lax.fori_loop(0, TM4 // LN_CH, body, 0)

    ya = jnp.dot(a_ref[...], wa_ref[...], preferred_element_type=F32)
    yb = jnp.dot(bo_ref[...], wb_ref[...], preferred_element_type=F32)
    o_ref[...] = (_twice_sigmoid_of_half(gm_ref[...].astype(F32)) * ya
                  + _twice_sigmoid_of_half(gd_ref[...].astype(F32)) * yb).astype(o_ref.dtype)


def _merge(hf, hb, proj, norm_g, b_out, wa, wb):
    return pl.pallas_call(
        _merge_kernel,
        grid=(R // TM4, D_MODEL // TN4),
        in_specs=[
            pl.BlockSpec((TM4, M_WIDTH), lambda m, n: (m, 0)),
            pl.BlockSpec((TM4, M_WIDTH), lambda m, n: (m, 0)),
            pl.BlockSpec((TM4, M_WIDTH), lambda m, n: (m, C_MO // M_WIDTH)),
            pl.BlockSpec((1, M_WIDTH), lambda m, n: (0, 0)),
            pl.BlockSpec((TM4, D_WIDTH), lambda m, n: (m, 0)),
            pl.BlockSpec((M_WIDTH, TN4), lambda m, n: (0, n)),
            pl.BlockSpec((D_WIDTH, TN4), lambda m, n: (0, n)),
            pl.BlockSpec((TM4, TN4), lambda m, n: (m, C_GM // TN4 + n)),
            pl.BlockSpec((TM4, TN4), lambda m, n: (m, C_GD // TN4 + n)),
        ],
        out_specs=pl.BlockSpec((TM4, TN4), lambda m, n: (m, n)),
        out_shape=jax.ShapeDtypeStruct((R, D_MODEL), BF16),
        scratch_shapes=[pltpu.VMEM((TM4, M_WIDTH), BF16)],
        compiler_params=_cparams(("parallel", "arbitrary")),
        name="merge",
    )(hf, hb, proj, norm_g, b_out, wa, wb, proj, proj)


TM5 = 384


def _outproj_kernel(h0_ref, mg_ref, w_ref, g2_ref, h1_ref, u2_ref):
    h1 = h0_ref[...] + jnp.dot(mg_ref[...], w_ref[...], preferred_element_type=F32)
    h1_ref[...] = h1
    r = lax.rsqrt(jnp.mean(h1 * h1, axis=-1, keepdims=True) + EPS)
    u2_ref[...] = ((h1 * r) * g2_ref[...]).astype(BF16)


def _outproj(h0, merged, w, g2):
    return pl.pallas_call(
        _outproj_kernel,
        grid=(R // TM5,),
        in_specs=[
            pl.BlockSpec((TM5, D_MODEL), lambda m: (m, 0)),
            pl.BlockSpec((TM5, D_MODEL), lambda m: (m, 0)),
            pl.BlockSpec((D_MODEL, D_MODEL), lambda m: (0, 0)),
            pl.BlockSpec((1, D_MODEL), lambda m: (0, 0)),
        ],
        out_specs=[
            pl.BlockSpec((TM5, D_MODEL), lambda m: (m, 0)),
            pl.BlockSpec((TM5, D_MODEL), lambda m: (m, 0)),
        ],
        out_shape=[
            jax.ShapeDtypeStruct((R, D_MODEL), F32),
            jax.ShapeDtypeStruct((R, D_MODEL), BF16),
        ],
        compiler_params=_cparams(("parallel",)),
        name="outproj",
    )(h0, merged, w, g2)


TMU = 1024
TNU = 512
HALO = 16
NNU = FFN_DIM // TNU


def _ffn_up_kernel(x_ref, xp_ref, xn_ref, wg_ref, wv_ref, cg_ref, cv_ref, o_ref, xs_ref, sg_ref, sv_ref):
    @pl.when(pl.program_id(2) == 0)
    def _():
        last_tile = pl.program_id(1) == SEQ // TMU - 1
        xs_ref[0:HALO, :] = xp_ref[...]
        xs_ref[HALO:HALO + TMU, :] = x_ref[...]
        xn = xn_ref[...]
        xs_ref[HALO + TMU:, :] = jnp.where(last_tile, jnp.zeros_like(xn), xn)

    def branch(w_ref, c, s_ref):
        s_ref[...] = jnp.dot(xs_ref[...], w_ref[...].astype(BF16), preferred_element_type=F32)
        return _conv3_rows(s_ref, c, HALO, TMU)

    gate_half = branch(wg_ref, 0.5 * cg_ref[...], sg_ref)
    val = branch(wv_ref, cv_ref[...], sv_ref)
    o_ref[...] = (_silu_of_half(gate_half) * val).astype(o_ref.dtype)


def _ffn_up(u2_3, w_up, conv_w):
    hb = TMU // HALO
    meta_hblk = SEQ // HALO
    return pl.pallas_call(
        _ffn_up_kernel,
        grid=(BATCH, SEQ // TMU, NNU),
        in_specs=[
            pl.BlockSpec((None, TMU, D_MODEL), lambda b, i, n: (b, i, 0)),
            pl.BlockSpec((None, HALO, D_MODEL), lambda b, i, n: (b, jnp.where(i == 0, meta_hblk, i * hb - 1), 0)),
            pl.BlockSpec((None, HALO, D_MODEL), lambda b, i, n: (b, (i + 1) * hb, 0)),
            pl.BlockSpec((D_MODEL, TNU), lambda b, i, n: (0, n)),
            pl.BlockSpec((D_MODEL, TNU), lambda b, i, n: (0, NNU + n)),
            pl.BlockSpec((3, TNU), lambda b, i, n: (0, n)),
            pl.BlockSpec((3, TNU), lambda b, i, n: (0, NNU + n)),
        ],
        out_specs=pl.BlockSpec((None, TMU, TNU), lambda b, i, n: (b, i, n)),
        out_shape=jax.ShapeDtypeStruct((BATCH, SEQ, FFN_DIM), BF16),
        scratch_shapes=[
            pltpu.VMEM((TMU + 2 * HALO, D_MODEL), BF16),
            pltpu.VMEM((TMU + 2 * HALO, TNU), F32),
            pltpu.VMEM((TMU + 2 * HALO, TNU), F32),
        ],
        compiler_params=_cparams(("parallel", "parallel", "arbitrary")),
        name="ffn_up",
    )(u2_3, u2_3, u2_3, w_up, w_up, conv_w, conv_w)


TM7 = 1024
TK7 = 512
NK7 = FFN_DIM // TK7


def _down_kernel(a_ref, wd_ref, h1_ref, gf_ref, o_ref, acc_ref):
    k = pl.program_id(2)

    def part():
        return jnp.dot(a_ref[...], wd_ref[...], preferred_element_type=F32)

    @pl.when(k == 0)
    def _():
        acc_ref[...] = part()

    @pl.when(jnp.logical_and(k > 0, k < NK7 - 1))
    def _():
        acc_ref[...] += part()

    @pl.when(k == NK7 - 1)
    def _():
        h2 = h1_ref[...] + (acc_ref[...] + part())
        r = lax.rsqrt(jnp.mean(h2 * h2, axis=-1, keepdims=True) + EPS)
        o_ref[...] = (h2 * r) * gf_ref[...]


def _ffn_down(act, w_down, h1_3, gf):
    return pl.pallas_call(
        _down_kernel,
        grid=(BATCH, SEQ // TM7, NK7),
        in_specs=[
            pl.BlockSpec((None, TM7, TK7), lambda b, i, k: (b, i, k)),
            pl.BlockSpec((TK7, D_MODEL), lambda b, i, k: (k, 0)),
            pl.BlockSpec((None, TM7, D_MODEL), lambda b, i, k: (b, i, 0)),
            pl.BlockSpec((1, D_MODEL), lambda b, i, k: (0, 0)),
        ],
        out_specs=pl.BlockSpec((None, TM7, D_MODEL), lambda b, i, k: (b, i, 0)),
        out_shape=jax.ShapeDtypeStruct((BATCH, SEQ, D_MODEL), F32),
        scratch_shapes=[pltpu.VMEM((TM7, D_MODEL), F32)],
        compiler_params=_cparams(("parallel", "parallel", "arbitrary")),
        name="ffn_down",
    )(act, w_down, h1_3, gf)


def kernel(x, meta_tokens, norm1_g, w_in, mlstm_conv_w, mlstm_gate_bias, mlstm_norm_g, lambda_q1, lambda_k1,
           lambda_q2, lambda_k2, diff_subln_g, w_branch_m, w_branch_d, w_out, norm2_g, w_up, ffn_conv_w,
           w_down, norm_f_g):
    assert x.shape == (BATCH, SEQ, D_MODEL) and w_in.shape[0] == 1
    meta = jnp.broadcast_to(meta_tokens[None].astype(x.dtype), (BATCH, N_META, D_MODEL))
    pad = jnp.zeros((BATCH, LP - SEQ - N_META, D_MODEL), x.dtype)
    h0 = jnp.concatenate([x, meta, pad], axis=1)
    h0f = h0.reshape(R, D_MODEL)

    wit = w_in[0].T
    halved = np.ones((wit.shape[0], 1), np.float32)
    for c0, width in ((C_MO, M_WIDTH), (C_GM + GROWS, D_MODEL), (C_GD + GROWS, D_MODEL)):
        halved[c0:c0 + width] = 0.5
    w_pa = w_pb = (wit * halved).astype(BF16)
    gbias = mlstm_gate_bias[0].reshape(GROWS, 1).astype(F32)
    g1 = norm1_g[0].reshape(1, D_MODEL)

    proj, gt_tiles = _inproj(h0f, g1, w_pa, w_pb, wit, gbias, _rope_tables())
    proj3 = proj.reshape(BATCH, LP, PROJ_W)

    gt = gt_tiles[:, :, :TM1].reshape(BATCH, LP // TM1, GROWS, TM1).transpose(0, 2, 1, 3).reshape(BATCH, GROWS, LP)
    arow, gcol = _scans(gt)

    kscale = jnp.concatenate([jnp.ones((1, M_WIDTH), F32), jnp.full((1, M_WIDTH), M_HEAD_DIM ** -0.5, F32)], axis=1)
    qk3 = _qkconv(proj3, mlstm_conv_w[0], kscale)
    hf, hb = _mlstm(qk3, proj3, gcol, arow)

    lam4 = jnp.stack([lambda_q1[0], lambda_k1[0], lambda_q2[0], lambda_k2[0]]).astype(F32)
    b_out = _attention(lam4, proj3, diff_subln_g[0].reshape(1, D_V_DIM))

    merged = _merge(hf.reshape(R, M_WIDTH), hb.reshape(R, M_WIDTH), proj,
                    (0.5 * mlstm_norm_g[0]).reshape(1, M_WIDTH), b_out.reshape(R, D_WIDTH),
                    (0.5 * w_branch_m[0]).astype(BF16), (0.5 * w_branch_d[0]).astype(BF16))
    h1, u2 = _outproj(h0f, merged, w_out[0].astype(BF16), norm2_g[0].reshape(1, D_MODEL))

    act = _ffn_up(u2.reshape(BATCH, LP, D_MODEL), w_up[0], ffn_conv_w[0])
    return _ffn_down(act, w_down[0].astype(BF16), h1.reshape(BATCH, LP, D_MODEL), norm_f_g.reshape(1, D_MODEL))
```

```python
import math

import numpy as np
import jax
import jax.numpy as jnp
from jax import lax
from jax.experimental import pallas as pl
from jax.experimental.pallas import tpu as pltpu

F32 = jnp.float32
BF16 = jnp.bfloat16

D_MODEL = 2048
BATCH = 2
SEQ = 4096
N_META = 16
EPS = 1e-6
M_HEADS = 4
M_WIDTH = 1024
M_HEAD_DIM = 256
D_HEADS = 8
D_V_DIM = 128
D_QK_DIM = 64
D_WIDTH = 1024
ROPE_THETA = 500000.0
ROPE_DIM = 16
FFN_DIM = 5632
LAM_INIT = 0.8 - 0.6 * math.exp(-0.3 * 0)

BLK = 128
LP = SEQ + BLK
NBLK = LP // BLK
META_BLK = SEQ // BLK
R = BATCH * LP
NEG = -1e30

C_MQK, C_MV, C_MO, C_DQ, C_DK, C_DV, C_GM, C_GD = 0, 2048, 3072, 4096, 5120, 6144, 7168, 9216
PROJ_W = 11264

V7X_VMEM_BYTES = 64 * 1024 * 1024
VMEM_LIMIT = V7X_VMEM_BYTES - 8 * 1024 * 1024


def _cparams(sem):
    return pltpu.CompilerParams(dimension_semantics=sem, vmem_limit_bytes=VMEM_LIMIT)


def _twice_sigmoid_of_half(xh):
    return jnp.tanh(xh) + 1.0


def _silu_of_half(yh):
    return yh * (jnp.tanh(yh) + 1.0)


TM1 = 1056
TN1 = 1024
NORM_CH = 96


def _rmsnorm_rows_to(h_ref, g_ref, u_ref, tm):
    def body(i, carry):
        rows = pl.ds(pl.multiple_of(i * NORM_CH, NORM_CH), NORM_CH)
        x = h_ref[rows, :]
        r = lax.rsqrt(jnp.mean(x * x, axis=-1, keepdims=True) + EPS)
        u_ref[rows, :] = ((x * r) * g_ref[...]).astype(BF16)
        return carry
    lax.fori_loop(0, tm // NORM_CH, body, 0)


def _rope_tile(acc, c, s):
    lane = lax.broadcasted_iota(jnp.int32, c.shape, 1)
    first = (lane % D_QK_DIM) < (ROPE_DIM // 2)
    outs = []
    for g in range(acc.shape[1] // 128):
        y = acc[:, g * 128:(g + 1) * 128]
        rot = jnp.where(first, pltpu.roll(y, 128 - ROPE_DIM // 2, 1), pltpu.roll(y, ROPE_DIM // 2, 1))
        outs.append(y * c + rot * s)
    return jnp.concatenate(outs, axis=1)


NA1 = C_DQ // TN1
GROWS = 16
GATE_COL0 = 4 * M_WIDTH
GT_PAD = -(-TM1 // 128) * 128


def _inproj_kernel(h_ref, g_ref, wa_ref, wb_ref, wg_ref, gb_ref, cq_ref, sq_ref, ck_ref, sk_ref,
                   o_ref, gt_ref, u_ref):
    n = pl.program_id(1)

    @pl.when(n == 0)
    def _():
        _rmsnorm_rows_to(h_ref, g_ref, u_ref, TM1)
        gt = lax.dot_general(wg_ref[...].astype(BF16), u_ref[...], (((1,), (1,)), ((), ())),
                             preferred_element_type=F32)
        gt_ref[:, :TM1] = gt + gb_ref[...]
        gt_ref[:, TM1:] = jnp.zeros((GROWS, GT_PAD - TM1), F32)

    def mm(wt_ref):
        return lax.dot_general(u_ref[...], wt_ref[...], (((1,), (1,)), ((), ())), preferred_element_type=F32)

    q0, k0, v0 = C_DQ // TN1, C_DK // TN1, C_DV // TN1

    @pl.when(n < NA1)
    def _():
        o_ref[...] = mm(wa_ref).astype(o_ref.dtype)

    @pl.when(jnp.logical_and(n >= q0, n < k0))
    def _():
        o_ref[...] = _rope_tile(mm(wb_ref), cq_ref[...], sq_ref[...]).astype(o_ref.dtype)

    @pl.when(jnp.logical_and(n >= k0, n < v0))
    def _():
        o_ref[...] = _rope_tile(mm(wb_ref), ck_ref[...], sk_ref[...]).astype(o_ref.dtype)

    @pl.when(n >= v0)
    def _():
        o_ref[...] = mm(wb_ref).astype(o_ref.dtype)


def _inproj(h, g, wa, wb, w_in_t, gbias, tabs):
    tiles_per_batch = LP // TM1
    tab_spec = pl.BlockSpec((TM1, 128), lambda m, n: (m % tiles_per_batch, 0))
    return pl.pallas_call(
        _inproj_kernel,
        grid=(R // TM1, PROJ_W // TN1),
        in_specs=[
            pl.BlockSpec((TM1, D_MODEL), lambda m, n: (m, 0)),
            pl.BlockSpec((1, D_MODEL), lambda m, n: (0, 0)),
            pl.BlockSpec((TN1, D_MODEL), lambda m, n: (jnp.minimum(n, NA1 - 1), 0)),
            pl.BlockSpec((pl.Element(TN1), pl.Element(D_MODEL)),
                         lambda m, n: (pl.multiple_of(GATE_COL0 + GROWS + jnp.maximum(n - NA1, 0) * TN1, GROWS), 0)),
            pl.BlockSpec((GROWS, D_MODEL), lambda m, n: (GATE_COL0 // GROWS, 0)),
            pl.BlockSpec((GROWS, 1), lambda m, n: (0, 0)),
            tab_spec, tab_spec, tab_spec, tab_spec,
        ],
        out_specs=[
            pl.BlockSpec((TM1, TN1), lambda m, n: (m, n)),
            pl.BlockSpec((None, GROWS, GT_PAD), lambda m, n: (m, 0, 0)),
        ],
        out_shape=[
            jax.ShapeDtypeStruct((R, PROJ_W), BF16),
            jax.ShapeDtypeStruct((R // TM1, GROWS, GT_PAD), F32),
        ],
        scratch_shapes=[pltpu.VMEM((TM1, D_MODEL), BF16)],
        compiler_params=_cparams(("parallel", "arbitrary")),
        name="inproj",
    )(h, g, wa, wb, w_in_t, gbias, *tabs)


def _rope_tables():
    pos = np.zeros((LP,), np.float64)
    pos[:SEQ] = np.arange(SEQ) + N_META
    pos[SEQ:SEQ + N_META] = np.arange(N_META)
    half = ROPE_DIM // 2
    inv = ROPE_THETA ** (-np.arange(0, ROPE_DIM, 2, dtype=np.float64) / ROPE_DIM)
    ang = pos[:, None] * inv[None, :]
    cos64 = np.ones((LP, D_QK_DIM)); sin64 = np.zeros((LP, D_QK_DIM))
    cos64[:, :half] = np.cos(ang); cos64[:, half:ROPE_DIM] = np.cos(ang)
    sin64[:, :half] = -np.sin(ang); sin64[:, half:ROPE_DIM] = np.sin(ang)
    cos = np.concatenate([cos64, cos64], axis=1)
    sin = np.concatenate([sin64, sin64], axis=1)
    qs = D_QK_DIM ** -0.5 * math.log2(math.e)
    return [jnp.asarray(t, F32) for t in (cos * qs, sin * qs, cos, sin)]


def _lane_scan(x, op, ident, reverse):
    lane = lax.broadcasted_iota(jnp.int32, x.shape, 1)
    k = 1
    while k < BLK:
        if reverse:
            sh = pltpu.roll(x, BLK - k, 1)
            x = op(x, jnp.where(lane < BLK - k, sh, ident))
        else:
            sh = pltpu.roll(x, k, 1)
            x = op(x, jnp.where(lane >= k, sh, ident))
        k *= 2
    return x


def _log_sigmoid(x):
    return jnp.minimum(x, 0.0) - jnp.log1p(jnp.exp(-jnp.abs(x)))


def _scan_kernel(g_ref, arow_ref, gcol_ref, colt_ref):
    colt_ref[...] = jnp.zeros_like(colt_ref)
    lane = lax.broadcasted_iota(jnp.int32, (8, BLK), 1)
    for d in range(2):
        reverse = d == 1
        carry_b = jnp.zeros((8, BLK), F32)
        carry_m = jnp.full((8, BLK), NEG, F32)
        logical = [META_BLK] + list(range(META_BLK))
        order = logical[::-1] if reverse else logical
        edge = 0 if reverse else BLK - 1
        for j in order:
            cols = slice(j * BLK, (j + 1) * BLK)
            li = g_ref[d * 8:(d + 1) * 8, cols]
            lf = pltpu.roll(_log_sigmoid(li), M_HEADS, 0)
            if j == META_BLK:
                valid = lane < N_META
                li = jnp.where(valid, li, NEG)
                lf = jnp.where(valid, lf, 0.0)
            bc = carry_b + _lane_scan(lf, jnp.add, 0.0, reverse)
            a = li - bc
            m = jnp.maximum(carry_m, _lane_scan(a, jnp.maximum, NEG, reverse))
            m_last = jnp.broadcast_to(m[:, edge:edge + 1], (8, BLK))
            arow_ref[d * 8:(d + 1) * 8, cols] = a
            base = d * 40
            colt_ref[j, base:base + 8, :] = a
            colt_ref[j, base + 8:base + 16, :] = m
            colt_ref[j, base + 16:base + 24, :] = bc + m
            colt_ref[j, base + 24:base + 32, :] = carry_m
            colt_ref[j, base + 32:base + 40, :] = m_last
            carry_b = jnp.broadcast_to(bc[:, edge:edge + 1], (8, BLK))
            carry_m = m_last
    for j in range(NBLK):
        gcol_ref[j * BLK:(j + 1) * BLK, :] = colt_ref[j].T


def _scans(gt):
    return pl.pallas_call(
        _scan_kernel,
        grid=(BATCH,),
        in_specs=[pl.BlockSpec((None, GROWS, LP), lambda b: (b, 0, 0))],
        out_specs=[
            pl.BlockSpec((None, 16, LP), lambda b: (b, 0, 0)),
            pl.BlockSpec((None, LP, BLK), lambda b: (b, 0, 0)),
        ],
        out_shape=[
            jax.ShapeDtypeStruct((BATCH, 16, LP), F32),
            jax.ShapeDtypeStruct((BATCH, LP, BLK), F32),
        ],
        scratch_shapes=[pltpu.VMEM((NBLK, BLK, BLK), F32)],
        compiler_params=_cparams(("parallel",)),
        name="gate_scans",
    )(gt)


TC_COLS = 256
CONV_CH = 256


def _conv3_rows(s_ref, w, lo, n):
    x = s_ref[lo:lo + n, :]
    prev = jnp.concatenate([s_ref[lo - 1:lo + 7, :], pltpu.roll(x, 1, 0)[8:]], axis=0)
    nxt = jnp.concatenate([pltpu.roll(x, n - 1, 0)[:n - 8], s_ref[lo + n - 7:lo + n + 1, :]], axis=0)
    return prev * w[0:1] + x * w[1:2] + nxt * w[2:3]


def _qkconv_kernel(x_ref, w_ref, sc_ref, o_ref, s_ref):
    xo = 8 + N_META
    s_ref[0:8, :] = jnp.zeros((8, TC_COLS), F32)
    s_ref[xo + SEQ:xo + SEQ + 8, :] = jnp.zeros((8, TC_COLS), F32)
    s_ref[8:xo, :] = x_ref[SEQ:SEQ + N_META, :].astype(F32)
    for i in range(SEQ // CONV_CH):
        s_ref[xo + i * CONV_CH:xo + (i + 1) * CONV_CH, :] = x_ref[i * CONV_CH:(i + 1) * CONV_CH, :].astype(F32)
    w = 0.5 * w_ref[...]
    sc = sc_ref[...]

    def act(yh):
        return (_silu_of_half(yh) * sc).astype(o_ref.dtype)

    for i in range(SEQ // CONV_CH):
        o_ref[i * CONV_CH:(i + 1) * CONV_CH, :] = act(_conv3_rows(s_ref, w, xo + i * CONV_CH, CONV_CH))
    o_ref[SEQ:SEQ + N_META, :] = act(_conv3_rows(s_ref, w, 8, N_META))
    o_ref[SEQ + N_META:LP, :] = jnp.zeros((LP - SEQ - N_META, TC_COLS), o_ref.dtype)


def _qkconv(proj3, w, scale):
    return pl.pallas_call(
        _qkconv_kernel,
        grid=(BATCH, 2 * M_WIDTH // TC_COLS),
        in_specs=[
            pl.BlockSpec((None, LP, TC_COLS), lambda b, c: (b, 0, C_MQK // TC_COLS + c)),
            pl.BlockSpec((3, TC_COLS), lambda b, c: (0, c)),
            pl.BlockSpec((1, TC_COLS), lambda b, c: (0, c)),
        ],
        out_specs=pl.BlockSpec((None, LP, TC_COLS), lambda b, c: (b, 0, c)),
        out_shape=jax.ShapeDtypeStruct((BATCH, LP, 2 * M_WIDTH), BF16),
        scratch_shapes=[pltpu.VMEM((SEQ + N_META + 16, TC_COLS), F32)],
        compiler_params=_cparams(("parallel", "parallel")),
        name="qk_conv",
    )(proj3, w, scale)


def _mlstm_kernel(qf, kf, vf, gcf, arf, qb, kb, vb, gcb, arb, hf_ref, hb_ref, c_ref):
    c = pl.program_id(1)

    @pl.when(c == 0)
    def _():
        c_ref[...] = jnp.zeros_like(c_ref)

    row = lax.broadcasted_iota(jnp.int32, (BLK, BLK), 0)
    col = lax.broadcasted_iota(jnp.int32, (BLK, BLK), 1)
    ones_b = jnp.ones((BLK, 128), BF16)
    ones_f = jnp.ones((BLK, 128), F32)
    dirs = ((qf, kf, vf, gcf, arf, hf_ref), (qb, kb, vb, gcb, arb, hb_ref))
    for d, (q_ref, k_ref, v_ref, gc_ref, ar_ref, o_ref) in enumerate(dirs):
        mask = (col <= row) if d == 0 else (col >= row)
        gc = gc_ref[...]
        for h in range(M_HEADS):
            st = d * M_HEADS + h
            def colq(qi):
                j = d * 40 + qi * 8 + h
                return gc[:, j:j + 1]
            a_col, m_col, e_col, mp_col, ml_col = (colq(i) for i in range(5))
            a_row = ar_ref[d * 8 + h:d * 8 + h + 1, :]
            hs = slice(h * M_HEAD_DIM, (h + 1) * M_HEAD_DIM)
            q = q_ref[:, hs]
            k = k_ref[:, hs]
            v = v_ref[:, hs]
            s = lax.dot_general(q, k, (((1,), (1,)), ((), ())), preferred_element_type=F32)
            p = jnp.where(mask, jnp.exp(a_row - m_col), 0.0)
            sp = (s * p).astype(BF16)
            a = jnp.exp(mp_col - m_col)
            cn = c_ref[st]
            tot = (jnp.dot(sp, jnp.concatenate([v, ones_b], axis=1), preferred_element_type=F32)
                   + a * jnp.dot(q, cn.astype(BF16), preferred_element_type=F32))
            r = 1.0 / jnp.maximum(jnp.abs(tot[:, M_HEAD_DIM:]), jnp.exp(-e_col))
            o_ref[:, hs] = tot[:, :M_HEAD_DIM] * jnp.concatenate([r, r], axis=1)
            w = jnp.exp(a_col - ml_col)
            dec = jnp.exp(mp_col[0:1, :] - ml_col[0:1, :])
            wvn = (w * jnp.concatenate([v.astype(F32), ones_f], axis=1)).astype(BF16)
            c_ref[st] = dec * cn + lax.dot_general(
                k, wvn, (((0,), (0,)), ((), ())), preferred_element_type=F32)


def _fwd_blk(c):
    return (c + NBLK - 1) % NBLK


def _bwd_blk(c):
    return (2 * NBLK - 2 - c) % NBLK


def _mlstm(qk3, proj3, gcol, arow):
    def specs(blk):
        return [
            pl.BlockSpec((None, BLK, M_WIDTH), lambda b, c: (b, blk(c), 0)),
            pl.BlockSpec((None, BLK, M_WIDTH), lambda b, c: (b, blk(c), 1)),
            pl.BlockSpec((None, BLK, M_WIDTH), lambda b, c: (b, blk(c), C_MV // M_WIDTH)),
            pl.BlockSpec((None, BLK, BLK), lambda b, c: (b, blk(c), 0)),
            pl.BlockSpec((None, 16, BLK), lambda b, c: (b, 0, blk(c))),
        ]
    out_sd = jax.ShapeDtypeStruct((BATCH, LP, M_WIDTH), F32)
    return pl.pallas_call(
        _mlstm_kernel,
        grid=(BATCH, NBLK),
        in_specs=specs(_fwd_blk) + specs(_bwd_blk),
        out_specs=[
            pl.BlockSpec((None, BLK, M_WIDTH), lambda b, c: (b, _fwd_blk(c), 0)),
            pl.BlockSpec((None, BLK, M_WIDTH), lambda b, c: (b, _bwd_blk(c), 0)),
        ],
        out_shape=[out_sd, out_sd],
        scratch_shapes=[pltpu.VMEM((2 * M_HEADS, M_HEAD_DIM, M_HEAD_DIM + 128), F32)],
        compiler_params=_cparams(("parallel", "arbitrary")),
        name="mlstm",
    )(qk3, qk3, proj3, gcol, arow, qk3, qk3, proj3, gcol, arow)


RQ = 128
NGRP = LP // RQ
EXP_CH = 512
ATTN_NBUF = 2


def _attn_kernel(lam_ref, q_ref, k_ref, v_ref, g_ref, o_ref, *scratch):
    s_refs, p_refs, l_refs, m_refs = (scratch[i * ATTN_NBUF:(i + 1) * ATTN_NBUF] for i in range(4))
    lq = lam_ref[...]
    la = jnp.sum(lq[0:1] * lq[1:2], axis=1, keepdims=True)
    lb = jnp.sum(lq[2:3] * lq[3:4], axis=1, keepdims=True)
    lam = jnp.exp(la) - jnp.exp(lb) + LAM_INIT

    lane = lax.broadcasted_iota(jnp.int32, (RQ, 128), 1)
    colid = lax.broadcasted_iota(jnp.int32, (1, BLK), 1)

    def rows(g):
        return pl.ds(pl.multiple_of(g * RQ, RQ), RQ)

    def stage_scores(g, s_ref, m_ref):
        qg = q_ref[rows(g), :]
        zero = jnp.zeros_like(qg)
        q2 = jnp.concatenate([jnp.where(lane < D_QK_DIM, qg, zero), jnp.where(lane >= D_QK_DIM, qg, zero)], axis=0)
        s = lax.dot_general(q2, k_ref[...], (((1,), (1,)), ((), ())), preferred_element_type=F32)
        tail = jnp.where(colid < N_META, s[:, LP - BLK:], -jnp.inf)
        s_ref[:, :LP - BLK] = s[:, :LP - BLK]
        s_ref[:, LP - BLK:] = tail
        m = jnp.maximum(jnp.max(s[:, :LP - BLK], axis=1, keepdims=True), jnp.max(tail, axis=1, keepdims=True))
        m_ref[...] = jnp.broadcast_to(m, m_ref.shape)

    def stage_exp(s_ref, m_ref, p_ref, l_ref):
        m = m_ref[:, 0:1]
        part = jnp.zeros((2 * RQ, 128), F32)
        for c0 in range(0, LP, EXP_CH):
            c1 = min(c0 + EXP_CH, LP)
            pc = jnp.exp2(s_ref[:, c0:c1] - m)
            for j in range(0, c1 - c0, 128):
                part = part + pc[:, j:j + 128]
            p_ref[:, c0:c1] = pc.astype(BF16)
        l_ref[...] = jnp.broadcast_to(jnp.sum(part, axis=1, keepdims=True), l_ref.shape)

    def stage_pv(g, p_ref, l_ref):
        l = l_ref[:, 0:1]
        ratio = (lam * l[:RQ] / l[RQ:]).astype(BF16)
        a = p_ref[:RQ, :] - ratio * p_ref[RQ:, :]
        o = jnp.dot(a, v_ref[...], preferred_element_type=F32) / l[:RQ]
        r = lax.rsqrt(jnp.mean(o * o, axis=-1, keepdims=True) + EPS)
        o_ref[rows(g), :] = (((o * r) * g_ref[...]) * (1.0 - LAM_INIT)).astype(o_ref.dtype)

    nb = len(s_refs)
    sbuf = tuple(zip(s_refs, m_refs))
    pbuf = tuple(zip(p_refs, l_refs))

    def tick(t, static):
        if static < NGRP:
            stage_scores(t, *sbuf[static % nb])
        if 1 <= static <= NGRP:
            stage_exp(*sbuf[(static - 1) % nb], *pbuf[(static - 1) % nb])
        if 2 <= static <= NGRP + 1:
            stage_pv(t - 2, *pbuf[(static - 2) % nb])

    tick(0, 0)
    tick(1, 1)
    n_rounds = (NGRP - 2) // nb

    def body(i, carry):
        t = nb * i + 2
        for j in range(nb):
            tick(t + j, 2 + j)
        return carry

    lax.fori_loop(0, n_rounds, body, 0)
    for t in range(2 + nb * n_rounds, NGRP + 2):
        tick(t, t)


def _attention(lam4, proj3, subln_g):
    return pl.pallas_call(
        _attn_kernel,
        grid=(BATCH, D_HEADS),
        in_specs=[
            pl.BlockSpec((4, D_QK_DIM), lambda b, h: (0, 0)),
            pl.BlockSpec((None, LP, 128), lambda b, h: (b, 0, C_DQ // 128 + h)),
            pl.BlockSpec((None, LP, 128), lambda b, h: (b, 0, C_DK // 128 + h)),
            pl.BlockSpec((None, LP, 128), lambda b, h: (b, 0, C_DV // 128 + h)),
            pl.BlockSpec((1, D_V_DIM), lambda b, h: (0, 0)),
        ],
        out_specs=pl.BlockSpec((None, LP, D_V_DIM), lambda b, h: (b, 0, h)),
        out_shape=jax.ShapeDtypeStruct((BATCH, LP, D_WIDTH), BF16),
        scratch_shapes=(
            [pltpu.VMEM((2 * RQ, LP), F32)] * ATTN_NBUF
            + [pltpu.VMEM((2 * RQ, LP), BF16)] * ATTN_NBUF
            + [pltpu.VMEM((2 * RQ, 128), F32)] * ATTN_NBUF
            + [pltpu.VMEM((2 * RQ, 128), F32)] * ATTN_NBUF
        ),
        compiler_params=_cparams(("parallel", "parallel")),
        name="diff_attn",
    )(lam4, proj3, proj3, proj3, subln_g)


TM4 = 1056
TN4 = 512
LN_CH = 96


def _merge_kernel(hf_ref, hb_ref, mo_ref, ng_ref, bo_ref, wa_ref, wb_ref, gm_ref, gd_ref, o_ref, a_ref):
    n = pl.program_id(1)

    @pl.when(n == 0)
    def _():
        def body(i, carry):
            rows = pl.ds(pl.multiple_of(i * LN_CH, LN_CH), LN_CH)
            for h in range(M_HEADS):
                hs = slice(h * M_HEAD_DIM, (h + 1) * M_HEAD_DIM)
                x = hf_ref[rows, hs] + hb_ref[rows, hs]
                mu = jnp.mean(x, axis=-1, keepdims=True)
                xc = x - mu
                var = jnp.mean(xc * xc, axis=-1, keepdims=True)
                hn = (xc * lax.rsqrt(var + EPS)) * ng_ref[:, hs]
                a_ref[rows, hs] = (_twice_sigmoid_of_half(mo_ref[rows, hs].astype(F32)) * hn).astype(BF16)
            return carry
        lax.fori_loop(0, TM4 // LN_CH, body, 0)

    ya = jnp.dot(a_ref[...], wa_ref[...], preferred_element_type=F32)
    yb = jnp.dot(bo_ref[...], wb_ref[...], preferred_element_type=F32)
    o_ref[...] = (_twice_sigmoid_of_half(gm_ref[...].astype(F32)) * ya
                  + _twice_sigmoid_of_half(gd_ref[...].astype(F32)) * yb).astype(o_ref.dtype)


def _merge(hf, hb, proj, norm_g, b_out, wa, wb):
    return pl.pallas_call(
        _merge_kernel,
        grid=(R // TM4, D_MODEL // TN4),
        in_specs=[
            pl.BlockSpec((TM4, M_WIDTH), lambda m, n: (m, 0)),
            pl.BlockSpec((TM4, M_WIDTH), lambda m, n: (m, 0)),
            pl.BlockSpec((TM4, M_WIDTH), lambda m, n: (m, C_MO // M_WIDTH)),
            pl.BlockSpec((1, M_WIDTH), lambda m, n: (0, 0)),
            pl.BlockSpec((TM4, D_WIDTH), lambda m, n: (m, 0)),
            pl.BlockSpec((M_WIDTH, TN4), lambda m, n: (0, n)),
            pl.BlockSpec((D_WIDTH, TN4), lambda m, n: (0, n)),
            pl.BlockSpec((TM4, TN4), lambda m, n: (m, C_GM // TN4 + n)),
            pl.BlockSpec((TM4, TN4), lambda m, n: (m, C_GD // TN4 + n)),
        ],
        out_specs=pl.BlockSpec((TM4, TN4), lambda m, n: (m, n)),
        out_shape=jax.ShapeDtypeStruct((R, D_MODEL), BF16),
        scratch_shapes=[pltpu.VMEM((TM4, M_WIDTH), BF16)],
        compiler_params=_cparams(("parallel", "arbitrary")),
        name="merge",
    )(hf, hb, proj, norm_g, b_out, wa, wb, proj, proj)


TM5 = 384


def _outproj_kernel(h0_ref, mg_ref, w_ref, g2_ref, h1_ref, u2_ref):
    h1 = h0_ref[...] + jnp.dot(mg_ref[...], w_ref[...], preferred_element_type=F32)
    h1_ref[...] = h1
    r = lax.rsqrt(jnp.mean(h1 * h1, axis=-1, keepdims=True) + EPS)
    u2_ref[...] = ((h1 * r) * g2_ref[...]).astype(BF16)


def _outproj(h0, merged, w, g2):
    return pl.pallas_call(
        _outproj_kernel,
        grid=(R // TM5,),
        in_specs=[
            pl.BlockSpec((TM5, D_MODEL), lambda m: (m, 0)),
            pl.BlockSpec((TM5, D_MODEL), lambda m: (m, 0)),
            pl.BlockSpec((D_MODEL, D_MODEL), lambda m: (0, 0)),
            pl.BlockSpec((1, D_MODEL), lambda m: (0, 0)),
        ],
        out_specs=[
            pl.BlockSpec((TM5, D_MODEL), lambda m: (m, 0)),
            pl.BlockSpec((TM5, D_MODEL), lambda m: (m, 0)),
        ],
        out_shape=[
            jax.ShapeDtypeStruct((R, D_MODEL), F32),
            jax.ShapeDtypeStruct((R, D_MODEL), BF16),
        ],
        compiler_params=_cparams(("parallel",)),
        name="outproj",
    )(h0, merged, w, g2)


TMU = 1024
TNU = 512
HALO = 16
NNU = FFN_DIM // TNU


def _ffn_up_kernel(x_ref, xp_ref, xn_ref, wg_ref, wv_ref, cg_ref, cv_ref, o_ref, xs_ref, sg_ref, sv_ref):
    @pl.when(pl.program_id(2) == 0)
    def _():
        last_tile = pl.program_id(1) == SEQ // TMU - 1
        xs_ref[0:HALO, :] = xp_ref[...]
        xs_ref[HALO:HALO + TMU, :] = x_ref[...]
        xn = xn_ref[...]
        xs_ref[HALO + TMU:, :] = jnp.where(last_tile, jnp.zeros_like(xn), xn)

    def branch(w_ref, c, s_ref):
        s_ref[...] = jnp.dot(xs_ref[...], w_ref[...].astype(BF16), preferred_element_type=F32)
        return _conv3_rows(s_ref, c, HALO, TMU)

    gate_half = branch(wg_ref, 0.5 * cg_ref[...], sg_ref)
    val = branch(wv_ref, cv_ref[...], sv_ref)
    o_ref[...] = (_silu_of_half(gate_half) * val).astype(o_ref.dtype)


def _ffn_up(u2_3, w_up, conv_w):
    hb = TMU // HALO
    meta_hblk = SEQ // HALO
    return pl.pallas_call(
        _ffn_up_kernel,
        grid=(BATCH, SEQ // TMU, NNU),
        in_specs=[
            pl.BlockSpec((None, TMU, D_MODEL), lambda b, i, n: (b, i, 0)),
            pl.BlockSpec((None, HALO, D_MODEL), lambda b, i, n: (b, jnp.where(i == 0, meta_hblk, i * hb - 1), 0)),
            pl.BlockSpec((None, HALO, D_MODEL), lambda b, i, n: (b, (i + 1) * hb, 0)),
            pl.BlockSpec((D_MODEL, TNU), lambda b, i, n: (0, n)),
            pl.BlockSpec((D_MODEL, TNU), lambda b, i, n: (0, NNU + n)),
            pl.BlockSpec((3, TNU), lambda b, i, n: (0, n)),
            pl.BlockSpec((3, TNU), lambda b, i, n: (0, NNU + n)),
        ],
        out_specs=pl.BlockSpec((None, TMU, TNU), lambda b, i, n: (b, i, n)),
        out_shape=jax.ShapeDtypeStruct((BATCH, SEQ, FFN_DIM), BF16),
        scratch_shapes=[
            pltpu.VMEM((TMU + 2 * HALO, D_MODEL), BF16),
            pltpu.VMEM((TMU + 2 * HALO, TNU), F32),
            pltpu.VMEM((TMU + 2 * HALO, TNU), F32),
        ],
        compiler_params=_cparams(("parallel", "parallel", "arbitrary")),
        name="ffn_up",
    )(u2_3, u2_3, u2_3, w_up, w_up, conv_w, conv_w)


TM7 = 1024
TK7 = 512
NK7 = FFN_DIM // TK7


def _down_kernel(a_ref, wd_ref, h1_ref, gf_ref, o_ref, acc_ref):
    k = pl.program_id(2)

    def part():
        return jnp.dot(a_ref[...], wd_ref[...], preferred_element_type=F32)

    @pl.when(k == 0)
    def _():
        acc_ref[...] = part()

    @pl.when(jnp.logical_and(k > 0, k < NK7 - 1))
    def _():
        acc_ref[...] += part()

    @pl.when(k == NK7 - 1)
    def _():
        h2 = h1_ref[...] + (acc_ref[...] + part())
        r = lax.rsqrt(jnp.mean(h2 * h2, axis=-1, keepdims=True) + EPS)
        o_ref[...] = (h2 * r) * gf_ref[...]


def _ffn_down(act, w_down, h1_3, gf):
    return pl.pallas_call(
        _down_kernel,
        grid=(BATCH, SEQ // TM7, NK7),
        in_specs=[
            pl.BlockSpec((None, TM7, TK7), lambda b, i, k: (b, i, k)),
            pl.BlockSpec((TK7, D_MODEL), lambda b, i, k: (k, 0)),
            pl.BlockSpec((None, TM7, D_MODEL), lambda b, i, k: (b, i, 0)),
            pl.BlockSpec((1, D_MODEL), lambda b, i, k: (0, 0)),
        ],
        out_specs=pl.BlockSpec((None, TM7, D_MODEL), lambda b, i, k: (b, i, 0)),
        out_shape=jax.ShapeDtypeStruct((BATCH, SEQ, D_MODEL), F32),
        scratch_shapes=[pltpu.VMEM((TM7, D_MODEL), F32)],
        compiler_params=_cparams(("parallel", "parallel", "arbitrary")),
        name="ffn_down",
    )(act, w_down, h1_3, gf)


def kernel(x, meta_tokens, norm1_g, w_in, mlstm_conv_w, mlstm_gate_bias, mlstm_norm_g, lambda_q1, lambda_k1,
           lambda_q2, lambda_k2, diff_subln_g, w_branch_m, w_branch_d, w_out, norm2_g, w_up, ffn_conv_w,
           w_down, norm_f_g):
    assert x.shape == (BATCH, SEQ, D_MODEL) and w_in.shape[0] == 1
    meta = jnp.broadcast_to(meta_tokens[None].astype(x.dtype), (BATCH, N_META, D_MODEL))
    pad = jnp.zeros((BATCH, LP - SEQ - N_META, D_MODEL), x.dtype)
    h0 = jnp.concatenate([x, meta, pad], axis=1)
    h0f = h0.reshape(R, D_MODEL)

    wit = w_in[0].T
    halved = np.ones((wit.shape[0], 1), np.float32)
    for c0, width in ((C_MO, M_WIDTH), (C_GM + GROWS, D_MODEL), (C_GD + GROWS, D_MODEL)):
        halved[c0:c0 + width] = 0.5
    w_pa = w_pb = (wit * halved).astype(BF16)
    gbias = mlstm_gate_bias[0].reshape(GROWS, 1).astype(F32)
    g1 = norm1_g[0].reshape(1, D_MODEL)

    proj, gt_tiles = _inproj(h0f, g1, w_pa, w_pb, wit, gbias, _rope_tables())
    proj3 = proj.reshape(BATCH, LP, PROJ_W)

    gt = gt_tiles[:, :, :TM1].reshape(BATCH, LP // TM1, GROWS, TM1).transpose(0, 2, 1, 3).reshape(BATCH, GROWS, LP)
    arow, gcol = _scans(gt)

    kscale = jnp.concatenate([jnp.ones((1, M_WIDTH), F32), jnp.full((1, M_WIDTH), M_HEAD_DIM ** -0.5, F32)], axis=1)
    qk3 = _qkconv(proj3, mlstm_conv_w[0], kscale)
    hf, hb = _mlstm(qk3, proj3, gcol, arow)

    lam4 = jnp.stack([lambda_q1[0], lambda_k1[0], lambda_q2[0], lambda_k2[0]]).astype(F32)
    b_out = _attention(lam4, proj3, diff_subln_g[0].reshape(1, D_V_DIM))

    merged = _merge(hf.reshape(R, M_WIDTH), hb.reshape(R, M_WIDTH), proj,
                    (0.5 * mlstm_norm_g[0]).reshape(1, M_WIDTH), b_out.reshape(R, D_WIDTH),
                    (0.5 * w_branch_m[0]).astype(BF16), (0.5 * w_branch_d[0]).astype(BF16))
    h1, u2 = _outproj(h0f, merged, w_out[0].astype(BF16), norm2_g[0].reshape(1, D_MODEL))

    act = _ffn_up(u2.reshape(BATCH, LP, D_MODEL), w_up[0], ffn_conv_w[0])
    return _ffn_down(act, w_down[0].astype(BF16), h1.reshape(BATCH, LP, D_MODEL), norm_f_g.reshape(1, D_MODEL))
```

```python
import math

import numpy as np
import jax
import jax.numpy as jnp
from jax import lax
from jax.experimental import pallas as pl
from jax.experimental.pallas import tpu as pltpu

F32 = jnp.float32
BF16 = jnp.bfloat16

D_MODEL = 2048
BATCH = 2
SEQ = 4096
N_META = 16
EPS = 1e-6
M_HEADS = 4
M_WIDTH = 1024
M_HEAD_DIM = 256
D_HEADS = 8
D_V_DIM = 128
D_QK_DIM = 64
D_WIDTH = 1024
ROPE_THETA = 500000.0
ROPE_DIM = 16
FFN_DIM = 5632
LAM_INIT = 0.8 - 0.6 * math.exp(-0.3 * 0)

BLK = 128
LP = SEQ + BLK
NBLK = LP // BLK
META_BLK = SEQ // BLK
R = BATCH * LP
NEG = -1e30

C_MQK, C_MV, C_MO, C_DQ, C_DK, C_DV, C_GM, C_GD = 0, 2048, 3072, 4096, 5120, 6144, 7168, 9216
PROJ_W = 11264

V7X_VMEM_BYTES = 64 * 1024 * 1024
VMEM_LIMIT = V7X_VMEM_BYTES - 8 * 1024 * 1024


def _cparams(sem):
    return pltpu.CompilerParams(dimension_semantics=sem, vmem_limit_bytes=VMEM_LIMIT)


def _twice_sigmoid_of_half(xh):
    return jnp.tanh(xh) + 1.0


def _silu_of_half(yh):
    return yh * (jnp.tanh(yh) + 1.0)


TM0 = 352
NT0 = LP // TM0
X_TAIL0 = (NT0 - 1) * TM0
assert X_TAIL0 <= SEQ < X_TAIL0 + TM0


def _tail_tile(x, meta_tokens):
    meta = jnp.broadcast_to(meta_tokens[None].astype(x.dtype), (BATCH, N_META, D_MODEL))
    pad = jnp.zeros((BATCH, LP - SEQ - N_META, D_MODEL), x.dtype)
    return jnp.concatenate([x[:, X_TAIL0:], meta, pad], axis=1)


def _x_tile_specs():
    return [
        pl.BlockSpec((None, TM0, D_MODEL), lambda b, j, *_: (b, jnp.minimum(j, NT0 - 2), 0)),
        pl.BlockSpec((None, TM0, D_MODEL), lambda b, j, *_: (b, 0, 0)),
    ]


def _prenorm_kernel(x_ref, t_ref, g_ref, u_ref):
    def run(src_ref):
        x = src_ref[...]
        r = lax.rsqrt(jnp.mean(x * x, axis=-1, keepdims=True) + EPS)
        u_ref[...] = ((x * r) * g_ref[...]).astype(BF16)

    pl.when(pl.program_id(1) < NT0 - 1)(lambda: run(x_ref))
    pl.when(pl.program_id(1) == NT0 - 1)(lambda: run(t_ref))


def _prenorm(x, tail, g):
    return pl.pallas_call(
        _prenorm_kernel,
        grid=(BATCH, NT0),
        in_specs=_x_tile_specs() +[pl.BlockSpec((1, D_MODEL), lambda b, j: (0, 0))],
        out_specs=pl.BlockSpec((None, TM0, D_MODEL), lambda b, j: (b, j, 0)),
        out_shape=jax.ShapeDtypeStruct((BATCH, LP, D_MODEL), BF16),
        compiler_params=_cparams(("parallel", "parallel")),
        name="prenorm",
    )(x, tail, g)


TM1 = 1408
TN1 = 1024


def _rope_tile(acc, c, s):
    lane = lax.broadcasted_iota(jnp.int32, c.shape, 1)
    first = (lane % D_QK_DIM) < (ROPE_DIM // 2)
    outs = []
    for g in range(acc.shape[1] // 128):
        y = acc[:, g * 128:(g + 1) * 128]
        rot = jnp.where(first, pltpu.roll(y, 128 - ROPE_DIM // 2, 1), pltpu.roll(y, ROPE_DIM // 2, 1))
        outs.append(y * c + rot * s)
    return jnp.concatenate(outs, axis=1)


NA1 = C_DQ // TN1
GROWS = 16
GATE_COL0 = 4 * M_WIDTH


def _inproj_kernel(u_ref, wa_ref, wb_ref, wg_ref, gb_ref, cq_ref, sq_ref, ck_ref, sk_ref, o_ref, gt_ref):
    n = pl.program_id(1)

    @pl.when(n == 0)
    def _():
        gt = lax.dot_general(wg_ref[...].astype(BF16), u_ref[...], (((1,), (1,)), ((), ())),
                             preferred_element_type=F32)
        gt_ref[...] = gt + gb_ref[...]

    def mm(wt_ref):
        return lax.dot_general(u_ref[...], wt_ref[...], (((1,), (1,)), ((), ())), preferred_element_type=F32)

    q0, k0, v0 = C_DQ // TN1, C_DK // TN1, C_DV // TN1

    @pl.when(n < NA1)
    def _():
        o_ref[...] = mm(wa_ref).astype(o_ref.dtype)

    @pl.when(jnp.logical_and(n >= q0, n < k0))
    def _():
        o_ref[...] = _rope_tile(mm(wb_ref), cq_ref[...], sq_ref[...]).astype(o_ref.dtype)

    @pl.when(jnp.logical_and(n >= k0, n < v0))
    def _():
        o_ref[...] = _rope_tile(mm(wb_ref), ck_ref[...], sk_ref[...]).astype(o_ref.dtype)

    @pl.when(n >= v0)
    def _():
        o_ref[...] = mm(wb_ref).astype(o_ref.dtype)


def _inproj(u, wa, wb, w_in_t, gbias, tabs):
    tiles_per_batch = LP // TM1
    tab_spec = pl.BlockSpec((TM1, 128), lambda m, n: (m % tiles_per_batch, 0))
    return pl.pallas_call(
        _inproj_kernel,
        grid=(R // TM1, PROJ_W // TN1),
        in_specs=[
            pl.BlockSpec((TM1, D_MODEL), lambda m, n: (m, 0)),
            pl.BlockSpec((TN1, D_MODEL), lambda m, n: (jnp.minimum(n, NA1 - 1), 0)),
            pl.BlockSpec((pl.Element(TN1), pl.Element(D_MODEL)),
                         lambda m, n: (pl.multiple_of(GATE_COL0 + GROWS + jnp.maximum(n - NA1, 0) * TN1, GROWS), 0)),
            pl.BlockSpec((GROWS, D_MODEL), lambda m, n: (GATE_COL0 // GROWS, 0)),
            pl.BlockSpec((GROWS, 1), lambda m, n: (0, 0)),
            tab_spec, tab_spec, tab_spec, tab_spec,
        ],
        out_specs=[
            pl.BlockSpec((TM1, TN1), lambda m, n: (m, n)),
            pl.BlockSpec((GROWS, TM1), lambda m, n: (0, m)),
        ],
        out_shape=[
            jax.ShapeDtypeStruct((R, PROJ_W), BF16),
            jax.ShapeDtypeStruct((GROWS, R), F32),
        ],
        compiler_params=_cparams(("parallel", "arbitrary")),
        name="inproj",
    )(u, wa, wb, w_in_t, gbias, *tabs)


def _rope_tables():
    pos = np.zeros((LP,), np.float64)
    pos[:SEQ] = np.arange(SEQ) + N_META
    pos[SEQ:SEQ + N_META] = np.arange(N_META)
    half = ROPE_DIM // 2
    inv = ROPE_THETA ** (-np.arange(0, ROPE_DIM, 2, dtype=np.float64) / ROPE_DIM)
    ang = pos[:, None] * inv[None, :]
    cos64 = np.ones((LP, D_QK_DIM)); sin64 = np.zeros((LP, D_QK_DIM))
    cos64[:, :half] = np.cos(ang); cos64[:, half:ROPE_DIM] = np.cos(ang)
    sin64[:, :half] = -np.sin(ang); sin64[:, half:ROPE_DIM] = np.sin(ang)
    cos = np.concatenate([cos64, cos64], axis=1)
    sin = np.concatenate([sin64, sin64], axis=1)
    qs = D_QK_DIM ** -0.5 * math.log2(math.e)
    return [jnp.asarray(t, F32) for t in (cos * qs, sin * qs, cos, sin)]


def _lane_scan(x, op, ident, reverse):
    lane = lax.broadcasted_iota(jnp.int32, x.shape, 1)
    k = 1
    while k < BLK:
        if reverse:
            sh = pltpu.roll(x, BLK - k, 1)
            x = op(x, jnp.where(lane < BLK - k, sh, ident))
        else:
            sh = pltpu.roll(x, k, 1)
            x = op(x, jnp.where(lane >= k, sh, ident))
        k *= 2
    return x


def _log_sigmoid(x):
    return jnp.minimum(x, 0.0) - jnp.log1p(jnp.exp(-jnp.abs(x)))


def _scan_kernel(g_ref, arow_ref, gcol_ref, colt_ref):
    colt_ref[...] = jnp.zeros_like(colt_ref)
    lane = lax.broadcasted_iota(jnp.int32, (8, BLK), 1)
    for d in range(2):
        reverse = d == 1
        carry_b = jnp.zeros((8, BLK), F32)
        carry_m = jnp.full((8, BLK), NEG, F32)
        logical = [META_BLK] + list(range(META_BLK))
        order = logical[::-1] if reverse else logical
        edge = 0 if reverse else BLK - 1
        for j in order:
            cols = slice(j * BLK, (j + 1) * BLK)
            li = g_ref[d * 8:(d + 1) * 8, cols]
            lf = pltpu.roll(_log_sigmoid(li), M_HEADS, 0)
            if j == META_BLK:
                valid = lane < N_META
                li = jnp.where(valid, li, NEG)
                lf = jnp.where(valid, lf, 0.0)
            bc = carry_b + _lane_scan(lf, jnp.add, 0.0, reverse)
            a = li - bc
            m = jnp.maximum(carry_m, _lane_scan(a, jnp.maximum, NEG, reverse))
            m_last = jnp.broadcast_to(m[:, edge:edge + 1], (8, BLK))
            arow_ref[d * 8:(d + 1) * 8, cols] = a
            base = d * 40
            colt_ref[j, base:base + 8, :] = a
            colt_ref[j, base + 8:base + 16, :] = m
            colt_ref[j, base + 16:base + 24, :] = bc + m
            colt_ref[j, base + 24:base + 32, :] = carry_m
            colt_ref[j, base + 32:base + 40, :] = m_last
            carry_b = jnp.broadcast_to(bc[:, edge:edge + 1], (8, BLK))
            carry_m = m_last
    for j in range(NBLK):
        gcol_ref[j * BLK:(j + 1) * BLK, :] = colt_ref[j].T


def _scans(gt):
    return pl.pallas_call(
        _scan_kernel,
        grid=(BATCH,),
        in_specs=[pl.BlockSpec((GROWS, LP), lambda b: (0, b))],
        out_specs=[
            pl.BlockSpec((None, 16, LP), lambda b: (b, 0, 0)),
            pl.BlockSpec((None, LP, BLK), lambda b: (b, 0, 0)),
        ],
        out_shape=[
            jax.ShapeDtypeStruct((BATCH, 16, LP), F32),
            jax.ShapeDtypeStruct((BATCH, LP, BLK), F32),
        ],
        scratch_shapes=[pltpu.VMEM((NBLK, BLK, BLK), F32)],
        compiler_params=_cparams(("parallel",)),
        name="gate_scans",
    )(gt)


TC_COLS = 256
CONV_CH = 256


def _conv3_rows(s_ref, w, lo, n):
    x = s_ref[lo:lo + n, :]
    prev = jnp.concatenate([s_ref[lo - 1:lo + 7, :], pltpu.roll(x, 1, 0)[8:]], axis=0)
    nxt = jnp.concatenate([pltpu.roll(x, n - 1, 0)[:n - 8], s_ref[lo + n - 7:lo + n + 1, :]], axis=0)
    return prev * w[0:1] + x * w[1:2] + nxt * w[2:3]


def _qkconv_kernel(x_ref, w_ref, sc_ref, o_ref, s_ref):
    xo = 8 + N_META
    s_ref[0:8, :] = jnp.zeros((8, TC_COLS), F32)
    s_ref[xo + SEQ:xo + SEQ + 8, :] = jnp.zeros((8, TC_COLS), F32)
    s_ref[8:xo, :] = x_ref[SEQ:SEQ + N_META, :].astype(F32)
    for i in range(SEQ // CONV_CH):
        s_ref[xo + i * CONV_CH:xo + (i + 1) * CONV_CH, :] = x_ref[i * CONV_CH:(i + 1) * CONV_CH, :].astype(F32)
    w = 0.5 * w_ref[...]
    sc = sc_ref[...]

    def act(yh):
        return (_silu_of_half(yh) * sc).astype(o_ref.dtype)

    for i in range(SEQ // CONV_CH):
        o_ref[i * CONV_CH:(i + 1) * CONV_CH, :] = act(_conv3_rows(s_ref, w, xo + i * CONV_CH, CONV_CH))
    o_ref[SEQ:SEQ + N_META, :] = act(_conv3_rows(s_ref, w, 8, N_META))
    o_ref[SEQ + N_META:LP, :] = jnp.zeros((LP - SEQ - N_META, TC_COLS), o_ref.dtype)


def _qkconv(proj3, w, scale):
    return pl.pallas_call(
        _qkconv_kernel,
        grid=(BATCH, 2 * M_WIDTH // TC_COLS),
        in_specs=[
            pl.BlockSpec((None, LP, TC_COLS), lambda b, c: (b, 0, C_MQK // TC_COLS + c)),
            pl.BlockSpec((3, TC_COLS), lambda b, c: (0, c)),
            pl.BlockSpec((1, TC_COLS), lambda b, c: (0, c)),
        ],
        out_specs=pl.BlockSpec((None, LP, TC_COLS), lambda b, c: (b, 0, c)),
        out_shape=jax.ShapeDtypeStruct((BATCH, LP, 2 * M_WIDTH), BF16),
        scratch_shapes=[pltpu.VMEM((SEQ + N_META + 16, TC_COLS), F32)],
        compiler_params=_cparams(("parallel", "parallel")),
        name="qk_conv",
    )(proj3, w, scale)


def _mlstm_kernel(qf, kf, vf, gcf, arf, qb, kb, vb, gcb, arb, hf_ref, hb_ref, c_ref):
    c = pl.program_id(1)

    @pl.when(c == 0)
    def _():
        c_ref[...] = jnp.zeros_like(c_ref)

    row = lax.broadcasted_iota(jnp.int32, (BLK, BLK), 0)
    col = lax.broadcasted_iota(jnp.int32, (BLK, BLK), 1)
    ones_b = jnp.ones((BLK, 128), BF16)
    ones_f = jnp.ones((BLK, 128), F32)
    dirs = ((qf, kf, vf, gcf, arf, hf_ref), (qb, kb, vb, gcb, arb, hb_ref))
    for d, (q_ref, k_ref, v_ref, gc_ref, ar_ref, o_ref) in enumerate(dirs):
        mask = (col <= row) if d == 0 else (col >= row)
        gc = gc_ref[...]
        for h in range(M_HEADS):
            st = d * M_HEADS + h
            def colq(qi):
                j = d * 40 + qi * 8 + h
                return gc[:, j:j + 1]
            a_col, m_col, e_col, mp_col, ml_col = (colq(i) for i in range(5))
            a_row = ar_ref[d * 8 + h:d * 8 + h + 1, :]
            hs = slice(h * M_HEAD_DIM, (h + 1) * M_HEAD_DIM)
            q = q_ref[:, hs]
            k = k_ref[:, hs]
            v = v_ref[:, hs]
            s = lax.dot_general(q, k, (((1,), (1,)), ((), ())), preferred_element_type=F32)
            p = jnp.where(mask, jnp.exp(a_row - m_col), 0.0)
            sp = (s * p).astype(BF16)
            a = jnp.exp(mp_col - m_col)
            cn = c_ref[st]
            tot = (jnp.dot(sp, jnp.concatenate([v, ones_b], axis=1), preferred_element_type=F32)
                   + a * jnp.dot(q, cn.astype(BF16), preferred_element_type=F32))
            r = 1.0 / jnp.maximum(jnp.abs(tot[:, M_HEAD_DIM:]), jnp.exp(-e_col))
            o_ref[:, hs] = tot[:, :M_HEAD_DIM] * jnp.concatenate([r, r], axis=1)
            w = jnp.exp(a_col - ml_col)
            dec = jnp.exp(mp_col[0:1, :] - ml_col[0:1, :])
            wvn = (w * jnp.concatenate([v.astype(F32), ones_f], axis=1)).astype(BF16)
            c_ref[st] = dec * cn + lax.dot_general(
                k, wvn, (((0,), (0,)), ((), ())), preferred_element_type=F32)


def _fwd_blk(c):
    return (c + NBLK - 1) % NBLK


def _bwd_blk(c):
    return (2 * NBLK - 2 - c) % NBLK


def _mlstm(qk3, proj3, gcol, arow):
    def specs(blk):
        return [
            pl.BlockSpec((None, BLK, M_WIDTH), lambda b, c: (b, blk(c), 0)),
            pl.BlockSpec((None, BLK, M_WIDTH), lambda b, c: (b, blk(c), 1)),
            pl.BlockSpec((None, BLK, M_WIDTH), lambda b, c: (b, blk(c), C_MV // M_WIDTH)),
            pl.BlockSpec((None, BLK, BLK), lambda b, c: (b, blk(c), 0)),
            pl.BlockSpec((None, 16, BLK), lambda b, c: (b, 0, blk(c))),
        ]
    out_sd = jax.ShapeDtypeStruct((BATCH, LP, M_WIDTH), F32)
    return pl.pallas_call(
        _mlstm_kernel,
        grid=(BATCH, NBLK),
        in_specs=specs(_fwd_blk) + specs(_bwd_blk),
        out_specs=[
            pl.BlockSpec((None, BLK, M_WIDTH), lambda b, c: (b, _fwd_blk(c), 0)),
            pl.BlockSpec((None, BLK, M_WIDTH), lambda b, c: (b, _bwd_blk(c), 0)),
        ],
        out_shape=[out_sd, out_sd],
        scratch_shapes=[pltpu.VMEM((2 * M_HEADS, M_HEAD_DIM, M_HEAD_DIM + 128), F32)],
        compiler_params=_cparams(("parallel", "arbitrary")),
        name="mlstm",
    )(qk3, qk3, proj3, gcol, arow, qk3, qk3, proj3, gcol, arow)


RQ = 128
NGRP = LP // RQ
EXP_CH = 512
ATTN_NBUF = 2


def _attn_kernel(lam_ref, q_ref, k_ref, v_ref, g_ref, o_ref, *scratch):
    s_refs, p_refs, l_refs, m_refs = (scratch[i * ATTN_NBUF:(i + 1) * ATTN_NBUF] for i in range(4))
    lq = lam_ref[...]
    la = jnp.sum(lq[0:1] * lq[1:2], axis=1, keepdims=True)
    lb = jnp.sum(lq[2:3] * lq[3:4], axis=1, keepdims=True)
    lam = jnp.exp(la) - jnp.exp(lb) + LAM_INIT

    lane = lax.broadcasted_iota(jnp.int32, (RQ, 128), 1)
    colid = lax.broadcasted_iota(jnp.int32, (1, BLK), 1)

    def rows(g):
        return pl.ds(pl.multiple_of(g * RQ, RQ), RQ)

    def stage_scores(g, s_ref, m_ref):
        qg = q_ref[rows(g), :]
        zero = jnp.zeros_like(qg)
        q2 = jnp.concatenate([jnp.where(lane < D_QK_DIM, qg, zero), jnp.where(lane >= D_QK_DIM, qg, zero)], axis=0)
        s = lax.dot_general(q2, k_ref[...], (((1,), (1,)), ((), ())), preferred_element_type=F32)
        tail = jnp.where(colid < N_META, s[:, LP - BLK:], -jnp.inf)
        s_ref[:, :LP - BLK] = s[:, :LP - BLK]
        s_ref[:, LP - BLK:] = tail
        m = jnp.maximum(jnp.max(s[:, :LP - BLK], axis=1, keepdims=True), jnp.max(tail, axis=1, keepdims=True))
        m_ref[...] = jnp.broadcast_to(m, m_ref.shape)

    def stage_exp(s_ref, m_ref, p_ref, l_ref):
        m = m_ref[:, 0:1]
        part = jnp.zeros((2 * RQ, 128), F32)
        for c0 in range(0, LP, EXP_CH):
            c1 = min(c0 + EXP_CH, LP)
            pc = jnp.exp2(s_ref[:, c0:c1] - m)
            for j in range(0, c1 - c0, 128):
                part = part + pc[:, j:j + 128]
            p_ref[:, c0:c1] = pc.astype(BF16)
        l_ref[...] = jnp.broadcast_to(jnp.sum(part, axis=1, keepdims=True), l_ref.shape)

    def stage_pv(g, p_ref, l_ref):
        l = l_ref[:, 0:1]
        ratio = (lam * l[:RQ] / l[RQ:]).astype(BF16)
        a = p_ref[:RQ, :] - ratio * p_ref[RQ:, :]
        o = jnp.dot(a, v_ref[...], preferred_element_type=F32) / l[:RQ]
        r = lax.rsqrt(jnp.mean(o * o, axis=-1, keepdims=True) + EPS)
        o_ref[rows(g), :] = (((o * r) * g_ref[...]) * (1.0 - LAM_INIT)).astype(o_ref.dtype)

    nb = len(s_refs)
    sbuf = tuple(zip(s_refs, m_refs))
    pbuf = tuple(zip(p_refs, l_refs))

    def tick(t, static):
        if static < NGRP:
            stage_scores(t, *sbuf[static % nb])
        if 1 <= static <= NGRP:
            stage_exp(*sbuf[(static - 1) % nb], *pbuf[(static - 1) % nb])
        if 2 <= static <= NGRP + 1:
            stage_pv(t - 2, *pbuf[(static - 2) % nb])

    tick(0, 0)
    tick(1, 1)
    n_rounds = (NGRP - 2) // nb

    def body(i, carry):
        t = nb * i + 2
        for j in range(nb):
            tick(t + j, 2 + j)
        return carry

    lax.fori_loop(0, n_rounds, body, 0)
    for t in range(2 + nb * n_rounds, NGRP + 2):
        tick(t, t)


def _attention(lam4, proj3, subln_g):
    return pl.pallas_call(
        _attn_kernel,
        grid=(BATCH, D_HEADS),
        in_specs=[
            pl.BlockSpec((4, D_QK_DIM), lambda b, h: (0, 0)),
            pl.BlockSpec((None, LP, 128), lambda b, h: (b, 0, C_DQ // 128 + h)),
            pl.BlockSpec((None, LP, 128), lambda b, h: (b, 0, C_DK // 128 + h)),
            pl.BlockSpec((None, LP, 128), lambda b, h: (b, 0, C_DV // 128 + h)),
            pl.BlockSpec((1, D_V_DIM), lambda b, h: (0, 0)),
        ],
        out_specs=pl.BlockSpec((None, LP, D_V_DIM), lambda b, h: (b, 0, h)),
        out_shape=jax.ShapeDtypeStruct((BATCH, LP, D_WIDTH), BF16),
        scratch_shapes=(
            [pltpu.VMEM((2 * RQ, LP), F32)] * ATTN_NBUF
            + [pltpu.VMEM((2 * RQ, LP), BF16)] * ATTN_NBUF
            + [pltpu.VMEM((2 * RQ, 128), F32)] * ATTN_NBUF
            + [pltpu.VMEM((2 * RQ, 128), F32)] * ATTN_NBUF
        ),
        compiler_params=_cparams(("parallel", "parallel")),
        name="diff_attn",
    )(lam4, proj3, proj3, proj3, subln_g)


TM4 = 1056
TN4 = 512
LN_CH = 96


def _merge_kernel(hf_ref, hb_ref, mo_ref, ng_ref, bo_ref, wa_ref, wb_ref, gm_ref, gd_ref, o_ref, a_ref):
    n = pl.program_id(1)

    @pl.when(n == 0)
    def _():
        def body(i, carry):
            rows = pl.ds(pl.multiple_of(i * LN_CH, LN_CH), LN_CH)
            for h in range(M_HEADS):
                hs = slice(h * M_HEAD_DIM, (h + 1) * M_HEAD_DIM)
                x = hf_ref[rows, hs] + hb_ref[rows, hs]
                mu = jnp.mean(x, axis=-1, keepdims=True)
                xc = x - mu
                var = jnp.mean(xc * xc, axis=-1, keepdims=True)
                hn = (xc * lax.rsqrt(var + EPS)) * ng_ref[:, hs]
                a_ref[rows, hs] = (_twice_sigmoid_of_half(mo_ref[rows, hs].astype(F32)) * hn).astype(BF16)
            return carry
        lax.fori_loop(0, TM4 // LN_CH, body, 0)

    ya = jnp.dot(a_ref[...], wa_ref[...], preferred_element_type=F32)
    yb = jnp.dot(bo_ref[...], wb_ref[...], preferred_element_type=F32)
    o_ref[...] = (_twice_sigmoid_of_half(gm_ref[...].astype(F32)) * ya
                  + _twice_sigmoid_of_half(gd_ref[...].astype(F32)) * yb).astype(o_ref.dtype)


def _merge(hf, hb, proj, norm_g, b_out, wa, wb):
    return pl.pallas_call(
        _merge_kernel,
        grid=(R // TM4, D_MODEL // TN4),
        in_specs=[
            pl.BlockSpec((TM4, M_WIDTH), lambda m, n: (m, 0)),
            pl.BlockSpec((TM4, M_WIDTH), lambda m, n: (m, 0)),
            pl.BlockSpec((TM4, M_WIDTH), lambda m, n: (m, C_MO // M_WIDTH)),
            pl.BlockSpec((1, M_WIDTH), lambda m, n: (0, 0)),
            pl.BlockSpec((TM4, D_WIDTH), lambda m, n: (m, 0)),
            pl.BlockSpec((M_WIDTH, TN4), lambda m, n: (0, n)),
            pl.BlockSpec((D_WIDTH, TN4), lambda m, n: (0, n)),
            pl.BlockSpec((TM4, TN4), lambda m, n: (m, C_GM // TN4 + n)),
            pl.BlockSpec((TM4, TN4), lambda m, n: (m, C_GD // TN4 + n)),
        ],
        out_specs=pl.BlockSpec((TM4, TN4), lambda m, n: (m, n)),
        out_shape=jax.ShapeDtypeStruct((R, D_MODEL), BF16),
        scratch_shapes=[pltpu.VMEM((TM4, M_WIDTH), BF16)],
        compiler_params=_cparams(("parallel", "arbitrary")),
        name="merge",
    )(hf, hb, proj, norm_g, b_out, wa, wb, proj, proj)


def _outproj_kernel(x_ref, t_ref, mg_ref, w_ref, g2_ref, h1_ref, u2_ref):
    def run(h0_ref):
        h1 = h0_ref[...] + jnp.dot(mg_ref[...], w_ref[...], preferred_element_type=F32)
        h1_ref[...] = h1
        r = lax.rsqrt(jnp.mean(h1 * h1, axis=-1, keepdims=True) + EPS)
        u2_ref[...] = ((h1 * r) * g2_ref[...]).astype(BF16)

    pl.when(pl.program_id(1) < NT0 - 1)(lambda: run(x_ref))
    pl.when(pl.program_id(1) == NT0 - 1)(lambda: run(t_ref))


def _outproj(x, tail, merged3, w, g2):
    row_spec = pl.BlockSpec((None, TM0, D_MODEL), lambda b, j: (b, j, 0))
    return pl.pallas_call(
        _outproj_kernel,
        grid=(BATCH, NT0),
        in_specs=_x_tile_specs() + [
            row_spec,
            pl.BlockSpec((D_MODEL, D_MODEL), lambda b, j: (0, 0)),
            pl.BlockSpec((1, D_MODEL), lambda b, j: (0, 0)),
        ],
        out_specs=[row_spec, row_spec],
        out_shape=[
            jax.ShapeDtypeStruct((BATCH, LP, D_MODEL), F32),
            jax.ShapeDtypeStruct((BATCH, LP, D_MODEL), BF16),
        ],
        compiler_params=_cparams(("parallel", "parallel")),
        name="outproj",
    )(x, tail, merged3, w, g2)


TMU = 1024
TNU = 512
HALO = 16
NNU = FFN_DIM // TNU


def _ffn_up_kernel(x_ref, xp_ref, xn_ref, wg_ref, wv_ref, cg_ref, cv_ref, o_ref, xs_ref, sg_ref, sv_ref):
    @pl.when(pl.program_id(2) == 0)
    def _():
        last_tile = pl.program_id(1) == SEQ // TMU - 1
        xs_ref[0:HALO, :] = xp_ref[...]
        xs_ref[HALO:HALO + TMU, :] = x_ref[...]
        xn = xn_ref[...]
        xs_ref[HALO + TMU:, :] = jnp.where(last_tile, jnp.zeros_like(xn), xn)

    def branch(w_ref, c, s_ref):
        s_ref[...] = jnp.dot(xs_ref[...], w_ref[...].astype(BF16), preferred_element_type=F32)
        return _conv3_rows(s_ref, c, HALO, TMU)

    gate_half = branch(wg_ref, 0.5 * cg_ref[...], sg_ref)
    val = branch(wv_ref, cv_ref[...], sv_ref)
    o_ref[...] = (_silu_of_half(gate_half) * val).astype(o_ref.dtype)


def _ffn_up(u2_3, w_up, conv_w):
    hb = TMU // HALO
    meta_hblk = SEQ // HALO
    return pl.pallas_call(
        _ffn_up_kernel,
        grid=(BATCH, SEQ // TMU, NNU),
        in_specs=[
            pl.BlockSpec((None, TMU, D_MODEL), lambda b, i, n: (b, i, 0)),
            pl.BlockSpec((None, HALO, D_MODEL), lambda b, i, n: (b, jnp.where(i == 0, meta_hblk, i * hb - 1), 0)),
            pl.BlockSpec((None, HALO, D_MODEL), lambda b, i, n: (b, (i + 1) * hb, 0)),
            pl.BlockSpec((D_MODEL, TNU), lambda b, i, n: (0, n)),
            pl.BlockSpec((D_MODEL, TNU), lambda b, i, n: (0, NNU + n)),
            pl.BlockSpec((3, TNU), lambda b, i, n: (0, n)),
            pl.BlockSpec((3, TNU), lambda b, i, n: (0, NNU + n)),
        ],
        out_specs=pl.BlockSpec((None, TMU, TNU), lambda b, i, n: (b, i, n)),
        out_shape=jax.ShapeDtypeStruct((BATCH, SEQ, FFN_DIM), BF16),
        scratch_shapes=[
            pltpu.VMEM((TMU + 2 * HALO, D_MODEL), BF16),
            pltpu.VMEM((TMU + 2 * HALO, TNU), F32),
            pltpu.VMEM((TMU + 2 * HALO, TNU), F32),
        ],
        compiler_params=_cparams(("parallel", "parallel", "arbitrary")),
        name="ffn_up",
    )(u2_3, u2_3, u2_3, w_up, w_up, conv_w, conv_w)


TM7 = 1024
TK7 = 512
NK7 = FFN_DIM // TK7


def _down_kernel(a_ref, wd_ref, h1_ref, gf_ref, o_ref, acc_ref):
    k = pl.program_id(2)

    def part():
        return jnp.dot(a_ref[...], wd_ref[...], preferred_element_type=F32)

    @pl.when(k == 0)
    def _():
        acc_ref[...] = part()

    @pl.when(jnp.logical_and(k > 0, k < NK7 - 1))
    def _():
        acc_ref[...] += part()

    @pl.when(k == NK7 - 1)
    def _():
        h2 = h1_ref[...] + (acc_ref[...] + part())
        r = lax.rsqrt(jnp.mean(h2 * h2, axis=-1, keepdims=True) + EPS)
        o_ref[...] = (h2 * r) * gf_ref[...]


def _ffn_down(act, w_down, h1_3, gf):
    return pl.pallas_call(
        _down_kernel,
        grid=(BATCH, SEQ // TM7, NK7),
        in_specs=[
            pl.BlockSpec((None, TM7, TK7), lambda b, i, k: (b, i, k)),
            pl.BlockSpec((TK7, D_MODEL), lambda b, i, k: (k, 0)),
            pl.BlockSpec((None, TM7, D_MODEL), lambda b, i, k: (b, i, 0)),
            pl.BlockSpec((1, D_MODEL), lambda b, i, k: (0, 0)),
        ],
        out_specs=pl.BlockSpec((None, TM7, D_MODEL), lambda b, i, k: (b, i, 0)),
        out_shape=jax.ShapeDtypeStruct((BATCH, SEQ, D_MODEL), F32),
        scratch_shapes=[pltpu.VMEM((TM7, D_MODEL), F32)],
        compiler_params=_cparams(("parallel", "parallel", "arbitrary")),
        name="ffn_down",
    )(act, w_down, h1_3, gf)


def kernel(x, meta_tokens, norm1_g, w_in, mlstm_conv_w, mlstm_gate_bias, mlstm_norm_g, lambda_q1, lambda_k1,
           lambda_q2, lambda_k2, diff_subln_g, w_branch_m, w_branch_d, w_out, norm2_g, w_up, ffn_conv_w,
           w_down, norm_f_g):
    assert x.shape == (BATCH, SEQ, D_MODEL) and w_in.shape[0] == 1
    tail = _tail_tile(x, meta_tokens)
    u = _prenorm(x, tail, norm1_g[0].reshape(1, D_MODEL))

    wit = w_in[0].T
    halved = np.ones((wit.shape[0], 1), np.float32)
    for c0, width in ((C_MO, M_WIDTH), (C_GM + GROWS, D_MODEL), (C_GD + GROWS, D_MODEL)):
        halved[c0:c0 + width] = 0.5
    w_pa = w_pb = (wit * halved).astype(BF16)
    gbias = mlstm_gate_bias[0].reshape(GROWS, 1).astype(F32)

    proj, gt = _inproj(u.reshape(R, D_MODEL), w_pa, w_pb, wit, gbias, _rope_tables())
    proj3 = proj.reshape(BATCH, LP, PROJ_W)

    arow, gcol = _scans(gt)

    kscale = jnp.concatenate([jnp.ones((1, M_WIDTH), F32), jnp.full((1, M_WIDTH), M_HEAD_DIM ** -0.5, F32)], axis=1)
    qk3 = _qkconv(proj3, mlstm_conv_w[0], kscale)
    hf, hb = _mlstm(qk3, proj3, gcol, arow)

    lam4 = jnp.stack([lambda_q1[0], lambda_k1[0], lambda_q2[0], lambda_k2[0]]).astype(F32)
    b_out = _attention(lam4, proj3, diff_subln_g[0].reshape(1, D_V_DIM))

    merged = _merge(hf.reshape(R, M_WIDTH), hb.reshape(R, M_WIDTH), proj,
                    (0.5 * mlstm_norm_g[0]).reshape(1, M_WIDTH), b_out.reshape(R, D_WIDTH),
                    (0.5 * w_branch_m[0]).astype(BF16), (0.5 * w_branch_d[0]).astype(BF16))
    h1, u2 = _outproj(x, tail, merged.reshape(BATCH, LP, D_MODEL), w_out[0].astype(BF16),
                      norm2_g[0].reshape(1, D_MODEL))

    act = _ffn_up(u2, w_up[0], ffn_conv_w[0])
    return _ffn_down(act, w_down[0].astype(BF16), h1, norm_f_g.reshape(1, D_MODEL))
```

```python
import math

import numpy as np
import jax
import jax.numpy as jnp
from jax import lax
from jax.experimental import pallas as pl
from jax.experimental.pallas import tpu as pltpu

F32 = jnp.float32
BF16 = jnp.bfloat16

D_MODEL = 2048
BATCH = 2
SEQ = 4096
N_META = 16
EPS = 1e-6
M_HEADS = 4
M_WIDTH = 1024
M_HEAD_DIM = 256
D_HEADS = 8
D_V_DIM = 128
D_QK_DIM = 64
D_WIDTH = 1024
ROPE_THETA = 500000.0
ROPE_DIM = 16
FFN_DIM = 5632
LAM_INIT = 0.8 - 0.6 * math.exp(-0.3 * 0)

BLK = 128
LP = SEQ + BLK
NBLK = LP // BLK
META_BLK = SEQ // BLK
R = BATCH * LP
NEG = -1e30

C_MQK, C_MV, C_MO, C_DQ, C_DK, C_DV, C_GM, C_GD = 0, 2048, 3072, 4096, 5120, 6144, 7168, 9216
PROJ_W = 11264

V7X_VMEM_BYTES = 64 * 1024 * 1024
VMEM_LIMIT = V7X_VMEM_BYTES - 8 * 1024 * 1024


def _cparams(sem):
    return pltpu.CompilerParams(dimension_semantics=sem, vmem_limit_bytes=VMEM_LIMIT)


def _twice_sigmoid_of_half(xh):
    return jnp.tanh(xh) + 1.0


def _silu_of_half(yh):
    return yh * (jnp.tanh(yh) + 1.0)


TM0 = 384
NT0 = LP // TM0
X_TAIL0 = (NT0 - 1) * TM0
assert X_TAIL0 <= SEQ < X_TAIL0 + TM0


def _tail_tile(x, meta_tokens):
    meta = jnp.broadcast_to(meta_tokens[None].astype(x.dtype), (BATCH, N_META, D_MODEL))
    pad = jnp.zeros((BATCH, LP - SEQ - N_META, D_MODEL), x.dtype)
    return jnp.concatenate([x[:, X_TAIL0:], meta, pad], axis=1)


def _x_tile_specs():
    return [
        pl.BlockSpec((None, TM0, D_MODEL), lambda b, j, *_: (b, jnp.minimum(j, NT0 - 2), 0)),
        pl.BlockSpec((None, TM0, D_MODEL), lambda b, j, *_: (b, 0, 0)),
    ]


GROWS = 16
GATE_COL0 = 4 * M_WIDTH


def _prenorm_kernel(x_ref, t_ref, g_ref, wg_ref, gb_ref, u_ref, gt_ref):
    def run(src_ref):
        x = src_ref[...]
        r = lax.rsqrt(jnp.mean(x * x, axis=-1, keepdims=True) + EPS)
        u = ((x * r) * g_ref[...]).astype(BF16)
        u_ref[...] = u
        gt = lax.dot_general(wg_ref[...].astype(BF16), u, (((1,), (1,)), ((), ())), preferred_element_type=F32)
        gt_ref[...] = gt + gb_ref[...]

    pl.when(pl.program_id(1) < NT0 - 1)(lambda: run(x_ref))
    pl.when(pl.program_id(1) == NT0 - 1)(lambda: run(t_ref))


def _prenorm(x, tail, g, w_in_t, gbias):
    return pl.pallas_call(
        _prenorm_kernel,
        grid=(BATCH, NT0),
        in_specs=_x_tile_specs() + [
            pl.BlockSpec((1, D_MODEL), lambda b, j: (0, 0)),
            pl.BlockSpec((GROWS, D_MODEL), lambda b, j: (GATE_COL0 // GROWS, 0)),
            pl.BlockSpec((GROWS, 1), lambda b, j: (0, 0)),
        ],
        out_specs=[
            pl.BlockSpec((None, TM0, D_MODEL), lambda b, j: (b, j, 0)),
            pl.BlockSpec((GROWS, TM0), lambda b, j: (0, b * NT0 + j)),
        ],
        out_shape=[
            jax.ShapeDtypeStruct((BATCH, LP, D_MODEL), BF16),
            jax.ShapeDtypeStruct((GROWS, R), F32),
        ],
        compiler_params=_cparams(("parallel", "parallel")),
        name="prenorm",
    )(x, tail, g, w_in_t, gbias)


TM1 = 1408
TN1 = 1024


def _rope_tile(acc, c, s):
    lane = lax.broadcasted_iota(jnp.int32, c.shape, 1)
    first = (lane % D_QK_DIM) < (ROPE_DIM // 2)
    outs = []
    for g in range(acc.shape[1] // 128):
        y = acc[:, g * 128:(g + 1) * 128]
        rot = jnp.where(first, pltpu.roll(y, 128 - ROPE_DIM // 2, 1), pltpu.roll(y, ROPE_DIM // 2, 1))
        outs.append(y * c + rot * s)
    return jnp.concatenate(outs, axis=1)


NA1 = C_DQ // TN1


def _inproj_kernel(u_ref, wa_ref, wb_ref, cq_ref, sq_ref, ck_ref, sk_ref, o_ref):
    n = pl.program_id(1)

    def mm(wt_ref):
        return lax.dot_general(u_ref[...], wt_ref[...], (((1,), (1,)), ((), ())), preferred_element_type=F32)

    q0, k0, v0 = C_DQ // TN1, C_DK // TN1, C_DV // TN1

    @pl.when(n < NA1)
    def _():
        o_ref[...] = mm(wa_ref).astype(o_ref.dtype)

    @pl.when(jnp.logical_and(n >= q0, n < k0))
    def _():
        o_ref[...] = _rope_tile(mm(wb_ref), cq_ref[...], sq_ref[...]).astype(o_ref.dtype)

    @pl.when(jnp.logical_and(n >= k0, n < v0))
    def _():
        o_ref[...] = _rope_tile(mm(wb_ref), ck_ref[...], sk_ref[...]).astype(o_ref.dtype)

    @pl.when(n >= v0)
    def _():
        o_ref[...] = mm(wb_ref).astype(o_ref.dtype)


def _inproj(u, wa, wb, tabs):
    tiles_per_batch = LP // TM1
    tab_spec = pl.BlockSpec((TM1, 128), lambda m, n: (m % tiles_per_batch, 0))
    return pl.pallas_call(
        _inproj_kernel,
        grid=(R // TM1, PROJ_W // TN1),
        in_specs=[
            pl.BlockSpec((TM1, D_MODEL), lambda m, n: (m, 0)),
            pl.BlockSpec((TN1, D_MODEL), lambda m, n: (jnp.minimum(n, NA1 - 1), 0)),
            pl.BlockSpec((pl.Element(TN1), pl.Element(D_MODEL)),
                         lambda m, n: (pl.multiple_of(GATE_COL0 + GROWS + jnp.maximum(n - NA1, 0) * TN1, GROWS), 0)),
            tab_spec, tab_spec, tab_spec, tab_spec,
        ],
        out_specs=pl.BlockSpec((TM1, TN1), lambda m, n: (m, n)),
        out_shape=jax.ShapeDtypeStruct((R, PROJ_W), BF16),
        compiler_params=_cparams(("parallel", "parallel")),
        name="inproj",
    )(u, wa, wb, *tabs)


def _rope_tables():
    pos = np.zeros((LP,), np.float64)
    pos[:SEQ] = np.arange(SEQ) + N_META
    pos[SEQ:SEQ + N_META] = np.arange(N_META)
    half = ROPE_DIM // 2
    inv = ROPE_THETA ** (-np.arange(0, ROPE_DIM, 2, dtype=np.float64) / ROPE_DIM)
    ang = pos[:, None] * inv[None, :]
    cos64 = np.ones((LP, D_QK_DIM)); sin64 = np.zeros((LP, D_QK_DIM))
    cos64[:, :half] = np.cos(ang); cos64[:, half:ROPE_DIM] = np.cos(ang)
    sin64[:, :half] = -np.sin(ang); sin64[:, half:ROPE_DIM] = np.sin(ang)
    cos = np.concatenate([cos64, cos64], axis=1)
    sin = np.concatenate([sin64, sin64], axis=1)
    qs = D_QK_DIM ** -0.5 * math.log2(math.e)
    return [jnp.asarray(t, F32) for t in (cos * qs, sin * qs, cos, sin)]


def _lane_scan(x, op, ident, reverse):
    lane = lax.broadcasted_iota(jnp.int32, x.shape, 1)
    k = 1
    while k < BLK:
        if reverse:
            sh = pltpu.roll(x, BLK - k, 1)
            x = op(x, jnp.where(lane < BLK - k, sh, ident))
        else:
            sh = pltpu.roll(x, k, 1)
            x = op(x, jnp.where(lane >= k, sh, ident))
        k *= 2
    return x


def _log_sigmoid(x):
    return jnp.minimum(x, 0.0) - jnp.log1p(jnp.exp(-jnp.abs(x)))


def _scan_kernel(g_ref, arow_ref, gcol_ref, colt_ref):
    colt_ref[...] = jnp.zeros_like(colt_ref)
    lane = lax.broadcasted_iota(jnp.int32, (8, BLK), 1)
    for d in range(2):
        reverse = d == 1
        carry_b = jnp.zeros((8, BLK), F32)
        carry_m = jnp.full((8, BLK), NEG, F32)
        logical = [META_BLK] + list(range(META_BLK))
        order = logical[::-1] if reverse else logical
        edge = 0 if reverse else BLK - 1
        for j in order:
            cols = slice(j * BLK, (j + 1) * BLK)
            li = g_ref[d * 8:(d + 1) * 8, cols]
            lf = pltpu.roll(_log_sigmoid(li), M_HEADS, 0)
            if j == META_BLK:
                valid = lane < N_META
                li = jnp.where(valid, li, NEG)
                lf = jnp.where(valid, lf, 0.0)
            bc = carry_b + _lane_scan(lf, jnp.add, 0.0, reverse)
            a = li - bc
            m = jnp.maximum(carry_m, _lane_scan(a, jnp.maximum, NEG, reverse))
            m_last = jnp.broadcast_to(m[:, edge:edge + 1], (8, BLK))
            arow_ref[d * 8:(d + 1) * 8, cols] = a
            base = d * 40
            colt_ref[j, base:base + 8, :] = a
            colt_ref[j, base + 8:base + 16, :] = m
            colt_ref[j, base + 16:base + 24, :] = bc + m
            colt_ref[j, base + 24:base + 32, :] = carry_m
            colt_ref[j, base + 32:base + 40, :] = m_last
            carry_b = jnp.broadcast_to(bc[:, edge:edge + 1], (8, BLK))
            carry_m = m_last
    for j in range(NBLK):
        gcol_ref[j * BLK:(j + 1) * BLK, :] = colt_ref[j].T


def _scans(gt):
    return pl.pallas_call(
        _scan_kernel,
        grid=(BATCH,),
        in_specs=[pl.BlockSpec((GROWS, LP), lambda b: (0, b))],
        out_specs=[
            pl.BlockSpec((None, 16, LP), lambda b: (b, 0, 0)),
            pl.BlockSpec((None, LP, BLK), lambda b: (b, 0, 0)),
        ],
        out_shape=[
            jax.ShapeDtypeStruct((BATCH, 16, LP), F32),
            jax.ShapeDtypeStruct((BATCH, LP, BLK), F32),
        ],
        scratch_shapes=[pltpu.VMEM((NBLK, BLK, BLK), F32)],
        compiler_params=_cparams(("parallel",)),
        name="gate_scans",
    )(gt)


TC_COLS = 256
CONV_CH = 256


def _conv3_rows(s_ref, w, lo, n):
    x = s_ref[lo:lo + n, :]
    prev = jnp.concatenate([s_ref[lo - 1:lo + 7, :], pltpu.roll(x, 1, 0)[8:]], axis=0)
    nxt = jnp.concatenate([pltpu.roll(x, n - 1, 0)[:n - 8], s_ref[lo + n - 7:lo + n + 1, :]], axis=0)
    return prev * w[0:1] + x * w[1:2] + nxt * w[2:3]


def _qkconv_kernel(x_ref, w_ref, sc_ref, o_ref, s_ref):
    xo = 8 + N_META
    s_ref[0:8, :] = jnp.zeros((8, TC_COLS), F32)
    s_ref[xo + SEQ:xo + SEQ + 8, :] = jnp.zeros((8, TC_COLS), F32)
    s_ref[8:xo, :] = x_ref[SEQ:SEQ + N_META, :].astype(F32)
    for i in range(SEQ // CONV_CH):
        s_ref[xo + i * CONV_CH:xo + (i + 1) * CONV_CH, :] = x_ref[i * CONV_CH:(i + 1) * CONV_CH, :].astype(F32)
    w = 0.5 * w_ref[...]
    sc = sc_ref[...]

    def act(yh):
        return (_silu_of_half(yh) * sc).astype(o_ref.dtype)

    for i in range(SEQ // CONV_CH):
        o_ref[i * CONV_CH:(i + 1) * CONV_CH, :] = act(_conv3_rows(s_ref, w, xo + i * CONV_CH, CONV_CH))
    o_ref[SEQ:SEQ + N_META, :] = act(_conv3_rows(s_ref, w, 8, N_META))
    o_ref[SEQ + N_META:LP, :] = jnp.zeros((LP - SEQ - N_META, TC_COLS), o_ref.dtype)


def _qkconv(proj3, w, scale):
    return pl.pallas_call(
        _qkconv_kernel,
        grid=(BATCH, 2 * M_WIDTH // TC_COLS),
        in_specs=[
            pl.BlockSpec((None, LP, TC_COLS), lambda b, c: (b, 0, C_MQK // TC_COLS + c)),
            pl.BlockSpec((3, TC_COLS), lambda b, c: (0, c)),
            pl.BlockSpec((1, TC_COLS), lambda b, c: (0, c)),
        ],
        out_specs=pl.BlockSpec((None, LP, TC_COLS), lambda b, c: (b, 0, c)),
        out_shape=jax.ShapeDtypeStruct((BATCH, LP, 2 * M_WIDTH), BF16),
        scratch_shapes=[pltpu.VMEM((SEQ + N_META + 16, TC_COLS), F32)],
        compiler_params=_cparams(("parallel", "parallel")),
        name="qk_conv",
    )(proj3, w, scale)


def _mlstm_kernel(qf, kf, vf, gcf, arf, qb, kb, vb, gcb, arb, hf_ref, hb_ref, c_ref):
    c = pl.program_id(1)

    @pl.when(c == 0)
    def _():
        c_ref[...] = jnp.zeros_like(c_ref)

    row = lax.broadcasted_iota(jnp.int32, (BLK, BLK), 0)
    col = lax.broadcasted_iota(jnp.int32, (BLK, BLK), 1)
    ones_b = jnp.ones((BLK, 128), BF16)
    ones_f = jnp.ones((BLK, 128), F32)
    dirs = ((qf, kf, vf, gcf, arf, hf_ref), (qb, kb, vb, gcb, arb, hb_ref))
    for d, (q_ref, k_ref, v_ref, gc_ref, ar_ref, o_ref) in enumerate(dirs):
        mask = (col <= row) if d == 0 else (col >= row)
        gc = gc_ref[...]
        for h in range(M_HEADS):
            st = d * M_HEADS + h
            def colq(qi):
                j = d * 40 + qi * 8 + h
                return gc[:, j:j + 1]
            a_col, m_col, e_col, mp_col, ml_col = (colq(i) for i in range(5))
            a_row = ar_ref[d * 8 + h:d * 8 + h + 1, :]
            hs = slice(h * M_HEAD_DIM, (h + 1) * M_HEAD_DIM)
            q = q_ref[:, hs]
            k = k_ref[:, hs]
            v = v_ref[:, hs]
            s = lax.dot_general(q, k, (((1,), (1,)), ((), ())), preferred_element_type=F32)
            p = jnp.where(mask, jnp.exp(a_row - m_col), 0.0)
            sp = (s * p).astype(BF16)
            a = jnp.exp(mp_col - m_col)
            cn = c_ref[st]
            tot = (jnp.dot(sp, jnp.concatenate([v, ones_b], axis=1), preferred_element_type=F32)
                   + a * jnp.dot(q, cn.astype(BF16), preferred_element_type=F32))
            r = 1.0 / jnp.maximum(jnp.abs(tot[:, M_HEAD_DIM:]), jnp.exp(-e_col))
            o_ref[:, hs] = tot[:, :M_HEAD_DIM] * jnp.concatenate([r, r], axis=1)
            w = jnp.exp(a_col - ml_col)
            dec = jnp.exp(mp_col[0:1, :] - ml_col[0:1, :])
            wvn = (w * jnp.concatenate([v.astype(F32), ones_f], axis=1)).astype(BF16)
            c_ref[st] = dec * cn + lax.dot_general(
                k, wvn, (((0,), (0,)), ((), ())), preferred_element_type=F32)


def _fwd_blk(c):
    return (c + NBLK - 1) % NBLK


def _bwd_blk(c):
    return (2 * NBLK - 2 - c) % NBLK


def _mlstm(qk3, proj3, gcol, arow):
    def specs(blk):
        return [
            pl.BlockSpec((None, BLK, M_WIDTH), lambda b, c: (b, blk(c), 0)),
            pl.BlockSpec((None, BLK, M_WIDTH), lambda b, c: (b, blk(c), 1)),
            pl.BlockSpec((None, BLK, M_WIDTH), lambda b, c: (b, blk(c), C_MV // M_WIDTH)),
            pl.BlockSpec((None, BLK, BLK), lambda b, c: (b, blk(c), 0)),
            pl.BlockSpec((None, 16, BLK), lambda b, c: (b, 0, blk(c))),
        ]
    out_sd = jax.ShapeDtypeStruct((BATCH, LP, M_WIDTH), F32)
    return pl.pallas_call(
        _mlstm_kernel,
        grid=(BATCH, NBLK),
        in_specs=specs(_fwd_blk) + specs(_bwd_blk),
        out_specs=[
            pl.BlockSpec((None, BLK, M_WIDTH), lambda b, c: (b, _fwd_blk(c), 0)),
            pl.BlockSpec((None, BLK, M_WIDTH), lambda b, c: (b, _bwd_blk(c), 0)),
        ],
        out_shape=[out_sd, out_sd],
        scratch_shapes=[pltpu.VMEM((2 * M_HEADS, M_HEAD_DIM, M_HEAD_DIM + 128), F32)],
        compiler_params=_cparams(("parallel", "arbitrary")),
        name="mlstm",
    )(qk3, qk3, proj3, gcol, arow, qk3, qk3, proj3, gcol, arow)


RQ = 128
NGRP = LP // RQ
EXP_CH = 512
ATTN_NBUF = 2


def _attn_kernel(lam_ref, q_ref, k_ref, v_ref, g_ref, o_ref, *scratch):
    s_refs, p_refs, l_refs, m_refs = (scratch[i * ATTN_NBUF:(i + 1) * ATTN_NBUF] for i in range(4))
    lq = lam_ref[...]
    la = jnp.sum(lq[0:1] * lq[1:2], axis=1, keepdims=True)
    lb = jnp.sum(lq[2:3] * lq[3:4], axis=1, keepdims=True)
    lam = jnp.exp(la) - jnp.exp(lb) + LAM_INIT

    lane = lax.broadcasted_iota(jnp.int32, (RQ, 128), 1)
    colid = lax.broadcasted_iota(jnp.int32, (1, BLK), 1)

    def rows(g):
        return pl.ds(pl.multiple_of(g * RQ, RQ), RQ)

    def stage_scores(g, s_ref, m_ref):
        qg = q_ref[rows(g), :]
        zero = jnp.zeros_like(qg)
        q2 = jnp.concatenate([jnp.where(lane < D_QK_DIM, qg, zero), jnp.where(lane >= D_QK_DIM, qg, zero)], axis=0)
        s = lax.dot_general(q2, k_ref[...], (((1,), (1,)), ((), ())), preferred_element_type=F32)
        tail = jnp.where(colid < N_META, s[:, LP - BLK:], -jnp.inf)
        s_ref[:, :LP - BLK] = s[:, :LP - BLK]
        s_ref[:, LP - BLK:] = tail
        m = jnp.maximum(jnp.max(s[:, :LP - BLK], axis=1, keepdims=True), jnp.max(tail, axis=1, keepdims=True))
        m_ref[...] = jnp.broadcast_to(m, m_ref.shape)

    def stage_exp(s_ref, m_ref, p_ref, l_ref):
        m = m_ref[:, 0:1]
        part = jnp.zeros((2 * RQ, 128), F32)
        for c0 in range(0, LP, EXP_CH):
            c1 = min(c0 + EXP_CH, LP)
            pc = jnp.exp2(s_ref[:, c0:c1] - m)
            for j in range(0, c1 - c0, 128):
                part = part + pc[:, j:j + 128]
            p_ref[:, c0:c1] = pc.astype(BF16)
        l_ref[...] = jnp.broadcast_to(jnp.sum(part, axis=1, keepdims=True), l_ref.shape)

    def stage_pv(g, p_ref, l_ref):
        l = l_ref[:, 0:1]
        ratio = (lam * l[:RQ] / l[RQ:]).astype(BF16)
        a = p_ref[:RQ, :] - ratio * p_ref[RQ:, :]
        o = jnp.dot(a, v_ref[...], preferred_element_type=F32) / l[:RQ]
        r = lax.rsqrt(jnp.mean(o * o, axis=-1, keepdims=True) + EPS)
        o_ref[rows(g), :] = (((o * r) * g_ref[...]) * (1.0 - LAM_INIT)).astype(o_ref.dtype)

    nb = len(s_refs)
    sbuf = tuple(zip(s_refs, m_refs))
    pbuf = tuple(zip(p_refs, l_refs))

    def tick(t, static):
        if static < NGRP:
            stage_scores(t, *sbuf[static % nb])
        if 1 <= static <= NGRP:
            stage_exp(*sbuf[(static - 1) % nb], *pbuf[(static - 1) % nb])
        if 2 <= static <= NGRP + 1:
            stage_pv(t - 2, *pbuf[(static - 2) % nb])

    tick(0, 0)
    tick(1, 1)
    n_rounds = (NGRP - 2) // nb

    def body(i, carry):
        t = nb * i + 2
        for j in range(nb):
            tick(t + j, 2 + j)
        return carry

    lax.fori_loop(0, n_rounds, body, 0)
    for t in range(2 + nb * n_rounds, NGRP + 2):
        tick(t, t)


def _attention(lam4, proj3, subln_g):
    return pl.pallas_call(
        _attn_kernel,
        grid=(BATCH, D_HEADS),
        in_specs=[
            pl.BlockSpec((4, D_QK_DIM), lambda b, h: (0, 0)),
            pl.BlockSpec((None, LP, 128), lambda b, h: (b, 0, C_DQ // 128 + h)),
            pl.BlockSpec((None, LP, 128), lambda b, h: (b, 0, C_DK // 128 + h)),
            pl.BlockSpec((None, LP, 128), lambda b, h: (b, 0, C_DV // 128 + h)),
            pl.BlockSpec((1, D_V_DIM), lambda b, h: (0, 0)),
        ],
        out_specs=pl.BlockSpec((None, LP, D_V_DIM), lambda b, h: (b, 0, h)),
        out_shape=jax.ShapeDtypeStruct((BATCH, LP, D_WIDTH), BF16),
        scratch_shapes=(
            [pltpu.VMEM((2 * RQ, LP), F32)] * ATTN_NBUF
            + [pltpu.VMEM((2 * RQ, LP), BF16)] * ATTN_NBUF
            + [pltpu.VMEM((2 * RQ, 128), F32)] * ATTN_NBUF
            + [pltpu.VMEM((2 * RQ, 128), F32)] * ATTN_NBUF
        ),
        compiler_params=_cparams(("parallel", "parallel")),
        name="diff_attn",
    )(lam4, proj3, proj3, proj3, subln_g)


TM4 = 1056
TN4 = 512
LN_CH = 96


def _merge_kernel(hf_ref, hb_ref, mo_ref, ng_ref, bo_ref, wa_ref, wb_ref, gm_ref, gd_ref, o_ref, a_ref):
    n = pl.program_id(1)

    @pl.when(n == 0)
    def _():
        def body(i, carry):
            rows = pl.ds(pl.multiple_of(i * LN_CH, LN_CH), LN_CH)
            for h in range(M_HEADS):
                hs = slice(h * M_HEAD_DIM, (h + 1) * M_HEAD_DIM)
                x = hf_ref[rows, hs] + hb_ref[rows, hs]
                mu = jnp.mean(x, axis=-1, keepdims=True)
                xc = x - mu
                var = jnp.mean(xc * xc, axis=-1, keepdims=True)
                hn = (xc * lax.rsqrt(var + EPS)) * ng_ref[:, hs]
                a_ref[rows, hs] = (_twice_sigmoid_of_half(mo_ref[rows, hs].astype(F32)) * hn).astype(BF16)
            return carry
        lax.fori_loop(0, TM4 // LN_CH, body, 0)

    ya = jnp.dot(a_ref[...], wa_ref[...], preferred_element_type=F32)
    yb = jnp.dot(bo_ref[...], wb_ref[...], preferred_element_type=F32)
    o_ref[...] = (_twice_sigmoid_of_half(gm_ref[...].astype(F32)) * ya
                  + _twice_sigmoid_of_half(gd_ref[...].astype(F32)) * yb).astype(o_ref.dtype)


def _merge(hf, hb, proj, norm_g, b_out, wa, wb):
    return pl.pallas_call(
        _merge_kernel,
        grid=(R // TM4, D_MODEL // TN4),
        in_specs=[
            pl.BlockSpec((TM4, M_WIDTH), lambda m, n: (m, 0)),
            pl.BlockSpec((TM4, M_WIDTH), lambda m, n: (m, 0)),
            pl.BlockSpec((TM4, M_WIDTH), lambda m, n: (m, C_MO // M_WIDTH)),
            pl.BlockSpec((1, M_WIDTH), lambda m, n: (0, 0)),
            pl.BlockSpec((TM4, D_WIDTH), lambda m, n: (m, 0)),
            pl.BlockSpec((M_WIDTH, TN4), lambda m, n: (0, n)),
            pl.BlockSpec((D_WIDTH, TN4), lambda m, n: (0, n)),
            pl.BlockSpec((TM4, TN4), lambda m, n: (m, C_GM // TN4 + n)),
            pl.BlockSpec((TM4, TN4), lambda m, n: (m, C_GD // TN4 + n)),
        ],
        out_specs=pl.BlockSpec((TM4, TN4), lambda m, n: (m, n)),
        out_shape=jax.ShapeDtypeStruct((R, D_MODEL), BF16),
        scratch_shapes=[pltpu.VMEM((TM4, M_WIDTH), BF16)],
        compiler_params=_cparams(("parallel", "arbitrary")),
        name="merge",
    )(hf, hb, proj, norm_g, b_out, wa, wb, proj, proj)


def _outproj_kernel(x_ref, t_ref, mg_ref, w_ref, g2_ref, h1_ref, u2_ref):
    def run(h0_ref):
        h1 = h0_ref[...] + jnp.dot(mg_ref[...], w_ref[...], preferred_element_type=F32)
        h1_ref[...] = h1
        r = lax.rsqrt(jnp.mean(h1 * h1, axis=-1, keepdims=True) + EPS)
        u2_ref[...] = ((h1 * r) * g2_ref[...]).astype(BF16)

    pl.when(pl.program_id(1) < NT0 - 1)(lambda: run(x_ref))
    pl.when(pl.program_id(1) == NT0 - 1)(lambda: run(t_ref))


def _outproj(x, tail, merged3, w, g2):
    row_spec = pl.BlockSpec((None, TM0, D_MODEL), lambda b, j: (b, j, 0))
    return pl.pallas_call(
        _outproj_kernel,
        grid=(BATCH, NT0),
        in_specs=_x_tile_specs() + [
            row_spec,
            pl.BlockSpec((D_MODEL, D_MODEL), lambda b, j: (0, 0)),
            pl.BlockSpec((1, D_MODEL), lambda b, j: (0, 0)),
        ],
        out_specs=[row_spec, row_spec],
        out_shape=[
            jax.ShapeDtypeStruct((BATCH, LP, D_MODEL), F32),
            jax.ShapeDtypeStruct((BATCH, LP, D_MODEL), BF16),
        ],
        compiler_params=_cparams(("parallel", "parallel")),
        name="outproj",
    )(x, tail, merged3, w, g2)


TMU = 1024
TNU = 512
HALO = 16
NNU = FFN_DIM // TNU


def _ffn_up_kernel(x_ref, xp_ref, xn_ref, wg_ref, wv_ref, cg_ref, cv_ref, o_ref, xs_ref, sg_ref, sv_ref):
    @pl.when(pl.program_id(2) == 0)
    def _():
        last_tile = pl.program_id(1) == SEQ // TMU - 1
        xs_ref[0:HALO, :] = xp_ref[...]
        xs_ref[HALO:HALO + TMU, :] = x_ref[...]
        xn = xn_ref[...]
        xs_ref[HALO + TMU:, :] = jnp.where(last_tile, jnp.zeros_like(xn), xn)

    def branch(w_ref, c, s_ref):
        s_ref[...] = jnp.dot(xs_ref[...], w_ref[...].astype(BF16), preferred_element_type=F32)
        return _conv3_rows(s_ref, c, HALO, TMU)

    gate_half = branch(wg_ref, 0.5 * cg_ref[...], sg_ref)
    val = branch(wv_ref, cv_ref[...], sv_ref)
    o_ref[...] = (_silu_of_half(gate_half) * val).astype(o_ref.dtype)


def _ffn_up(u2_3, w_up, conv_w):
    hb = TMU // HALO
    meta_hblk = SEQ // HALO
    return pl.pallas_call(
        _ffn_up_kernel,
        grid=(BATCH, SEQ // TMU, NNU),
        in_specs=[
            pl.BlockSpec((None, TMU, D_MODEL), lambda b, i, n: (b, i, 0)),
            pl.BlockSpec((None, HALO, D_MODEL), lambda b, i, n: (b, jnp.where(i == 0, meta_hblk, i * hb - 1), 0)),
            pl.BlockSpec((None, HALO, D_MODEL), lambda b, i, n: (b, (i + 1) * hb, 0)),
            pl.BlockSpec((D_MODEL, TNU), lambda b, i, n: (0, n)),
            pl.BlockSpec((D_MODEL, TNU), lambda b, i, n: (0, NNU + n)),
            pl.BlockSpec((3, TNU), lambda b, i, n: (0, n)),
            pl.BlockSpec((3, TNU), lambda b, i, n: (0, NNU + n)),
        ],
        out_specs=pl.BlockSpec((None, TMU, TNU), lambda b, i, n: (b, i, n)),
        out_shape=jax.ShapeDtypeStruct((BATCH, SEQ, FFN_DIM), BF16),
        scratch_shapes=[
            pltpu.VMEM((TMU + 2 * HALO, D_MODEL), BF16),
            pltpu.VMEM((TMU + 2 * HALO, TNU), F32),
            pltpu.VMEM((TMU + 2 * HALO, TNU), F32),
        ],
        compiler_params=_cparams(("parallel", "parallel", "arbitrary")),
        name="ffn_up",
    )(u2_3, u2_3, u2_3, w_up, w_up, conv_w, conv_w)


TM7 = 1024
TK7 = 512
NK7 = FFN_DIM // TK7


def _down_kernel(a_ref, wd_ref, h1_ref, gf_ref, o_ref, acc_ref):
    k = pl.program_id(2)

    def part():
        return jnp.dot(a_ref[...], wd_ref[...], preferred_element_type=F32)

    @pl.when(k == 0)
    def _():
        acc_ref[...] = part()

    @pl.when(jnp.logical_and(k > 0, k < NK7 - 1))
    def _():
        acc_ref[...] += part()

    @pl.when(k == NK7 - 1)
    def _():
        h2 = h1_ref[...] + (acc_ref[...] + part())
        r = lax.rsqrt(jnp.mean(h2 * h2, axis=-1, keepdims=True) + EPS)
        o_ref[...] = (h2 * r) * gf_ref[...]


def _ffn_down(act, w_down, h1_3, gf):
    return pl.pallas_call(
        _down_kernel,
        grid=(BATCH, SEQ // TM7, NK7),
        in_specs=[
            pl.BlockSpec((None, TM7, TK7), lambda b, i, k: (b, i, k)),
            pl.BlockSpec((TK7, D_MODEL), lambda b, i, k: (k, 0)),
            pl.BlockSpec((None, TM7, D_MODEL), lambda b, i, k: (b, i, 0)),
            pl.BlockSpec((1, D_MODEL), lambda b, i, k: (0, 0)),
        ],
        out_specs=pl.BlockSpec((None, TM7, D_MODEL), lambda b, i, k: (b, i, 0)),
        out_shape=jax.ShapeDtypeStruct((BATCH, SEQ, D_MODEL), F32),
        scratch_shapes=[pltpu.VMEM((TM7, D_MODEL), F32)],
        compiler_params=_cparams(("parallel", "parallel", "arbitrary")),
        name="ffn_down",
    )(act, w_down, h1_3, gf)


def kernel(x, meta_tokens, norm1_g, w_in, mlstm_conv_w, mlstm_gate_bias, mlstm_norm_g, lambda_q1, lambda_k1,
           lambda_q2, lambda_k2, diff_subln_g, w_branch_m, w_branch_d, w_out, norm2_g, w_up, ffn_conv_w,
           w_down, norm_f_g):
    assert x.shape == (BATCH, SEQ, D_MODEL) and w_in.shape[0] == 1
    tail = _tail_tile(x, meta_tokens)

    wit = w_in[0].T
    halved = np.ones((wit.shape[0], 1), np.float32)
    for c0, width in ((C_MO, M_WIDTH), (C_GM + GROWS, D_MODEL), (C_GD + GROWS, D_MODEL)):
        halved[c0:c0 + width] = 0.5
    w_pa = w_pb = (wit * halved).astype(BF16)
    gbias = mlstm_gate_bias[0].reshape(GROWS, 1).astype(F32)

    u, gt = _prenorm(x, tail, norm1_g[0].reshape(1, D_MODEL), wit, gbias)
    proj = _inproj(u.reshape(R, D_MODEL), w_pa, w_pb, _rope_tables())
    proj3 = proj.reshape(BATCH, LP, PROJ_W)

    arow, gcol = _scans(gt)

    kscale = jnp.concatenate([jnp.ones((1, M_WIDTH), F32), jnp.full((1, M_WIDTH), M_HEAD_DIM ** -0.5, F32)], axis=1)
    qk3 = _qkconv(proj3, mlstm_conv_w[0], kscale)
    hf, hb = _mlstm(qk3, proj3, gcol, arow)

    lam4 = jnp.stack([lambda_q1[0], lambda_k1[0], lambda_q2[0], lambda_k2[0]]).astype(F32)
    b_out = _attention(lam4, proj3, diff_subln_g[0].reshape(1, D_V_DIM))

    merged = _merge(hf.reshape(R, M_WIDTH), hb.reshape(R, M_WIDTH), proj,
                    (0.5 * mlstm_norm_g[0]).reshape(1, M_WIDTH), b_out.reshape(R, D_WIDTH),
                    (0.5 * w_branch_m[0]).astype(BF16), (0.5 * w_branch_d[0]).astype(BF16))
    h1, u2 = _outproj(x, tail, merged.reshape(BATCH, LP, D_MODEL), w_out[0].astype(BF16),
                      norm2_g[0].reshape(1, D_MODEL))

    act = _ffn_up(u2, w_up[0], ffn_conv_w[0])
    return _ffn_down(act, w_down[0].astype(BF16), h1, norm_f_g.reshape(1, D_MODEL))
```

```python
import math

import numpy as np
import jax
import jax.numpy as jnp
from jax import lax
from jax.experimental import pallas as pl
from jax.experimental.pallas import tpu as pltpu

F32 = jnp.float32
BF16 = jnp.bfloat16

D_MODEL = 2048
BATCH = 2
SEQ = 4096
N_META = 16
EPS = 1e-6
M_HEADS = 4
M_WIDTH = 1024
M_HEAD_DIM = 256
D_HEADS = 8
D_V_DIM = 128
D_QK_DIM = 64
D_WIDTH = 1024
ROPE_THETA = 500000.0
ROPE_DIM = 16
FFN_DIM = 5632
LAM_INIT = 0.8 - 0.6 * math.exp(-0.3 * 0)

BLK = 128
LP = SEQ + BLK
NBLK = LP // BLK
META_BLK = SEQ // BLK
R = BATCH * LP
NEG = -1e30

C_MQK, C_MV, C_MO, C_DQ, C_DK, C_DV, C_GM, C_GD = 0, 2048, 3072, 4096, 5120, 6144, 7168, 9216
PROJ_W = 11264

V7X_VMEM_BYTES = 64 * 1024 * 1024
VMEM_LIMIT = V7X_VMEM_BYTES - 8 * 1024 * 1024


def _cparams(sem):
    return pltpu.CompilerParams(dimension_semantics=sem, vmem_limit_bytes=VMEM_LIMIT)


def _twice_sigmoid_of_half(xh):
    return jnp.tanh(xh) + 1.0


def _silu_of_half(yh):
    return yh * (jnp.tanh(yh) + 1.0)


TM0 = 384
NT0 = LP // TM0
X_TAIL0 = (NT0 - 1) * TM0
assert X_TAIL0 <= SEQ < X_TAIL0 + TM0


def _tail_tile(x, meta_tokens):
    meta = jnp.broadcast_to(meta_tokens[None].astype(x.dtype), (BATCH, N_META, D_MODEL))
    pad = jnp.zeros((BATCH, LP - SEQ - N_META, D_MODEL), x.dtype)
    return jnp.concatenate([x[:, X_TAIL0:], meta, pad], axis=1)


def _x_tile_specs():
    return [
        pl.BlockSpec((None, TM0, D_MODEL), lambda b, j, *_: (b, jnp.minimum(j, NT0 - 2), 0)),
        pl.BlockSpec((None, TM0, D_MODEL), lambda b, j, *_: (b, 0, 0)),
    ]


GROWS = 16
GATE_COL0 = 4 * M_WIDTH


def _prenorm_kernel(x_ref, t_ref, g_ref, wg_ref, gb_ref, u_ref, gt_ref):
    def run(src_ref):
        x = src_ref[...]
        r = lax.rsqrt(jnp.mean(x * x, axis=-1, keepdims=True) + EPS)
        u = ((x * r) * g_ref[...]).astype(BF16)
        u_ref[...] = u
        gt = lax.dot_general(wg_ref[...].astype(BF16), u, (((1,), (1,)), ((), ())), preferred_element_type=F32)
        gt_ref[...] = gt + gb_ref[...]

    pl.when(pl.program_id(1) < NT0 - 1)(lambda: run(x_ref))
    pl.when(pl.program_id(1) == NT0 - 1)(lambda: run(t_ref))


def _prenorm(x, tail, g, w_in_t, gbias):
    return pl.pallas_call(
        _prenorm_kernel,
        grid=(BATCH, NT0),
        in_specs=_x_tile_specs() + [
            pl.BlockSpec((1, D_MODEL), lambda b, j: (0, 0)),
            pl.BlockSpec((GROWS, D_MODEL), lambda b, j: (GATE_COL0 // GROWS, 0)),
            pl.BlockSpec((GROWS, 1), lambda b, j: (0, 0)),
        ],
        out_specs=[
            pl.BlockSpec((None, TM0, D_MODEL), lambda b, j: (b, j, 0)),
            pl.BlockSpec((GROWS, TM0), lambda b, j: (0, b * NT0 + j)),
        ],
        out_shape=[
            jax.ShapeDtypeStruct((BATCH, LP, D_MODEL), BF16),
            jax.ShapeDtypeStruct((GROWS, R), F32),
        ],
        compiler_params=_cparams(("parallel", "parallel")),
        name="prenorm",
    )(x, tail, g, w_in_t, gbias)


TM1 = 1408
TN1 = 1024


def _rope_tile(acc, c, s):
    lane = lax.broadcasted_iota(jnp.int32, c.shape, 1)
    first = (lane % D_QK_DIM) < (ROPE_DIM // 2)
    outs = []
    for g in range(acc.shape[1] // 128):
        y = acc[:, g * 128:(g + 1) * 128]
        rot = jnp.where(first, pltpu.roll(y, 128 - ROPE_DIM // 2, 1), pltpu.roll(y, ROPE_DIM // 2, 1))
        outs.append(y * c + rot * s)
    return jnp.concatenate(outs, axis=1)


NA1 = C_DQ // TN1


def _inproj_kernel(u_ref, wa_ref, wb_ref, cq_ref, sq_ref, ck_ref, sk_ref, o_ref):
    n = pl.program_id(1)

    def mm(wt_ref):
        return lax.dot_general(u_ref[...], wt_ref[...], (((1,), (1,)), ((), ())), preferred_element_type=F32)

    q0, k0, v0 = C_DQ // TN1, C_DK // TN1, C_DV // TN1

    @pl.when(n < NA1)
    def _():
        o_ref[...] = mm(wa_ref).astype(o_ref.dtype)

    @pl.when(jnp.logical_and(n >= q0, n < k0))
    def _():
        o_ref[...] = _rope_tile(mm(wb_ref), cq_ref[...], sq_ref[...]).astype(o_ref.dtype)

    @pl.when(jnp.logical_and(n >= k0, n < v0))
    def _():
        o_ref[...] = _rope_tile(mm(wb_ref), ck_ref[...], sk_ref[...]).astype(o_ref.dtype)

    @pl.when(n >= v0)
    def _():
        o_ref[...] = mm(wb_ref).astype(o_ref.dtype)


def _inproj(u, wa, wb, tabs):
    tiles_per_batch = LP // TM1
    tab_spec = pl.BlockSpec((TM1, 128), lambda m, n: (m % tiles_per_batch, 0))
    return pl.pallas_call(
        _inproj_kernel,
        grid=(R // TM1, PROJ_W // TN1),
        in_specs=[
            pl.BlockSpec((TM1, D_MODEL), lambda m, n: (m, 0)),
            pl.BlockSpec((TN1, D_MODEL), lambda m, n: (jnp.minimum(n, NA1 - 1), 0)),
            pl.BlockSpec((pl.Element(TN1), pl.Element(D_MODEL)),
                         lambda m, n: (pl.multiple_of(GATE_COL0 + GROWS + jnp.maximum(n - NA1, 0) * TN1, GROWS), 0)),
            tab_spec, tab_spec, tab_spec, tab_spec,
        ],
        out_specs=pl.BlockSpec((TM1, TN1), lambda m, n: (m, n)),
        out_shape=jax.ShapeDtypeStruct((R, PROJ_W), BF16),
        compiler_params=_cparams(("parallel", "parallel")),
        name="inproj",
    )(u, wa, wb, *tabs)


def _rope_tables():
    pos = np.zeros((LP,), np.float64)
    pos[:SEQ] = np.arange(SEQ) + N_META
    pos[SEQ:SEQ + N_META] = np.arange(N_META)
    half = ROPE_DIM // 2
    inv = ROPE_THETA ** (-np.arange(0, ROPE_DIM, 2, dtype=np.float64) / ROPE_DIM)
    ang = pos[:, None] * inv[None, :]
    cos64 = np.ones((LP, D_QK_DIM)); sin64 = np.zeros((LP, D_QK_DIM))
    cos64[:, :half] = np.cos(ang); cos64[:, half:ROPE_DIM] = np.cos(ang)
    sin64[:, :half] = -np.sin(ang); sin64[:, half:ROPE_DIM] = np.sin(ang)
    cos = np.concatenate([cos64, cos64], axis=1)
    sin = np.concatenate([sin64, sin64], axis=1)
    qs = D_QK_DIM ** -0.5 * math.log2(math.e)
    return [jnp.asarray(t, F32) for t in (cos * qs, sin * qs, cos, sin)]


def _lane_scan(x, op, ident, reverse):
    lane = lax.broadcasted_iota(jnp.int32, x.shape, 1)
    k = 1
    while k < BLK:
        if reverse:
            sh = pltpu.roll(x, BLK - k, 1)
            x = op(x, jnp.where(lane < BLK - k, sh, ident))
        else:
            sh = pltpu.roll(x, k, 1)
            x = op(x, jnp.where(lane >= k, sh, ident))
        k *= 2
    return x


def _log_sigmoid(x):
    return jnp.minimum(x, 0.0) - jnp.log1p(jnp.exp(-jnp.abs(x)))


def _scan_kernel(g_ref, arow_ref, gcol_ref, colt_ref):
    colt_ref[...] = jnp.zeros_like(colt_ref)
    lane = lax.broadcasted_iota(jnp.int32, (8, BLK), 1)
    for d in range(2):
        reverse = d == 1
        carry_b = jnp.zeros((8, BLK), F32)
        carry_m = jnp.full((8, BLK), NEG, F32)
        logical = [META_BLK] + list(range(META_BLK))
        order = logical[::-1] if reverse else logical
        edge = 0 if reverse else BLK - 1
        for j in order:
            cols = slice(j * BLK, (j + 1) * BLK)
            li = g_ref[d * 8:(d + 1) * 8, cols]
            lf = pltpu.roll(_log_sigmoid(li), M_HEADS, 0)
            if j == META_BLK:
                valid = lane < N_META
                li = jnp.where(valid, li, NEG)
                lf = jnp.where(valid, lf, 0.0)
            bc = carry_b + _lane_scan(lf, jnp.add, 0.0, reverse)
            a = li - bc
            m = jnp.maximum(carry_m, _lane_scan(a, jnp.maximum, NEG, reverse))
            m_last = jnp.broadcast_to(m[:, edge:edge + 1], (8, BLK))
            arow_ref[d * 8:(d + 1) * 8, cols] = a
            base = d * 40
            colt_ref[j, base:base + 8, :] = a
            colt_ref[j, base + 8:base + 16, :] = m
            colt_ref[j, base + 16:base + 24, :] = bc + m
            colt_ref[j, base + 24:base + 32, :] = carry_m
            colt_ref[j, base + 32:base + 40, :] = m_last
            carry_b = jnp.broadcast_to(bc[:, edge:edge + 1], (8, BLK))
            carry_m = m_last
    for j in range(NBLK):
        gcol_ref[j * BLK:(j + 1) * BLK, :] = colt_ref[j].T


def _scans(gt):
    return pl.pallas_call(
        _scan_kernel,
        grid=(BATCH,),
        in_specs=[pl.BlockSpec((GROWS, LP), lambda b: (0, b))],
        out_specs=[
            pl.BlockSpec((None, 16, LP), lambda b: (b, 0, 0)),
            pl.BlockSpec((None, LP, BLK), lambda b: (b, 0, 0)),
        ],
        out_shape=[
            jax.ShapeDtypeStruct((BATCH, 16, LP), F32),
            jax.ShapeDtypeStruct((BATCH, LP, BLK), F32),
        ],
        scratch_shapes=[pltpu.VMEM((NBLK, BLK, BLK), F32)],
        compiler_params=_cparams(("parallel",)),
        name="gate_scans",
    )(gt)


TC_COLS = 256
CONV_CH = 256


def _conv3_rows(s_ref, w, lo, n):
    x = s_ref[lo:lo + n, :]
    prev = jnp.concatenate([s_ref[lo - 1:lo + 7, :], pltpu.roll(x, 1, 0)[8:]], axis=0)
    nxt = jnp.concatenate([pltpu.roll(x, n - 1, 0)[:n - 8], s_ref[lo + n - 7:lo + n + 1, :]], axis=0)
    return prev * w[0:1] + x * w[1:2] + nxt * w[2:3]


def _qkconv_kernel(x_ref, w_ref, sc_ref, o_ref, s_ref):
    xo = 8 + N_META
    s_ref[0:8, :] = jnp.zeros((8, TC_COLS), F32)
    s_ref[xo + SEQ:xo + SEQ + 8, :] = jnp.zeros((8, TC_COLS), F32)
    s_ref[8:xo, :] = x_ref[SEQ:SEQ + N_META, :].astype(F32)
    for i in range(SEQ // CONV_CH):
        s_ref[xo + i * CONV_CH:xo + (i + 1) * CONV_CH, :] = x_ref[i * CONV_CH:(i + 1) * CONV_CH, :].astype(F32)
    w = 0.5 * w_ref[...]
    sc = sc_ref[...]

    def act(yh):
        return (_silu_of_half(yh) * sc).astype(o_ref.dtype)

    for i in range(SEQ // CONV_CH):
        o_ref[i * CONV_CH:(i + 1) * CONV_CH, :] = act(_conv3_rows(s_ref, w, xo + i * CONV_CH, CONV_CH))
    o_ref[SEQ:SEQ + N_META, :] = act(_conv3_rows(s_ref, w, 8, N_META))
    o_ref[SEQ + N_META:LP, :] = jnp.zeros((LP - SEQ - N_META, TC_COLS), o_ref.dtype)


def _qkconv(proj3, w, scale):
    return pl.pallas_call(
        _qkconv_kernel,
        grid=(BATCH, 2 * M_WIDTH // TC_COLS),
        in_specs=[
            pl.BlockSpec((None, LP, TC_COLS), lambda b, c: (b, 0, C_MQK // TC_COLS + c)),
            pl.BlockSpec((3, TC_COLS), lambda b, c: (0, c)),
            pl.BlockSpec((1, TC_COLS), lambda b, c: (0, c)),
        ],
        out_specs=pl.BlockSpec((None, LP, TC_COLS), lambda b, c: (b, 0, c)),
        out_shape=jax.ShapeDtypeStruct((BATCH, LP, 2 * M_WIDTH), BF16),
        scratch_shapes=[pltpu.VMEM((SEQ + N_META + 16, TC_COLS), F32)],
        compiler_params=_cparams(("parallel", "parallel")),
        name="qk_conv",
    )(proj3, w, scale)


def _mlstm_kernel(qf, kf, vf, gcf, arf, qb, kb, vb, gcb, arb, hf_ref, hb_ref, c_ref):
    c = pl.program_id(1)

    @pl.when(c == 0)
    def _():
        c_ref[...] = jnp.zeros_like(c_ref)

    row = lax.broadcasted_iota(jnp.int32, (BLK, BLK), 0)
    col = lax.broadcasted_iota(jnp.int32, (BLK, BLK), 1)
    ones_b = jnp.ones((BLK, 128), BF16)
    ones_f = jnp.ones((BLK, 128), F32)
    dirs = ((qf, kf, vf, gcf, arf, hf_ref), (qb, kb, vb, gcb, arb, hb_ref))
    for d, (q_ref, k_ref, v_ref, gc_ref, ar_ref, o_ref) in enumerate(dirs):
        mask = (col <= row) if d == 0 else (col >= row)
        gc = gc_ref[...]
        for h in range(M_HEADS):
            st = d * M_HEADS + h
            def colq(qi):
                j = d * 40 + qi * 8 + h
                return gc[:, j:j + 1]
            a_col, m_col, e_col, mp_col, ml_col = (colq(i) for i in range(5))
            a_row = ar_ref[d * 8 + h:d * 8 + h + 1, :]
            hs = slice(h * M_HEAD_DIM, (h + 1) * M_HEAD_DIM)
            q = q_ref[:, hs]
            k = k_ref[:, hs]
            v = v_ref[:, hs]
            s = lax.dot_general(q, k, (((1,), (1,)), ((), ())), preferred_element_type=F32)
            p = jnp.where(mask, jnp.exp(a_row - m_col), 0.0)
            sp = (s * p).astype(BF16)
            a = jnp.exp(mp_col - m_col)
            cn = c_ref[st]
            tot = (jnp.dot(sp, jnp.concatenate([v, ones_b], axis=1), preferred_element_type=F32)
                   + a * jnp.dot(q, cn.astype(BF16), preferred_element_type=F32))
            r = 1.0 / jnp.maximum(jnp.abs(tot[:, M_HEAD_DIM:]), jnp.exp(-e_col))
            o_ref[:, hs] = tot[:, :M_HEAD_DIM] * jnp.concatenate([r, r], axis=1)
            w = jnp.exp(a_col - ml_col)
            dec = jnp.exp(mp_col[0:1, :] - ml_col[0:1, :])
            wvn = (w * jnp.concatenate([v.astype(F32), ones_f], axis=1)).astype(BF16)
            c_ref[st] = dec * cn + lax.dot_general(
                k, wvn, (((0,), (0,)), ((), ())), preferred_element_type=F32)


def _fwd_blk(c):
    return (c + NBLK - 1) % NBLK


def _bwd_blk(c):
    return (2 * NBLK - 2 - c) % NBLK


def _mlstm(qk3, proj3, gcol, arow):
    def specs(blk):
        return [
            pl.BlockSpec((None, BLK, M_WIDTH), lambda b, c: (b, blk(c), 0)),
            pl.BlockSpec((None, BLK, M_WIDTH), lambda b, c: (b, blk(c), 1)),
            pl.BlockSpec((None, BLK, M_WIDTH), lambda b, c: (b, blk(c), C_MV // M_WIDTH)),
            pl.BlockSpec((None, BLK, BLK), lambda b, c: (b, blk(c), 0)),
            pl.BlockSpec((None, 16, BLK), lambda b, c: (b, 0, blk(c))),
        ]
    out_sd = jax.ShapeDtypeStruct((BATCH, LP, M_WIDTH), F32)
    return pl.pallas_call(
        _mlstm_kernel,
        grid=(BATCH, NBLK),
        in_specs=specs(_fwd_blk) + specs(_bwd_blk),
        out_specs=[
            pl.BlockSpec((None, BLK, M_WIDTH), lambda b, c: (b, _fwd_blk(c), 0)),
            pl.BlockSpec((None, BLK, M_WIDTH), lambda b, c: (b, _bwd_blk(c), 0)),
        ],
        out_shape=[out_sd, out_sd],
        scratch_shapes=[pltpu.VMEM((2 * M_HEADS, M_HEAD_DIM, M_HEAD_DIM + 128), F32)],
        compiler_params=_cparams(("parallel", "arbitrary")),
        name="mlstm",
    )(qk3, qk3, proj3, gcol, arow, qk3, qk3, proj3, gcol, arow)


RQ = 128
NGRP = LP // RQ
EXP_CH = 512
ATTN_NBUF = 2


def _attn_kernel(lam_ref, q_ref, k_ref, v_ref, g_ref, o_ref, *scratch):
    s_refs, p_refs, l_refs, m_refs = (scratch[i * ATTN_NBUF:(i + 1) * ATTN_NBUF] for i in range(4))
    lq = lam_ref[...]
    la = jnp.sum(lq[0:1] * lq[1:2], axis=1, keepdims=True)
    lb = jnp.sum(lq[2:3] * lq[3:4], axis=1, keepdims=True)
    lam = jnp.exp(la) - jnp.exp(lb) + LAM_INIT

    colid = lax.broadcasted_iota(jnp.int32, (1, BLK), 1)
    o_ref[SEQ + N_META:, :] = jnp.zeros((LP - SEQ - N_META, D_V_DIM), o_ref.dtype)

    def group_rows(static_g):
        return N_META if static_g == NGRP - 1 else RQ

    def rows(g, rq):
        return pl.ds(pl.multiple_of(g * RQ, RQ), rq)

    def stage_scores(g, rq, s_ref, m_ref):
        qg = q_ref[rows(g, rq), :]
        lane = lax.broadcasted_iota(jnp.int32, (rq, 128), 1)
        zero = jnp.zeros_like(qg)
        q2 = jnp.concatenate([jnp.where(lane < D_QK_DIM, qg, zero), jnp.where(lane >= D_QK_DIM, qg, zero)], axis=0)
        s = lax.dot_general(q2, k_ref[...], (((1,), (1,)), ((), ())), preferred_element_type=F32)
        tail = jnp.where(colid < N_META, s[:, LP - BLK:], -jnp.inf)
        s_ref[:2 * rq, :LP - BLK] = s[:, :LP - BLK]
        s_ref[:2 * rq, LP - BLK:] = tail
        m = jnp.maximum(jnp.max(s[:, :LP - BLK], axis=1, keepdims=True), jnp.max(tail, axis=1, keepdims=True))
        m_ref[:2 * rq, :] = jnp.broadcast_to(m, (2 * rq, 128))

    def stage_exp(rq, s_ref, m_ref, p_ref, l_ref):
        m = m_ref[:2 * rq, 0:1]
        part = jnp.zeros((2 * rq, 128), F32)
        for c0 in range(0, LP, EXP_CH):
            c1 = min(c0 + EXP_CH, LP)
            pc = jnp.exp2(s_ref[:2 * rq, c0:c1] - m)
            for j in range(0, c1 - c0, 128):
                part = part + pc[:, j:j + 128]
            p_ref[:2 * rq, c0:c1] = pc.astype(BF16)
        l_ref[:2 * rq, :] = jnp.broadcast_to(jnp.sum(part, axis=1, keepdims=True), (2 * rq, 128))

    def stage_pv(g, rq, p_ref, l_ref):
        l = l_ref[:2 * rq, 0:1]
        ratio = (lam * l[:rq] / l[rq:]).astype(BF16)
        a = p_ref[:rq, :] - ratio * p_ref[rq:2 * rq, :]
        o = jnp.dot(a, v_ref[...], preferred_element_type=F32) / l[:rq]
        r = lax.rsqrt(jnp.mean(o * o, axis=-1, keepdims=True) + EPS)
        o_ref[rows(g, rq), :] = (((o * r) * g_ref[...]) * (1.0 - LAM_INIT)).astype(o_ref.dtype)

    nb = len(s_refs)
    sbuf = tuple(zip(s_refs, m_refs))
    pbuf = tuple(zip(p_refs, l_refs))

    def tick(t, static):
        if static < NGRP:
            stage_scores(t, group_rows(static), *sbuf[static % nb])
        if 1 <= static <= NGRP:
            stage_exp(group_rows(static - 1), *sbuf[(static - 1) % nb], *pbuf[(static - 1) % nb])
        if 2 <= static <= NGRP + 1:
            stage_pv(t - 2, group_rows(static - 2), *pbuf[(static - 2) % nb])

    tick(0, 0)
    tick(1, 1)
    n_rounds = (NGRP - 2) // nb

    def body(i, carry):
        t = nb * i + 2
        for j in range(nb):
            tick(t + j, 2 + j)
        return carry

    lax.fori_loop(0, n_rounds, body, 0)
    for t in range(2 + nb * n_rounds, NGRP + 2):
        tick(t, t)


def _attention(lam4, proj3, subln_g):
    return pl.pallas_call(
        _attn_kernel,
        grid=(BATCH, D_HEADS),
        in_specs=[
            pl.BlockSpec((4, D_QK_DIM), lambda b, h: (0, 0)),
            pl.BlockSpec((None, LP, 128), lambda b, h: (b, 0, C_DQ // 128 + h)),
            pl.BlockSpec((None, LP, 128), lambda b, h: (b, 0, C_DK // 128 + h)),
            pl.BlockSpec((None, LP, 128), lambda b, h: (b, 0, C_DV // 128 + h)),
            pl.BlockSpec((1, D_V_DIM), lambda b, h: (0, 0)),
        ],
        out_specs=pl.BlockSpec((None, LP, D_V_DIM), lambda b, h: (b, 0, h)),
        out_shape=jax.ShapeDtypeStruct((BATCH, LP, D_WIDTH), BF16),
        scratch_shapes=(
            [pltpu.VMEM((2 * RQ, LP), F32)] * ATTN_NBUF
            + [pltpu.VMEM((2 * RQ, LP), BF16)] * ATTN_NBUF
            + [pltpu.VMEM((2 * RQ, 128), F32)] * ATTN_NBUF
            + [pltpu.VMEM((2 * RQ, 128), F32)] * ATTN_NBUF
        ),
        compiler_params=_cparams(("parallel", "parallel")),
        name="diff_attn",
    )(lam4, proj3, proj3, proj3, subln_g)


TM4 = 1056
TN4 = 512
LN_CH = 96


def _merge_kernel(hf_ref, hb_ref, mo_ref, ng_ref, bo_ref, wa_ref, wb_ref, gm_ref, gd_ref, o_ref, a_ref):
    n = pl.program_id(1)

    @pl.when(n == 0)
    def _():
        def body(i, carry):
            rows = pl.ds(pl.multiple_of(i * LN_CH, LN_CH), LN_CH)
            for h in range(M_HEADS):
                hs = slice(h * M_HEAD_DIM, (h + 1) * M_HEAD_DIM)
                x = hf_ref[rows, hs] + hb_ref[rows, hs]
                mu = jnp.mean(x, axis=-1, keepdims=True)
                xc = x - mu
                var = jnp.mean(xc * xc, axis=-1, keepdims=True)
                hn = (xc * lax.rsqrt(var + EPS)) * ng_ref[:, hs]
                a_ref[rows, hs] = (_twice_sigmoid_of_half(mo_ref[rows, hs].astype(F32)) * hn).astype(BF16)
            return carry
        lax.fori_loop(0, TM4 // LN_CH, body, 0)

    ya = jnp.dot(a_ref[...], wa_ref[...], preferred_element_type=F32)
    yb = jnp.dot(bo_ref[...], wb_ref[...], preferred_element_type=F32)
    o_ref[...] = (_twice_sigmoid_of_half(gm_ref[...].astype(F32)) * ya
                  + _twice_sigmoid_of_half(gd_ref[...].astype(F32)) * yb).astype(o_ref.dtype)


def _merge(hf, hb, proj, norm_g, b_out, wa, wb):
    return pl.pallas_call(
        _merge_kernel,
        grid=(R // TM4, D_MODEL // TN4),
        in_specs=[
            pl.BlockSpec((TM4, M_WIDTH), lambda m, n: (m, 0)),
            pl.BlockSpec((TM4, M_WIDTH), lambda m, n: (m, 0)),
            pl.BlockSpec((TM4, M_WIDTH), lambda m, n: (m, C_MO // M_WIDTH)),
            pl.BlockSpec((1, M_WIDTH), lambda m, n: (0, 0)),
            pl.BlockSpec((TM4, D_WIDTH), lambda m, n: (m, 0)),
            pl.BlockSpec((M_WIDTH, TN4), lambda m, n: (0, n)),
            pl.BlockSpec((D_WIDTH, TN4), lambda m, n: (0, n)),
            pl.BlockSpec((TM4, TN4), lambda m, n: (m, C_GM // TN4 + n)),
            pl.BlockSpec((TM4, TN4), lambda m, n: (m, C_GD // TN4 + n)),
        ],
        out_specs=pl.BlockSpec((TM4, TN4), lambda m, n: (m, n)),
        out_shape=jax.ShapeDtypeStruct((R, D_MODEL), BF16),
        scratch_shapes=[pltpu.VMEM((TM4, M_WIDTH), BF16)],
        compiler_params=_cparams(("parallel", "arbitrary")),
        name="merge",
    )(hf, hb, proj, norm_g, b_out, wa, wb, proj, proj)


def _outproj_kernel(x_ref, t_ref, mg_ref, w_ref, g2_ref, h1_ref, u2_ref):
    def run(h0_ref):
        h1 = h0_ref[...] + jnp.dot(mg_ref[...], w_ref[...], preferred_element_type=F32)
        h1_ref[...] = h1
        r = lax.rsqrt(jnp.mean(h1 * h1, axis=-1, keepdims=True) + EPS)
        u2_ref[...] = ((h1 * r) * g2_ref[...]).astype(BF16)

    pl.when(pl.program_id(1) < NT0 - 1)(lambda: run(x_ref))
    pl.when(pl.program_id(1) == NT0 - 1)(lambda: run(t_ref))


def _outproj(x, tail, merged3, w, g2):
    row_spec = pl.BlockSpec((None, TM0, D_MODEL), lambda b, j: (b, j, 0))
    return pl.pallas_call(
        _outproj_kernel,
        grid=(BATCH, NT0),
        in_specs=_x_tile_specs() + [
            row_spec,
            pl.BlockSpec((D_MODEL, D_MODEL), lambda b, j: (0, 0)),
            pl.BlockSpec((1, D_MODEL), lambda b, j: (0, 0)),
        ],
        out_specs=[row_spec, row_spec],
        out_shape=[
            jax.ShapeDtypeStruct((BATCH, LP, D_MODEL), F32),
            jax.ShapeDtypeStruct((BATCH, LP, D_MODEL), BF16),
        ],
        compiler_params=_cparams(("parallel", "parallel")),
        name="outproj",
    )(x, tail, merged3, w, g2)


TMU = 1024
TNU = 512
HALO = 16
NNU = FFN_DIM // TNU


def _ffn_up_kernel(x_ref, xp_ref, xn_ref, wg_ref, wv_ref, cg_ref, cv_ref, o_ref, xs_ref, sg_ref, sv_ref):
    @pl.when(pl.program_id(2) == 0)
    def _():
        last_tile = pl.program_id(1) == SEQ // TMU - 1
        xs_ref[0:HALO, :] = xp_ref[...]
        xs_ref[HALO:HALO + TMU, :] = x_ref[...]
        xn = xn_ref[...]
        xs_ref[HALO + TMU:, :] = jnp.where(last_tile, jnp.zeros_like(xn), xn)

    def branch(w_ref, c, s_ref):
        s_ref[...] = jnp.dot(xs_ref[...], w_ref[...].astype(BF16), preferred_element_type=F32)
        return _conv3_rows(s_ref, c, HALO, TMU)

    gate_half = branch(wg_ref, 0.5 * cg_ref[...], sg_ref)
    val = branch(wv_ref, cv_ref[...], sv_ref)
    o_ref[...] = (_silu_of_half(gate_half) * val).astype(o_ref.dtype)


def _ffn_up(u2_3, w_up, conv_w):
    hb = TMU // HALO
    meta_hblk = SEQ // HALO
    return pl.pallas_call(
        _ffn_up_kernel,
        grid=(BATCH, SEQ // TMU, NNU),
        in_specs=[
            pl.BlockSpec((None, TMU, D_MODEL), lambda b, i, n: (b, i, 0)),
            pl.BlockSpec((None, HALO, D_MODEL), lambda b, i, n: (b, jnp.where(i == 0, meta_hblk, i * hb - 1), 0)),
            pl.BlockSpec((None, HALO, D_MODEL), lambda b, i, n: (b, (i + 1) * hb, 0)),
            pl.BlockSpec((D_MODEL, TNU), lambda b, i, n: (0, n)),
            pl.BlockSpec((D_MODEL, TNU), lambda b, i, n: (0, NNU + n)),
            pl.BlockSpec((3, TNU), lambda b, i, n: (0, n)),
            pl.BlockSpec((3, TNU), lambda b, i, n: (0, NNU + n)),
        ],
        out_specs=pl.BlockSpec((None, TMU, TNU), lambda b, i, n: (b, i, n)),
        out_shape=jax.ShapeDtypeStruct((BATCH, SEQ, FFN_DIM), BF16),
        scratch_shapes=[
            pltpu.VMEM((TMU + 2 * HALO, D_MODEL), BF16),
            pltpu.VMEM((TMU + 2 * HALO, TNU), F32),
            pltpu.VMEM((TMU + 2 * HALO, TNU), F32),
        ],
        compiler_params=_cparams(("parallel", "parallel", "arbitrary")),
        name="ffn_up",
    )(u2_3, u2_3, u2_3, w_up, w_up, conv_w, conv_w)


TM7 = 1024
TK7 = 512
NK7 = FFN_DIM // TK7


def _down_kernel(a_ref, wd_ref, h1_ref, gf_ref, o_ref, acc_ref):
    k = pl.program_id(2)

    def part():
        return jnp.dot(a_ref[...], wd_ref[...], preferred_element_type=F32)

    @pl.when(k == 0)
    def _():
        acc_ref[...] = part()

    @pl.when(jnp.logical_and(k > 0, k < NK7 - 1))
    def _():
        acc_ref[...] += part()

    @pl.when(k == NK7 - 1)
    def _():
        h2 = h1_ref[...] + (acc_ref[...] + part())
        r = lax.rsqrt(jnp.mean(h2 * h2, axis=-1, keepdims=True) + EPS)
        o_ref[...] = (h2 * r) * gf_ref[...]


def _ffn_down(act, w_down, h1_3, gf):
    return pl.pallas_call(
        _down_kernel,
        grid=(BATCH, SEQ // TM7, NK7),
        in_specs=[
            pl.BlockSpec((None, TM7, TK7), lambda b, i, k: (b, i, k)),
            pl.BlockSpec((TK7, D_MODEL), lambda b, i, k: (k, 0)),
            pl.BlockSpec((None, TM7, D_MODEL), lambda b, i, k: (b, i, 0)),
            pl.BlockSpec((1, D_MODEL), lambda b, i, k: (0, 0)),
        ],
        out_specs=pl.BlockSpec((None, TM7, D_MODEL), lambda b, i, k: (b, i, 0)),
        out_shape=jax.ShapeDtypeStruct((BATCH, SEQ, D_MODEL), F32),
        scratch_shapes=[pltpu.VMEM((TM7, D_MODEL), F32)],
        compiler_params=_cparams(("parallel", "parallel", "arbitrary")),
        name="ffn_down",
    )(act, w_down, h1_3, gf)


def kernel(x, meta_tokens, norm1_g, w_in, mlstm_conv_w, mlstm_gate_bias, mlstm_norm_g, lambda_q1, lambda_k1,
           lambda_q2, lambda_k2, diff_subln_g, w_branch_m, w_branch_d, w_out, norm2_g, w_up, ffn_conv_w,
           w_down, norm_f_g):
    assert x.shape == (BATCH, SEQ, D_MODEL) and w_in.shape[0] == 1
    tail = _tail_tile(x, meta_tokens)

    wit = w_in[0].T
    halved = np.ones((wit.shape[0], 1), np.float32)
    for c0, width in ((C_MO, M_WIDTH), (C_GM + GROWS, D_MODEL), (C_GD + GROWS, D_MODEL)):
        halved[c0:c0 + width] = 0.5
    w_pa = w_pb = (wit * halved).astype(BF16)
    gbias = mlstm_gate_bias[0].reshape(GROWS, 1).astype(F32)

    u, gt = _prenorm(x, tail, norm1_g[0].reshape(1, D_MODEL), wit, gbias)
    proj = _inproj(u.reshape(R, D_MODEL), w_pa, w_pb, _rope_tables())
    proj3 = proj.reshape(BATCH, LP, PROJ_W)

    arow, gcol = _scans(gt)

    kscale = jnp.concatenate([jnp.ones((1, M_WIDTH), F32), jnp.full((1, M_WIDTH), M_HEAD_DIM ** -0.5, F32)], axis=1)
    qk3 = _qkconv(proj3, mlstm_conv_w[0], kscale)
    hf, hb = _mlstm(qk3, proj3, gcol, arow)

    lam4 = jnp.stack([lambda_q1[0], lambda_k1[0], lambda_q2[0], lambda_k2[0]]).astype(F32)
    b_out = _attention(lam4, proj3, diff_subln_g[0].reshape(1, D_V_DIM))

    merged = _merge(hf.reshape(R, M_WIDTH), hb.reshape(R, M_WIDTH), proj,
                    (0.5 * mlstm_norm_g[0]).reshape(1, M_WIDTH), b_out.reshape(R, D_WIDTH),
                    (0.5 * w_branch_m[0]).astype(BF16), (0.5 * w_branch_d[0]).astype(BF16))
    h1, u2 = _outproj(x, tail, merged.reshape(BATCH, LP, D_MODEL), w_out[0].astype(BF16),
                      norm2_g[0].reshape(1, D_MODEL))

    act = _ffn_up(u2, w_up[0], ffn_conv_w[0])
    return _ffn_down(act, w_down[0].astype(BF16), h1, norm_f_g.reshape(1, D_MODEL))
```

```python
import math

import numpy as np
import jax
import jax.numpy as jnp
from jax import lax
from jax.experimental import pallas as pl
from jax.experimental.pallas import tpu as pltpu

F32 = jnp.float32
BF16 = jnp.bfloat16

D_MODEL = 2048
BATCH = 2
SEQ = 4096
N_META = 16
EPS = 1e-6
M_HEADS = 4
M_WIDTH = 1024
M_HEAD_DIM = 256
D_HEADS = 8
D_V_DIM = 128
D_QK_DIM = 64
D_WIDTH = 1024
ROPE_THETA = 500000.0
ROPE_DIM = 16
FFN_DIM = 5632
LAM_INIT = 0.8 - 0.6 * math.exp(-0.3 * 0)

BLK = 128
LP = SEQ + BLK
NBLK = LP // BLK
META_BLK = SEQ // BLK
R = BATCH * LP
NEG = -1e30

C_MQK, C_MV, C_MO, C_DQ, C_DK, C_DV, C_GM, C_GD = 0, 2048, 3072, 4096, 5120, 6144, 7168, 9216
PROJ_W = 11264

V7X_VMEM_BYTES = 64 * 1024 * 1024
VMEM_LIMIT = V7X_VMEM_BYTES - 8 * 1024 * 1024


def _cparams(sem):
    return pltpu.CompilerParams(dimension_semantics=sem, vmem_limit_bytes=VMEM_LIMIT)


def _twice_sigmoid_of_half(xh):
    return jnp.tanh(xh) + 1.0


def _silu_of_half(yh):
    return yh * (jnp.tanh(yh) + 1.0)


TM0 = 384
NT0 = LP // TM0
X_TAIL0 = (NT0 - 1) * TM0
assert X_TAIL0 <= SEQ < X_TAIL0 + TM0


def _tail_tile(x, meta_tokens):
    meta = jnp.broadcast_to(meta_tokens[None].astype(x.dtype), (BATCH, N_META, D_MODEL))
    pad = jnp.zeros((BATCH, LP - SEQ - N_META, D_MODEL), x.dtype)
    return jnp.concatenate([x[:, X_TAIL0:], meta, pad], axis=1)


def _x_tile_specs():
    return [
        pl.BlockSpec((None, TM0, D_MODEL), lambda b, j, *_: (b, jnp.minimum(j, NT0 - 2), 0)),
        pl.BlockSpec((None, TM0, D_MODEL), lambda b, j, *_: (b, 0, 0)),
    ]


GROWS = 16
GATE_COL0 = 4 * M_WIDTH


def _prenorm_kernel(x_ref, t_ref, g_ref, wg_ref, gb_ref, u_ref, gt_ref):
    def run(src_ref):
        x = src_ref[...]
        r = lax.rsqrt(jnp.mean(x * x, axis=-1, keepdims=True) + EPS)
        u = ((x * r) * g_ref[...]).astype(BF16)
        u_ref[...] = u
        gt = lax.dot_general(wg_ref[...].astype(BF16), u, (((1,), (1,)), ((), ())), preferred_element_type=F32)
        gt_ref[...] = gt + gb_ref[...]

    pl.when(pl.program_id(1) < NT0 - 1)(lambda: run(x_ref))
    pl.when(pl.program_id(1) == NT0 - 1)(lambda: run(t_ref))


def _prenorm(x, tail, g, w_in_t, gbias):
    return pl.pallas_call(
        _prenorm_kernel,
        grid=(BATCH, NT0),
        in_specs=_x_tile_specs() + [
            pl.BlockSpec((1, D_MODEL), lambda b, j: (0, 0)),
            pl.BlockSpec((GROWS, D_MODEL), lambda b, j: (GATE_COL0 // GROWS, 0)),
            pl.BlockSpec((GROWS, 1), lambda b, j: (0, 0)),
        ],
        out_specs=[
            pl.BlockSpec((None, TM0, D_MODEL), lambda b, j: (b, j, 0)),
            pl.BlockSpec((GROWS, TM0), lambda b, j: (0, b * NT0 + j)),
        ],
        out_shape=[
            jax.ShapeDtypeStruct((BATCH, LP, D_MODEL), BF16),
            jax.ShapeDtypeStruct((GROWS, R), F32),
        ],
        compiler_params=_cparams(("parallel", "parallel")),
        name="prenorm",
    )(x, tail, g, w_in_t, gbias)


TM1 = 1408
TN1 = 1024


def _rope_tile(acc, c, s):
    lane = lax.broadcasted_iota(jnp.int32, c.shape, 1)
    first = (lane % D_QK_DIM) < (ROPE_DIM // 2)
    outs = []
    for g in range(acc.shape[1] // 128):
        y = acc[:, g * 128:(g + 1) * 128]
        rot = jnp.where(first, pltpu.roll(y, 128 - ROPE_DIM // 2, 1), pltpu.roll(y, ROPE_DIM // 2, 1))
        outs.append(y * c + rot * s)
    return jnp.concatenate(outs, axis=1)


NA1 = C_DQ // TN1


def _inproj_kernel(u_ref, wa_ref, wb_ref, cq_ref, sq_ref, ck_ref, sk_ref, o_ref):
    n = pl.program_id(1)

    def mm(wt_ref):
        return lax.dot_general(u_ref[...], wt_ref[...], (((1,), (1,)), ((), ())), preferred_element_type=F32)

    q0, k0, v0 = C_DQ // TN1, C_DK // TN1, C_DV // TN1

    @pl.when(n < NA1)
    def _():
        o_ref[...] = mm(wa_ref).astype(o_ref.dtype)

    @pl.when(jnp.logical_and(n >= q0, n < k0))
    def _():
        o_ref[...] = _rope_tile(mm(wb_ref), cq_ref[...], sq_ref[...]).astype(o_ref.dtype)

    @pl.when(jnp.logical_and(n >= k0, n < v0))
    def _():
        o_ref[...] = _rope_tile(mm(wb_ref), ck_ref[...], sk_ref[...]).astype(o_ref.dtype)

    @pl.when(n >= v0)
    def _():
        o_ref[...] = mm(wb_ref).astype(o_ref.dtype)


def _inproj(u, wa, wb, tabs):
    tiles_per_batch = LP // TM1
    tab_spec = pl.BlockSpec((TM1, 128), lambda m, n: (m % tiles_per_batch, 0))
    return pl.pallas_call(
        _inproj_kernel,
        grid=(R // TM1, PROJ_W // TN1),
        in_specs=[
            pl.BlockSpec((TM1, D_MODEL), lambda m, n: (m, 0)),
            pl.BlockSpec((TN1, D_MODEL), lambda m, n: (jnp.minimum(n, NA1 - 1), 0)),
            pl.BlockSpec((pl.Element(TN1), pl.Element(D_MODEL)),
                         lambda m, n: (pl.multiple_of(GATE_COL0 + GROWS + jnp.maximum(n - NA1, 0) * TN1, GROWS), 0)),
            tab_spec, tab_spec, tab_spec, tab_spec,
        ],
        out_specs=pl.BlockSpec((TM1, TN1), lambda m, n: (m, n)),
        out_shape=jax.ShapeDtypeStruct((R, PROJ_W), BF16),
        compiler_params=_cparams(("parallel", "parallel")),
        name="inproj",
    )(u, wa, wb, *tabs)


def _rope_tables():
    pos = np.zeros((LP,), np.float64)
    pos[:SEQ] = np.arange(SEQ) + N_META
    pos[SEQ:SEQ + N_META] = np.arange(N_META)
    half = ROPE_DIM // 2
    inv = ROPE_THETA ** (-np.arange(0, ROPE_DIM, 2, dtype=np.float64) / ROPE_DIM)
    ang = pos[:, None] * inv[None, :]
    cos64 = np.ones((LP, D_QK_DIM)); sin64 = np.zeros((LP, D_QK_DIM))
    cos64[:, :half] = np.cos(ang); cos64[:, half:ROPE_DIM] = np.cos(ang)
    sin64[:, :half] = -np.sin(ang); sin64[:, half:ROPE_DIM] = np.sin(ang)
    cos = np.concatenate([cos64, cos64], axis=1)
    sin = np.concatenate([sin64, sin64], axis=1)
    qs = D_QK_DIM ** -0.5 * math.log2(math.e)
    return [jnp.asarray(t, F32) for t in (cos * qs, sin * qs, cos, sin)]


def _lane_scan(x, op, ident, reverse):
    lane = lax.broadcasted_iota(jnp.int32, x.shape, 1)
    k = 1
    while k < BLK:
        if reverse:
            sh = pltpu.roll(x, BLK - k, 1)
            x = op(x, jnp.where(lane < BLK - k, sh, ident))
        else:
            sh = pltpu.roll(x, k, 1)
            x = op(x, jnp.where(lane >= k, sh, ident))
        k *= 2
    return x


def _log_sigmoid(x):
    return jnp.minimum(x, 0.0) - jnp.log1p(jnp.exp(-jnp.abs(x)))


def _scan_kernel(g_ref, arow_ref, gcol_ref, colt_ref):
    colt_ref[...] = jnp.zeros_like(colt_ref)
    lane = lax.broadcasted_iota(jnp.int32, (8, BLK), 1)
    for d in range(2):
        reverse = d == 1
        carry_b = jnp.zeros((8, BLK), F32)
        carry_m = jnp.full((8, BLK), NEG, F32)
        logical = [META_BLK] + list(range(META_BLK))
        order = logical[::-1] if reverse else logical
        edge = 0 if reverse else BLK - 1
        for j in order:
            cols = slice(j * BLK, (j + 1) * BLK)
            li = g_ref[d * 8:(d + 1) * 8, cols]
            lf = pltpu.roll(_log_sigmoid(li), M_HEADS, 0)
            if j == META_BLK:
                valid = lane < N_META
                li = jnp.where(valid, li, NEG)
                lf = jnp.where(valid, lf, 0.0)
            bc = carry_b + _lane_scan(lf, jnp.add, 0.0, reverse)
            a = li - bc
            m = jnp.maximum(carry_m, _lane_scan(a, jnp.maximum, NEG, reverse))
            m_last = jnp.broadcast_to(m[:, edge:edge + 1], (8, BLK))
            arow_ref[d * 8:(d + 1) * 8, cols] = a
            base = d * 40
            colt_ref[j, base:base + 8, :] = a
            colt_ref[j, base + 8:base + 16, :] = m
            colt_ref[j, base + 16:base + 24, :] = bc + m
            colt_ref[j, base + 24:base + 32, :] = carry_m
            colt_ref[j, base + 32:base + 40, :] = m_last
            carry_b = jnp.broadcast_to(bc[:, edge:edge + 1], (8, BLK))
            carry_m = m_last
    for j in range(NBLK):
        gcol_ref[j * BLK:(j + 1) * BLK, :] = colt_ref[j].T


def _scans(gt):
    return pl.pallas_call(
        _scan_kernel,
        grid=(BATCH,),
        in_specs=[pl.BlockSpec((GROWS, LP), lambda b: (0, b))],
        out_specs=[
            pl.BlockSpec((None, 16, LP), lambda b: (b, 0, 0)),
            pl.BlockSpec((None, LP, BLK), lambda b: (b, 0, 0)),
        ],
        out_shape=[
            jax.ShapeDtypeStruct((BATCH, 16, LP), F32),
            jax.ShapeDtypeStruct((BATCH, LP, BLK), F32),
        ],
        scratch_shapes=[pltpu.VMEM((NBLK, BLK, BLK), F32)],
        compiler_params=_cparams(("parallel",)),
        name="gate_scans",
    )(gt)


TC_COLS = 256
CONV_CH = 256


def _conv3_rows(s_ref, w, lo, n):
    x = s_ref[lo:lo + n, :]
    prev = jnp.concatenate([s_ref[lo - 1:lo + 7, :], pltpu.roll(x, 1, 0)[8:]], axis=0)
    nxt = jnp.concatenate([pltpu.roll(x, n - 1, 0)[:n - 8], s_ref[lo + n - 7:lo + n + 1, :]], axis=0)
    return prev * w[0:1] + x * w[1:2] + nxt * w[2:3]


def _qkconv_kernel(x_ref, w_ref, sc_ref, o_ref, s_ref):
    xo = 8 + N_META
    s_ref[0:8, :] = jnp.zeros((8, TC_COLS), F32)
    s_ref[xo + SEQ:xo + SEQ + 8, :] = jnp.zeros((8, TC_COLS), F32)
    s_ref[8:xo, :] = x_ref[SEQ:SEQ + N_META, :].astype(F32)
    for i in range(SEQ // CONV_CH):
        s_ref[xo + i * CONV_CH:xo + (i + 1) * CONV_CH, :] = x_ref[i * CONV_CH:(i + 1) * CONV_CH, :].astype(F32)
    w = 0.5 * w_ref[...]
    sc = sc_ref[...]

    def act(yh):
        return (_silu_of_half(yh) * sc).astype(o_ref.dtype)

    for i in range(SEQ // CONV_CH):
        o_ref[i * CONV_CH:(i + 1) * CONV_CH, :] = act(_conv3_rows(s_ref, w, xo + i * CONV_CH, CONV_CH))
    o_ref[SEQ:SEQ + N_META, :] = act(_conv3_rows(s_ref, w, 8, N_META))
    o_ref[SEQ + N_META:LP, :] = jnp.zeros((LP - SEQ - N_META, TC_COLS), o_ref.dtype)


def _qkconv(proj3, w, scale):
    return pl.pallas_call(
        _qkconv_kernel,
        grid=(BATCH, 2 * M_WIDTH // TC_COLS),
        in_specs=[
            pl.BlockSpec((None, LP, TC_COLS), lambda b, c: (b, 0, C_MQK // TC_COLS + c)),
            pl.BlockSpec((3, TC_COLS), lambda b, c: (0, c)),
            pl.BlockSpec((1, TC_COLS), lambda b, c: (0, c)),
        ],
        out_specs=pl.BlockSpec((None, LP, TC_COLS), lambda b, c: (b, 0, c)),
        out_shape=jax.ShapeDtypeStruct((BATCH, LP, 2 * M_WIDTH), BF16),
        scratch_shapes=[pltpu.VMEM((SEQ + N_META + 16, TC_COLS), F32)],
        compiler_params=_cparams(("parallel", "parallel")),
        name="qk_conv",
    )(proj3, w, scale)


def _mlstm_kernel(qf, kf, vf, gcf, arf, qb, kb, vb, gcb, arb, hf_ref, hb_ref, c_ref):
    c = pl.program_id(1)

    @pl.when(c == 0)
    def _():
        c_ref[...] = jnp.zeros_like(c_ref)

    row = lax.broadcasted_iota(jnp.int32, (BLK, BLK), 0)
    col = lax.broadcasted_iota(jnp.int32, (BLK, BLK), 1)
    ones_b = jnp.ones((BLK, 128), BF16)
    ones_f = jnp.ones((BLK, 128), F32)
    dirs = ((qf, kf, vf, gcf, arf, hf_ref), (qb, kb, vb, gcb, arb, hb_ref))
    for d, (q_ref, k_ref, v_ref, gc_ref, ar_ref, o_ref) in enumerate(dirs):
        mask = (col <= row) if d == 0 else (col >= row)
        gc = gc_ref[...]
        for h in range(M_HEADS):
            st = d * M_HEADS + h
            def colq(qi):
                j = d * 40 + qi * 8 + h
                return gc[:, j:j + 1]
            a_col, m_col, e_col, mp_col, ml_col = (colq(i) for i in range(5))
            a_row = ar_ref[d * 8 + h:d * 8 + h + 1, :]
            hs = slice(h * M_HEAD_DIM, (h + 1) * M_HEAD_DIM)
            q = q_ref[:, hs]
            k = k_ref[:, hs]
            v = v_ref[:, hs]
            s = lax.dot_general(q, k, (((1,), (1,)), ((), ())), preferred_element_type=F32)
            p = jnp.where(mask, jnp.exp(a_row - m_col), 0.0)
            sp = (s * p).astype(BF16)
            a = jnp.exp(mp_col - m_col)
            cn = c_ref[st]
            tot = (jnp.dot(sp, jnp.concatenate([v, ones_b], axis=1), preferred_element_type=F32)
                   + a * jnp.dot(q, cn.astype(BF16), preferred_element_type=F32))
            r = 1.0 / jnp.maximum(jnp.abs(tot[:, M_HEAD_DIM:]), jnp.exp(-e_col))
            o_ref[:, hs] = (tot[:, :M_HEAD_DIM] * jnp.concatenate([r, r], axis=1)).astype(o_ref.dtype)
            w = jnp.exp(a_col - ml_col)
            dec = jnp.exp(mp_col[0:1, :] - ml_col[0:1, :])
            wvn = (w * jnp.concatenate([v.astype(F32), ones_f], axis=1)).astype(BF16)
            c_ref[st] = dec * cn + lax.dot_general(
                k, wvn, (((0,), (0,)), ((), ())), preferred_element_type=F32)


def _fwd_blk(c):
    return (c + NBLK - 1) % NBLK


def _bwd_blk(c):
    return (2 * NBLK - 2 - c) % NBLK


def _mlstm(qk3, proj3, gcol, arow):
    def specs(blk):
        return [
            pl.BlockSpec((None, BLK, M_WIDTH), lambda b, c: (b, blk(c), 0)),
            pl.BlockSpec((None, BLK, M_WIDTH), lambda b, c: (b, blk(c), 1)),
            pl.BlockSpec((None, BLK, M_WIDTH), lambda b, c: (b, blk(c), C_MV // M_WIDTH)),
            pl.BlockSpec((None, BLK, BLK), lambda b, c: (b, blk(c), 0)),
            pl.BlockSpec((None, 16, BLK), lambda b, c: (b, 0, blk(c))),
        ]
    out_sd = jax.ShapeDtypeStruct((BATCH, LP, M_WIDTH), BF16)
    return pl.pallas_call(
        _mlstm_kernel,
        grid=(BATCH, NBLK),
        in_specs=specs(_fwd_blk) + specs(_bwd_blk),
        out_specs=[
            pl.BlockSpec((None, BLK, M_WIDTH), lambda b, c: (b, _fwd_blk(c), 0)),
            pl.BlockSpec((None, BLK, M_WIDTH), lambda b, c: (b, _bwd_blk(c), 0)),
        ],
        out_shape=[out_sd, out_sd],
        scratch_shapes=[pltpu.VMEM((2 * M_HEADS, M_HEAD_DIM, M_HEAD_DIM + 128), F32)],
        compiler_params=_cparams(("parallel", "arbitrary")),
        name="mlstm",
    )(qk3, qk3, proj3, gcol, arow, qk3, qk3, proj3, gcol, arow)


RQ = 128
NGRP = LP // RQ
EXP_CH = 512
ATTN_NBUF = 2


def _attn_kernel(lam_ref, q_ref, k_ref, v_ref, g_ref, o_ref, *scratch):
    s_refs, p_refs, l_refs, m_refs = (scratch[i * ATTN_NBUF:(i + 1) * ATTN_NBUF] for i in range(4))
    lq = lam_ref[...]
    la = jnp.sum(lq[0:1] * lq[1:2], axis=1, keepdims=True)
    lb = jnp.sum(lq[2:3] * lq[3:4], axis=1, keepdims=True)
    lam = jnp.exp(la) - jnp.exp(lb) + LAM_INIT

    colid = lax.broadcasted_iota(jnp.int32, (1, BLK), 1)
    o_ref[SEQ + N_META:, :] = jnp.zeros((LP - SEQ - N_META, D_V_DIM), o_ref.dtype)

    def group_rows(static_g):
        return N_META if static_g == NGRP - 1 else RQ

    def rows(g, rq):
        return pl.ds(pl.multiple_of(g * RQ, RQ), rq)

    def stage_scores(g, rq, s_ref, m_ref):
        qg = q_ref[rows(g, rq), :]
        lane = lax.broadcasted_iota(jnp.int32, (rq, 128), 1)
        zero = jnp.zeros_like(qg)
        q2 = jnp.concatenate([jnp.where(lane < D_QK_DIM, qg, zero), jnp.where(lane >= D_QK_DIM, qg, zero)], axis=0)
        s = lax.dot_general(q2, k_ref[...], (((1,), (1,)), ((), ())), preferred_element_type=F32)
        tail = jnp.where(colid < N_META, s[:, LP - BLK:], -jnp.inf)
        s_ref[:2 * rq, :LP - BLK] = s[:, :LP - BLK]
        s_ref[:2 * rq, LP - BLK:] = tail
        m = jnp.maximum(jnp.max(s[:, :LP - BLK], axis=1, keepdims=True), jnp.max(tail, axis=1, keepdims=True))
        m_ref[:2 * rq, :] = jnp.broadcast_to(m, (2 * rq, 128))

    def stage_exp(rq, s_ref, m_ref, p_ref, l_ref):
        m = m_ref[:2 * rq, 0:1]
        part = jnp.zeros((2 * rq, 128), F32)
        for c0 in range(0, LP, EXP_CH):
            c1 = min(c0 + EXP_CH, LP)
            pc = jnp.exp2(s_ref[:2 * rq, c0:c1] - m)
            for j in range(0, c1 - c0, 128):
                part = part + pc[:, j:j + 128]
            p_ref[:2 * rq, c0:c1] = pc.astype(BF16)
        l_ref[:2 * rq, :] = jnp.broadcast_to(jnp.sum(part, axis=1, keepdims=True), (2 * rq, 128))

    def stage_pv(g, rq, p_ref, l_ref):
        l = l_ref[:2 * rq, 0:1]
        ratio = (lam * l[:rq] / l[rq:]).astype(BF16)
        a = p_ref[:rq, :] - ratio * p_ref[rq:2 * rq, :]
        o = jnp.dot(a, v_ref[...], preferred_element_type=F32) / l[:rq]
        r = lax.rsqrt(jnp.mean(o * o, axis=-1, keepdims=True) + EPS)
        o_ref[rows(g, rq), :] = (((o * r) * g_ref[...]) * (1.0 - LAM_INIT)).astype(o_ref.dtype)

    nb = len(s_refs)
    sbuf = tuple(zip(s_refs, m_refs))
    pbuf = tuple(zip(p_refs, l_refs))

    def tick(t, static):
        if static < NGRP:
            stage_scores(t, group_rows(static), *sbuf[static % nb])
        if 1 <= static <= NGRP:
            stage_exp(group_rows(static - 1), *sbuf[(static - 1) % nb], *pbuf[(static - 1) % nb])
        if 2 <= static <= NGRP + 1:
            stage_pv(t - 2, group_rows(static - 2), *pbuf[(static - 2) % nb])

    tick(0, 0)
    tick(1, 1)
    n_rounds = (NGRP - 2) // nb

    def body(i, carry):
        t = nb * i + 2
        for j in range(nb):
            tick(t + j, 2 + j)
        return carry

    lax.fori_loop(0, n_rounds, body, 0)
    for t in range(2 + nb * n_rounds, NGRP + 2):
        tick(t, t)


def _attention(lam4, proj3, subln_g):
    return pl.pallas_call(
        _attn_kernel,
        grid=(BATCH, D_HEADS),
        in_specs=[
            pl.BlockSpec((4, D_QK_DIM), lambda b, h: (0, 0)),
            pl.BlockSpec((None, LP, 128), lambda b, h: (b, 0, C_DQ // 128 + h)),
            pl.BlockSpec((None, LP, 128), lambda b, h: (b, 0, C_DK // 128 + h)),
            pl.BlockSpec((None, LP, 128), lambda b, h: (b, 0, C_DV // 128 + h)),
            pl.BlockSpec((1, D_V_DIM), lambda b, h: (0, 0)),
        ],
        out_specs=pl.BlockSpec((None, LP, D_V_DIM), lambda b, h: (b, 0, h)),
        out_shape=jax.ShapeDtypeStruct((BATCH, LP, D_WIDTH), BF16),
        scratch_shapes=(
            [pltpu.VMEM((2 * RQ, LP), F32)] * ATTN_NBUF
            + [pltpu.VMEM((2 * RQ, LP), BF16)] * ATTN_NBUF
            + [pltpu.VMEM((2 * RQ, 128), F32)] * ATTN_NBUF
            + [pltpu.VMEM((2 * RQ, 128), F32)] * ATTN_NBUF
        ),
        compiler_params=_cparams(("parallel", "parallel")),
        name="diff_attn",
    )(lam4, proj3, proj3, proj3, subln_g)


TM4 = 1056
TN4 = 1024
LN_CH = 96


def _merge_kernel(hf_ref, hb_ref, mo_ref, ng_ref, bo_ref, wa_ref, wb_ref, gm_ref, gd_ref, o_ref, a_ref):
    n = pl.program_id(1)

    @pl.when(n == 0)
    def _():
        def body(i, carry):
            rows = pl.ds(pl.multiple_of(i * LN_CH, LN_CH), LN_CH)
            for h in range(M_HEADS):
                hs = slice(h * M_HEAD_DIM, (h + 1) * M_HEAD_DIM)
                x = hf_ref[rows, hs].astype(F32) + hb_ref[rows, hs].astype(F32)
                mu = jnp.mean(x, axis=-1, keepdims=True)
                xc = x - mu
                var = jnp.mean(xc * xc, axis=-1, keepdims=True)
                hn = (xc * lax.rsqrt(var + EPS)) * ng_ref[:, hs]
                a_ref[rows, hs] = (_twice_sigmoid_of_half(mo_ref[rows, hs].astype(F32)) * hn).astype(BF16)
            return carry
        lax.fori_loop(0, TM4 // LN_CH, body, 0)

    ya = jnp.dot(a_ref[...], wa_ref[...], preferred_element_type=F32)
    yb = jnp.dot(bo_ref[...], wb_ref[...], preferred_element_type=F32)
    o_ref[...] = (_twice_sigmoid_of_half(gm_ref[...].astype(F32)) * ya
                  + _twice_sigmoid_of_half(gd_ref[...].astype(F32)) * yb).astype(o_ref.dtype)


def _merge(hf, hb, proj, norm_g, b_out, wa, wb):
    return pl.pallas_call(
        _merge_kernel,
        grid=(R // TM4, D_MODEL // TN4),
        in_specs=[
            pl.BlockSpec((TM4, M_WIDTH), lambda m, n: (m, 0)),
            pl.BlockSpec((TM4, M_WIDTH), lambda m, n: (m, 0)),
            pl.BlockSpec((TM4, M_WIDTH), lambda m, n: (m, C_MO // M_WIDTH)),
            pl.BlockSpec((1, M_WIDTH), lambda m, n: (0, 0)),
            pl.BlockSpec((TM4, D_WIDTH), lambda m, n: (m, 0)),
            pl.BlockSpec((M_WIDTH, TN4), lambda m, n: (0, n)),
            pl.BlockSpec((D_WIDTH, TN4), lambda m, n: (0, n)),
            pl.BlockSpec((TM4, TN4), lambda m, n: (m, C_GM // TN4 + n)),
            pl.BlockSpec((TM4, TN4), lambda m, n: (m, C_GD // TN4 + n)),
        ],
        out_specs=pl.BlockSpec((TM4, TN4), lambda m, n: (m, n)),
        out_shape=jax.ShapeDtypeStruct((R, D_MODEL), BF16),
        scratch_shapes=[pltpu.VMEM((TM4, M_WIDTH), BF16)],
        compiler_params=_cparams(("parallel", "arbitrary")),
        name="merge",
    )(hf, hb, proj, norm_g, b_out, wa, wb, proj, proj)


def _outproj_kernel(x_ref, t_ref, mg_ref, w_ref, g2_ref, h1_ref, u2_ref):
    def run(h0_ref):
        h1 = h0_ref[...] + jnp.dot(mg_ref[...], w_ref[...], preferred_element_type=F32)
        h1_ref[...] = h1
        r = lax.rsqrt(jnp.mean(h1 * h1, axis=-1, keepdims=True) + EPS)
        u2_ref[...] = ((h1 * r) * g2_ref[...]).astype(BF16)

    pl.when(pl.program_id(1) < NT0 - 1)(lambda: run(x_ref))
    pl.when(pl.program_id(1) == NT0 - 1)(lambda: run(t_ref))


def _outproj(x, tail, merged3, w, g2):
    row_spec = pl.BlockSpec((None, TM0, D_MODEL), lambda b, j: (b, j, 0))
    return pl.pallas_call(
        _outproj_kernel,
        grid=(BATCH, NT0),
        in_specs=_x_tile_specs() + [
            row_spec,
            pl.BlockSpec((D_MODEL, D_MODEL), lambda b, j: (0, 0)),
            pl.BlockSpec((1, D_MODEL), lambda b, j: (0, 0)),
        ],
        out_specs=[row_spec, row_spec],
        out_shape=[
            jax.ShapeDtypeStruct((BATCH, LP, D_MODEL), F32),
            jax.ShapeDtypeStruct((BATCH, LP, D_MODEL), BF16),
        ],
        compiler_params=_cparams(("parallel", "parallel")),
        name="outproj",
    )(x, tail, merged3, w, g2)


TMU = 1024
TNU = 512
HALO = 16
NNU = FFN_DIM // TNU


def _ffn_up_kernel(x_ref, xp_ref, xn_ref, wg_ref, wv_ref, cg_ref, cv_ref, o_ref, xs_ref, sg_ref, sv_ref):
    @pl.when(pl.program_id(2) == 0)
    def _():
        last_tile = pl.program_id(1) == SEQ // TMU - 1
        xs_ref[0:HALO, :] = xp_ref[...]
        xs_ref[HALO:HALO + TMU, :] = x_ref[...]
        xn = xn_ref[...]
        xs_ref[HALO + TMU:, :] = jnp.where(last_tile, jnp.zeros_like(xn), xn)

    def branch(w_ref, c, s_ref):
        s_ref[...] = jnp.dot(xs_ref[...], w_ref[...].astype(BF16), preferred_element_type=F32)
        return _conv3_rows(s_ref, c, HALO, TMU)

    gate_half = branch(wg_ref, 0.5 * cg_ref[...], sg_ref)
    val = branch(wv_ref, cv_ref[...], sv_ref)
    o_ref[...] = (_silu_of_half(gate_half) * val).astype(o_ref.dtype)


def _ffn_up(u2_3, w_up, conv_w):
    hb = TMU // HALO
    meta_hblk = SEQ // HALO
    return pl.pallas_call(
        _ffn_up_kernel,
        grid=(BATCH, SEQ // TMU, NNU),
        in_specs=[
            pl.BlockSpec((None, TMU, D_MODEL), lambda b, i, n: (b, i, 0)),
            pl.BlockSpec((None, HALO, D_MODEL), lambda b, i, n: (b, jnp.where(i == 0, meta_hblk, i * hb - 1), 0)),
            pl.BlockSpec((None, HALO, D_MODEL), lambda b, i, n: (b, (i + 1) * hb, 0)),
            pl.BlockSpec((D_MODEL, TNU), lambda b, i, n: (0, n)),
            pl.BlockSpec((D_MODEL, TNU), lambda b, i, n: (0, NNU + n)),
            pl.BlockSpec((3, TNU), lambda b, i, n: (0, n)),
            pl.BlockSpec((3, TNU), lambda b, i, n: (0, NNU + n)),
        ],
        out_specs=pl.BlockSpec((None, TMU, TNU), lambda b, i, n: (b, i, n)),
        out_shape=jax.ShapeDtypeStruct((BATCH, SEQ, FFN_DIM), BF16),
        scratch_shapes=[
            pltpu.VMEM((TMU + 2 * HALO, D_MODEL), BF16),
            pltpu.VMEM((TMU + 2 * HALO, TNU), F32),
            pltpu.VMEM((TMU + 2 * HALO, TNU), F32),
        ],
        compiler_params=_cparams(("parallel", "parallel", "arbitrary")),
        name="ffn_up",
    )(u2_3, u2_3, u2_3, w_up, w_up, conv_w, conv_w)


TM7 = 1024
TK7 = 512
NK7 = FFN_DIM // TK7


def _down_kernel(a_ref, wd_ref, h1_ref, gf_ref, o_ref, acc_ref):
    k = pl.program_id(2)

    def part():
        return jnp.dot(a_ref[...], wd_ref[...], preferred_element_type=F32)

    @pl.when(k == 0)
    def _():
        acc_ref[...] = part()

    @pl.when(jnp.logical_and(k > 0, k < NK7 - 1))
    def _():
        acc_ref[...] += part()

    @pl.when(k == NK7 - 1)
    def _():
        h2 = h1_ref[...] + (acc_ref[...] + part())
        r = lax.rsqrt(jnp.mean(h2 * h2, axis=-1, keepdims=True) + EPS)
        o_ref[...] = (h2 * r) * gf_ref[...]


def _ffn_down(act, w_down, h1_3, gf):
    return pl.pallas_call(
        _down_kernel,
        grid=(BATCH, SEQ // TM7, NK7),
        in_specs=[
            pl.BlockSpec((None, TM7, TK7), lambda b, i, k: (b, i, k)),
            pl.BlockSpec((TK7, D_MODEL), lambda b, i, k: (k, 0)),
            pl.BlockSpec((None, TM7, D_MODEL), lambda b, i, k: (b, i, 0)),
            pl.BlockSpec((1, D_MODEL), lambda b, i, k: (0, 0)),
        ],
        out_specs=pl.BlockSpec((None, TM7, D_MODEL), lambda b, i, k: (b, i, 0)),
        out_shape=jax.ShapeDtypeStruct((BATCH, SEQ, D_MODEL), F32),
        scratch_shapes=[pltpu.VMEM((TM7, D_MODEL), F32)],
        compiler_params=_cparams(("parallel", "parallel", "arbitrary")),
        name="ffn_down",
    )(act, w_down, h1_3, gf)


def kernel(x, meta_tokens, norm1_g, w_in, mlstm_conv_w, mlstm_gate_bias, mlstm_norm_g, lambda_q1, lambda_k1,
           lambda_q2, lambda_k2, diff_subln_g, w_branch_m, w_branch_d, w_out, norm2_g, w_up, ffn_conv_w,
           w_down, norm_f_g):
    assert x.shape == (BATCH, SEQ, D_MODEL) and w_in.shape[0] == 1
    tail = _tail_tile(x, meta_tokens)

    wit = w_in[0].T
    halved = np.ones((wit.shape[0], 1), np.float32)
    for c0, width in ((C_MO, M_WIDTH), (C_GM + GROWS, D_MODEL), (C_GD + GROWS, D_MODEL)):
        halved[c0:c0 + width] = 0.5
    w_pa = w_pb = (wit * halved).astype(BF16)
    gbias = mlstm_gate_bias[0].reshape(GROWS, 1).astype(F32)

    u, gt = _prenorm(x, tail, norm1_g[0].reshape(1, D_MODEL), wit, gbias)
    proj = _inproj(u.reshape(R, D_MODEL), w_pa, w_pb, _rope_tables())
    proj3 = proj.reshape(BATCH, LP, PROJ_W)

    arow, gcol = _scans(gt)

    kscale = jnp.concatenate([jnp.ones((1, M_WIDTH), F32), jnp.full((1, M_WIDTH), M_HEAD_DIM ** -0.5, F32)], axis=1)
    qk3 = _qkconv(proj3, mlstm_conv_w[0], kscale)
    hf, hb = _mlstm(qk3, proj3, gcol, arow)

    lam4 = jnp.stack([lambda_q1[0], lambda_k1[0], lambda_q2[0], lambda_k2[0]]).astype(F32)
    b_out = _attention(lam4, proj3, diff_subln_g[0].reshape(1, D_V_DIM))

    merged = _merge(hf.reshape(R, M_WIDTH), hb.reshape(R, M_WIDTH), proj,
                    (0.5 * mlstm_norm_g[0]).reshape(1, M_WIDTH), b_out.reshape(R, D_WIDTH),
                    (0.5 * w_branch_m[0]).astype(BF16), (0.5 * w_branch_d[0]).astype(BF16))
    h1, u2 = _outproj(x, tail, merged.reshape(BATCH, LP, D_MODEL), w_out[0].astype(BF16),
                      norm2_g[0].reshape(1, D_MODEL))

    act = _ffn_up(u2, w_up[0], ffn_conv_w[0])
    return _ffn_down(act, w_down[0].astype(BF16), h1, norm_f_g.reshape(1, D_MODEL))
```

```python
import math

import numpy as np
import jax
import jax.numpy as jnp
from jax import lax
from jax.experimental import pallas as pl
from jax.experimental.pallas import tpu as pltpu

F32 = jnp.float32
BF16 = jnp.bfloat16

D_MODEL = 2048
BATCH = 2
SEQ = 4096
N_META = 16
EPS = 1e-6
M_HEADS = 4
M_WIDTH = 1024
M_HEAD_DIM = 256
D_HEADS = 8
D_V_DIM = 128
D_QK_DIM = 64
D_WIDTH = 1024
ROPE_THETA = 500000.0
ROPE_DIM = 16
FFN_DIM = 5632
LAM_INIT = 0.8 - 0.6 * math.exp(-0.3 * 0)

BLK = 128
LP = SEQ + BLK
NBLK = LP // BLK
META_BLK = SEQ // BLK
R = BATCH * LP
NEG = -1e30

C_MQK, C_MV, C_MO, C_DQ, C_DK, C_DV, C_GM, C_GD = 0, 2048, 3072, 4096, 5120, 6144, 7168, 9216
PROJ_W = 11264

V7X_VMEM_BYTES = 64 * 1024 * 1024
VMEM_LIMIT = V7X_VMEM_BYTES - 8 * 1024 * 1024


def _cparams(sem):
    return pltpu.CompilerParams(dimension_semantics=sem, vmem_limit_bytes=VMEM_LIMIT)


def _twice_sigmoid_of_half(xh):
    return jnp.tanh(xh) + 1.0


def _silu_of_half(yh):
    return yh * (jnp.tanh(yh) + 1.0)


TM0 = 384
NT0 = LP // TM0
X_TAIL0 = (NT0 - 1) * TM0
assert X_TAIL0 <= SEQ < X_TAIL0 + TM0


def _tail_tile(x, meta_tokens):
    meta = jnp.broadcast_to(meta_tokens[None].astype(x.dtype), (BATCH, N_META, D_MODEL))
    pad = jnp.zeros((BATCH, LP - SEQ - N_META, D_MODEL), x.dtype)
    return jnp.concatenate([x[:, X_TAIL0:], meta, pad], axis=1)


def _x_tile_specs():
    return [
        pl.BlockSpec((None, TM0, D_MODEL), lambda b, j, *_: (b, jnp.minimum(j, NT0 - 2), 0)),
        pl.BlockSpec((None, TM0, D_MODEL), lambda b, j, *_: (b, 0, 0)),
    ]


GROWS = 16
GATE_COL0 = 4 * M_WIDTH


def _prenorm_kernel(x_ref, t_ref, g_ref, wg_ref, gb_ref, u_ref, gt_ref):
    def run(src_ref):
        x = src_ref[...]
        r = lax.rsqrt(jnp.mean(x * x, axis=-1, keepdims=True) + EPS)
        u = ((x * r) * g_ref[...]).astype(BF16)
        u_ref[...] = u
        gt = lax.dot_general(wg_ref[...].astype(BF16), u, (((1,), (1,)), ((), ())), preferred_element_type=F32)
        gt_ref[...] = gt + gb_ref[...]

    pl.when(pl.program_id(1) < NT0 - 1)(lambda: run(x_ref))
    pl.when(pl.program_id(1) == NT0 - 1)(lambda: run(t_ref))


def _prenorm(x, tail, g, w_in_t, gbias):
    return pl.pallas_call(
        _prenorm_kernel,
        grid=(BATCH, NT0),
        in_specs=_x_tile_specs() + [
            pl.BlockSpec((1, D_MODEL), lambda b, j: (0, 0)),
            pl.BlockSpec((GROWS, D_MODEL), lambda b, j: (GATE_COL0 // GROWS, 0)),
            pl.BlockSpec((GROWS, 1), lambda b, j: (0, 0)),
        ],
        out_specs=[
            pl.BlockSpec((None, TM0, D_MODEL), lambda b, j: (b, j, 0)),
            pl.BlockSpec((GROWS, TM0), lambda b, j: (0, b * NT0 + j)),
        ],
        out_shape=[
            jax.ShapeDtypeStruct((BATCH, LP, D_MODEL), BF16),
            jax.ShapeDtypeStruct((GROWS, R), F32),
        ],
        compiler_params=_cparams(("parallel", "parallel")),
        name="prenorm",
    )(x, tail, g, w_in_t, gbias)


TM1 = 1408
TN1 = 1024


def _rope_tile(acc, c, s):
    lane = lax.broadcasted_iota(jnp.int32, c.shape, 1)
    first = (lane % D_QK_DIM) < (ROPE_DIM // 2)
    outs = []
    for g in range(acc.shape[1] // 128):
        y = acc[:, g * 128:(g + 1) * 128]
        rot = jnp.where(first, pltpu.roll(y, 128 - ROPE_DIM // 2, 1), pltpu.roll(y, ROPE_DIM // 2, 1))
        outs.append(y * c + rot * s)
    return jnp.concatenate(outs, axis=1)


NA1 = C_DQ // TN1


def _inproj_kernel(u_ref, wa_ref, wb_ref, cq_ref, sq_ref, ck_ref, sk_ref, o_ref):
    n = pl.program_id(1)

    def mm(wt_ref):
        return lax.dot_general(u_ref[...], wt_ref[...], (((1,), (1,)), ((), ())), preferred_element_type=F32)

    q0, k0, v0 = C_DQ // TN1, C_DK // TN1, C_DV // TN1

    @pl.when(n < NA1)
    def _():
        o_ref[...] = mm(wa_ref).astype(o_ref.dtype)

    @pl.when(jnp.logical_and(n >= q0, n < k0))
    def _():
        o_ref[...] = _rope_tile(mm(wb_ref), cq_ref[...], sq_ref[...]).astype(o_ref.dtype)

    @pl.when(jnp.logical_and(n >= k0, n < v0))
    def _():
        o_ref[...] = _rope_tile(mm(wb_ref), ck_ref[...], sk_ref[...]).astype(o_ref.dtype)

    @pl.when(n >= v0)
    def _():
        o_ref[...] = mm(wb_ref).astype(o_ref.dtype)


def _inproj(u, wa, wb, tabs):
    tiles_per_batch = LP // TM1
    tab_spec = pl.BlockSpec((TM1, 128), lambda m, n: (m % tiles_per_batch, 0))
    return pl.pallas_call(
        _inproj_kernel,
        grid=(R // TM1, PROJ_W // TN1),
        in_specs=[
            pl.BlockSpec((TM1, D_MODEL), lambda m, n: (m, 0)),
            pl.BlockSpec((TN1, D_MODEL), lambda m, n: (jnp.minimum(n, NA1 - 1), 0)),
            pl.BlockSpec((pl.Element(TN1), pl.Element(D_MODEL)),
                         lambda m, n: (pl.multiple_of(GATE_COL0 + GROWS + jnp.maximum(n - NA1, 0) * TN1, GROWS), 0)),
            tab_spec, tab_spec, tab_spec, tab_spec,
        ],
        out_specs=pl.BlockSpec((TM1, TN1), lambda m, n: (m, n)),
        out_shape=jax.ShapeDtypeStruct((R, PROJ_W), BF16),
        compiler_params=_cparams(("parallel", "parallel")),
        name="inproj",
    )(u, wa, wb, *tabs)


def _rope_tables():
    pos = np.zeros((LP,), np.float64)
    pos[:SEQ] = np.arange(SEQ) + N_META
    pos[SEQ:SEQ + N_META] = np.arange(N_META)
    half = ROPE_DIM // 2
    inv = ROPE_THETA ** (-np.arange(0, ROPE_DIM, 2, dtype=np.float64) / ROPE_DIM)
    ang = pos[:, None] * inv[None, :]
    cos64 = np.ones((LP, D_QK_DIM)); sin64 = np.zeros((LP, D_QK_DIM))
    cos64[:, :half] = np.cos(ang); cos64[:, half:ROPE_DIM] = np.cos(ang)
    sin64[:, :half] = -np.sin(ang); sin64[:, half:ROPE_DIM] = np.sin(ang)
    cos = np.concatenate([cos64, cos64], axis=1)
    sin = np.concatenate([sin64, sin64], axis=1)
    qs = D_QK_DIM ** -0.5 * math.log2(math.e)
    return [jnp.asarray(t, F32) for t in (cos * qs, sin * qs, cos, sin)]


def _lane_scan(x, op, ident, reverse):
    lane = lax.broadcasted_iota(jnp.int32, x.shape, 1)
    k = 1
    while k < BLK:
        if reverse:
            sh = pltpu.roll(x, BLK - k, 1)
            x = op(x, jnp.where(lane < BLK - k, sh, ident))
        else:
            sh = pltpu.roll(x, k, 1)
            x = op(x, jnp.where(lane >= k, sh, ident))
        k *= 2
    return x


def _log_sigmoid(x):
    return jnp.minimum(x, 0.0) - jnp.log1p(jnp.exp(-jnp.abs(x)))


def _scan_kernel(g_ref, arow_ref, gcol_ref, colt_ref):
    colt_ref[...] = jnp.zeros_like(colt_ref)
    lane = lax.broadcasted_iota(jnp.int32, (8, BLK), 1)
    for d in range(2):
        reverse = d == 1
        carry_b = jnp.zeros((8, BLK), F32)
        carry_m = jnp.full((8, BLK), NEG, F32)
        logical = [META_BLK] + list(range(META_BLK))
        order = logical[::-1] if reverse else logical
        edge = 0 if reverse else BLK - 1
        for j in order:
            cols = slice(j * BLK, (j + 1) * BLK)
            li = g_ref[d * 8:(d + 1) * 8, cols]
            lf = pltpu.roll(_log_sigmoid(li), M_HEADS, 0)
            if j == META_BLK:
                valid = lane < N_META
                li = jnp.where(valid, li, NEG)
                lf = jnp.where(valid, lf, 0.0)
            bc = carry_b + _lane_scan(lf, jnp.add, 0.0, reverse)
            a = li - bc
            m = jnp.maximum(carry_m, _lane_scan(a, jnp.maximum, NEG, reverse))
            m_last = jnp.broadcast_to(m[:, edge:edge + 1], (8, BLK))
            arow_ref[d * 8:(d + 1) * 8, cols] = a
            base = d * 40
            colt_ref[j, base:base + 8, :] = a
            colt_ref[j, base + 8:base + 16, :] = m
            colt_ref[j, base + 16:base + 24, :] = bc + m
            colt_ref[j, base + 24:base + 32, :] = carry_m
            colt_ref[j, base + 32:base + 40, :] = m_last
            carry_b = jnp.broadcast_to(bc[:, edge:edge + 1], (8, BLK))
            carry_m = m_last
    for j in range(NBLK):
        gcol_ref[j * BLK:(j + 1) * BLK, :] = colt_ref[j].T


def _scans(gt):
    return pl.pallas_call(
        _scan_kernel,
        grid=(BATCH,),
        in_specs=[pl.BlockSpec((GROWS, LP), lambda b: (0, b))],
        out_specs=[
            pl.BlockSpec((None, 16, LP), lambda b: (b, 0, 0)),
            pl.BlockSpec((None, LP, BLK), lambda b: (b, 0, 0)),
        ],
        out_shape=[
            jax.ShapeDtypeStruct((BATCH, 16, LP), F32),
            jax.ShapeDtypeStruct((BATCH, LP, BLK), F32),
        ],
        scratch_shapes=[pltpu.VMEM((NBLK, BLK, BLK), F32)],
        compiler_params=_cparams(("parallel",)),
        name="gate_scans",
    )(gt)


TC_COLS = 256
CONV_CH = 256


def _conv3_rows(s_ref, w, lo, n):
    x = s_ref[lo:lo + n, :]
    prev = jnp.concatenate([s_ref[lo - 1:lo + 7, :], pltpu.roll(x, 1, 0)[8:]], axis=0)
    nxt = jnp.concatenate([pltpu.roll(x, n - 1, 0)[:n - 8], s_ref[lo + n - 7:lo + n + 1, :]], axis=0)
    return prev * w[0:1] + x * w[1:2] + nxt * w[2:3]


def _qkconv_kernel(x_ref, w_ref, sc_ref, o_ref, s_ref):
    xo = 8 + N_META
    s_ref[0:8, :] = jnp.zeros((8, TC_COLS), F32)
    s_ref[xo + SEQ:xo + SEQ + 8, :] = jnp.zeros((8, TC_COLS), F32)
    s_ref[8:xo, :] = x_ref[SEQ:SEQ + N_META, :].astype(F32)
    for i in range(SEQ // CONV_CH):
        s_ref[xo + i * CONV_CH:xo + (i + 1) * CONV_CH, :] = x_ref[i * CONV_CH:(i + 1) * CONV_CH, :].astype(F32)
    w = 0.5 * w_ref[...]
    sc = sc_ref[...]

    def act(yh):
        return (_silu_of_half(yh) * sc).astype(o_ref.dtype)

    for i in range(SEQ // CONV_CH):
        o_ref[i * CONV_CH:(i + 1) * CONV_CH, :] = act(_conv3_rows(s_ref, w, xo + i * CONV_CH, CONV_CH))
    o_ref[SEQ:SEQ + N_META, :] = act(_conv3_rows(s_ref, w, 8, N_META))
    o_ref[SEQ + N_META:LP, :] = jnp.zeros((LP - SEQ - N_META, TC_COLS), o_ref.dtype)


def _qkconv(proj3, w, scale):
    return pl.pallas_call(
        _qkconv_kernel,
        grid=(BATCH, 2 * M_WIDTH // TC_COLS),
        in_specs=[
            pl.BlockSpec((None, LP, TC_COLS), lambda b, c: (b, 0, C_MQK // TC_COLS + c)),
            pl.BlockSpec((3, TC_COLS), lambda b, c: (0, c)),
            pl.BlockSpec((1, TC_COLS), lambda b, c: (0, c)),
        ],
        out_specs=pl.BlockSpec((None, LP, TC_COLS), lambda b, c: (b, 0, c)),
        out_shape=jax.ShapeDtypeStruct((BATCH, LP, 2 * M_WIDTH), BF16),
        scratch_shapes=[pltpu.VMEM((SEQ + N_META + 16, TC_COLS), F32)],
        compiler_params=_cparams(("parallel", "parallel")),
        name="qk_conv",
    )(proj3, w, scale)


def _mlstm_kernel(qf, kf, vf, gcf, arf, qb, kb, vb, gcb, arb, hf_ref, hb_ref, c_ref):
    c = pl.program_id(1)

    @pl.when(c == 0)
    def _():
        c_ref[...] = jnp.zeros_like(c_ref)

    row = lax.broadcasted_iota(jnp.int32, (BLK, BLK), 0)
    col = lax.broadcasted_iota(jnp.int32, (BLK, BLK), 1)
    ones_b = jnp.ones((BLK, 128), BF16)
    ones_f = jnp.ones((BLK, 128), F32)
    dirs = ((qf, kf, vf, gcf, arf, hf_ref), (qb, kb, vb, gcb, arb, hb_ref))
    for d, (q_ref, k_ref, v_ref, gc_ref, ar_ref, o_ref) in enumerate(dirs):
        mask = (col <= row) if d == 0 else (col >= row)
        gc = gc_ref[...]
        for h in range(M_HEADS):
            st = d * M_HEADS + h
            def colq(qi):
                j = d * 40 + qi * 8 + h
                return gc[:, j:j + 1]
            a_col, m_col, e_col, mp_col, ml_col = (colq(i) for i in range(5))
            a_row = ar_ref[d * 8 + h:d * 8 + h + 1, :]
            hs = slice(h * M_HEAD_DIM, (h + 1) * M_HEAD_DIM)
            q = q_ref[:, hs]
            k = k_ref[:, hs]
            v = v_ref[:, hs]
            s = lax.dot_general(q, k, (((1,), (1,)), ((), ())), preferred_element_type=F32)
            p = jnp.where(mask, jnp.exp(a_row - m_col), 0.0)
            sp = (s * p).astype(BF16)
            a = jnp.exp(mp_col - m_col)
            cn = c_ref[st]
            tot = (jnp.dot(sp, jnp.concatenate([v, ones_b], axis=1), preferred_element_type=F32)
                   + a * jnp.dot(q, cn.astype(BF16), preferred_element_type=F32))
            r = 1.0 / jnp.maximum(jnp.abs(tot[:, M_HEAD_DIM:]), jnp.exp(-e_col))
            o_ref[:, hs] = (tot[:, :M_HEAD_DIM] * jnp.concatenate([r, r], axis=1)).astype(o_ref.dtype)
            w = jnp.exp(a_col - ml_col)
            dec = jnp.exp(mp_col[0:1, :] - ml_col[0:1, :])
            wvn = (w * jnp.concatenate([v.astype(F32), ones_f], axis=1)).astype(BF16)
            c_ref[st] = dec * cn + lax.dot_general(
                k, wvn, (((0,), (0,)), ((), ())), preferred_element_type=F32)


def _fwd_blk(c):
    return (c + NBLK - 1) % NBLK


def _bwd_blk(c):
    return (2 * NBLK - 2 - c) % NBLK


def _mlstm(qk3, proj3, gcol, arow):
    def specs(blk):
        return [
            pl.BlockSpec((None, BLK, M_WIDTH), lambda b, c: (b, blk(c), 0)),
            pl.BlockSpec((None, BLK, M_WIDTH), lambda b, c: (b, blk(c), 1)),
            pl.BlockSpec((None, BLK, M_WIDTH), lambda b, c: (b, blk(c), C_MV // M_WIDTH)),
            pl.BlockSpec((None, BLK, BLK), lambda b, c: (b, blk(c), 0)),
            pl.BlockSpec((None, 16, BLK), lambda b, c: (b, 0, blk(c))),
        ]
    out_sd = jax.ShapeDtypeStruct((BATCH, LP, M_WIDTH), BF16)
    return pl.pallas_call(
        _mlstm_kernel,
        grid=(BATCH, NBLK),
        in_specs=specs(_fwd_blk) + specs(_bwd_blk),
        out_specs=[
            pl.BlockSpec((None, BLK, M_WIDTH), lambda b, c: (b, _fwd_blk(c), 0)),
            pl.BlockSpec((None, BLK, M_WIDTH), lambda b, c: (b, _bwd_blk(c), 0)),
        ],
        out_shape=[out_sd, out_sd],
        scratch_shapes=[pltpu.VMEM((2 * M_HEADS, M_HEAD_DIM, M_HEAD_DIM + 128), F32)],
        compiler_params=_cparams(("parallel", "arbitrary")),
        name="mlstm",
    )(qk3, qk3, proj3, gcol, arow, qk3, qk3, proj3, gcol, arow)


RQ = 128
NGRP = LP // RQ
EXP_CH = 512
ATTN_NBUF = 2


def _attn_kernel(lam_ref, q_ref, k_ref, v_ref, g_ref, o_ref, *scratch):
    s_refs, p_refs, l_refs, m_refs = (scratch[i * ATTN_NBUF:(i + 1) * ATTN_NBUF] for i in range(4))
    lq = lam_ref[...]
    la = jnp.sum(lq[0:1] * lq[1:2], axis=1, keepdims=True)
    lb = jnp.sum(lq[2:3] * lq[3:4], axis=1, keepdims=True)
    lam = jnp.exp(la) - jnp.exp(lb) + LAM_INIT

    colid = lax.broadcasted_iota(jnp.int32, (1, BLK), 1)
    o_ref[SEQ + N_META:, :] = jnp.zeros((LP - SEQ - N_META, D_V_DIM), o_ref.dtype)

    def group_rows(static_g):
        return N_META if static_g == NGRP - 1 else RQ

    def rows(g, rq):
        return pl.ds(pl.multiple_of(g * RQ, RQ), rq)

    def stage_scores(g, rq, s_ref, m_ref):
        qg = q_ref[rows(g, rq), :]
        lane = lax.broadcasted_iota(jnp.int32, (rq, 128), 1)
        zero = jnp.zeros_like(qg)
        q2 = jnp.concatenate([jnp.where(lane < D_QK_DIM, qg, zero), jnp.where(lane >= D_QK_DIM, qg, zero)], axis=0)
        s = lax.dot_general(q2, k_ref[...], (((1,), (1,)), ((), ())), preferred_element_type=F32)
        tail = jnp.where(colid < N_META, s[:, LP - BLK:], -jnp.inf)
        s_ref[:2 * rq, :LP - BLK] = s[:, :LP - BLK]
        s_ref[:2 * rq, LP - BLK:] = tail
        m = jnp.maximum(jnp.max(s[:, :LP - BLK], axis=1, keepdims=True), jnp.max(tail, axis=1, keepdims=True))
        m_ref[:2 * rq, :] = jnp.broadcast_to(m, (2 * rq, 128))

    def stage_exp(rq, s_ref, m_ref, p_ref, l_ref):
        m = m_ref[:2 * rq, 0:1]
        part = jnp.zeros((2 * rq, 128), F32)
        for c0 in range(0, LP, EXP_CH):
            c1 = min(c0 + EXP_CH, LP)
            pc = jnp.exp2(s_ref[:2 * rq, c0:c1] - m)
            for j in range(0, c1 - c0, 128):
                part = part + pc[:, j:j + 128]
            p_ref[:2 * rq, c0:c1] = pc.astype(BF16)
        l_ref[:2 * rq, :] = jnp.broadcast_to(jnp.sum(part, axis=1, keepdims=True), (2 * rq, 128))

    def stage_pv(g, rq, p_ref, l_ref):
        l = l_ref[:2 * rq, 0:1]
        ratio = (lam * l[:rq] / l[rq:]).astype(BF16)
        a = p_ref[:rq, :] - ratio * p_ref[rq:2 * rq, :]
        o = jnp.dot(a, v_ref[...], preferred_element_type=F32) / l[:rq]
        r = lax.rsqrt(jnp.mean(o * o, axis=-1, keepdims=True) + EPS)
        o_ref[rows(g, rq), :] = (((o * r) * g_ref[...]) * (1.0 - LAM_INIT)).astype(o_ref.dtype)

    nb = len(s_refs)
    sbuf = tuple(zip(s_refs, m_refs))
    pbuf = tuple(zip(p_refs, l_refs))

    def tick(t, static):
        if static < NGRP:
            stage_scores(t, group_rows(static), *sbuf[static % nb])
        if 1 <= static <= NGRP:
            stage_exp(group_rows(static - 1), *sbuf[(static - 1) % nb], *pbuf[(static - 1) % nb])
        if 2 <= static <= NGRP + 1:
            stage_pv(t - 2, group_rows(static - 2), *pbuf[(static - 2) % nb])

    tick(0, 0)
    tick(1, 1)
    n_rounds = (NGRP - 2) // nb

    def body(i, carry):
        t = nb * i + 2
        for j in range(nb):
            tick(t + j, 2 + j)
        return carry

    lax.fori_loop(0, n_rounds, body, 0)
    for t in range(2 + nb * n_rounds, NGRP + 2):
        tick(t, t)


def _attention(lam4, proj3, subln_g):
    return pl.pallas_call(
        _attn_kernel,
        grid=(BATCH, D_HEADS),
        in_specs=[
            pl.BlockSpec((4, D_QK_DIM), lambda b, h: (0, 0)),
            pl.BlockSpec((None, LP, 128), lambda b, h: (b, 0, C_DQ // 128 + h)),
            pl.BlockSpec((None, LP, 128), lambda b, h: (b, 0, C_DK // 128 + h)),
            pl.BlockSpec((None, LP, 128), lambda b, h: (b, 0, C_DV // 128 + h)),
            pl.BlockSpec((1, D_V_DIM), lambda b, h: (0, 0)),
        ],
        out_specs=pl.BlockSpec((None, LP, D_V_DIM), lambda b, h: (b, 0, h)),
        out_shape=jax.ShapeDtypeStruct((BATCH, LP, D_WIDTH), BF16),
        scratch_shapes=(
            [pltpu.VMEM((2 * RQ, LP), F32)] * ATTN_NBUF
            + [pltpu.VMEM((2 * RQ, LP), BF16)] * ATTN_NBUF
            + [pltpu.VMEM((2 * RQ, 128), F32)] * ATTN_NBUF
            + [pltpu.VMEM((2 * RQ, 128), F32)] * ATTN_NBUF
        ),
        compiler_params=_cparams(("parallel", "parallel")),
        name="diff_attn",
    )(lam4, proj3, proj3, proj3, subln_g)


TM4 = 1056
TN4 = 1024
LN_CH = 96


def _merge_kernel(hf_ref, hb_ref, mo_ref, ng_ref, bo_ref, wa_ref, wb_ref, gm_ref, gd_ref, o_ref, a_ref):
    n = pl.program_id(1)

    @pl.when(n == 0)
    def _():
        def body(i, carry):
            rows = pl.ds(pl.multiple_of(i * LN_CH, LN_CH), LN_CH)
            for h in range(M_HEADS):
                hs = slice(h * M_HEAD_DIM, (h + 1) * M_HEAD_DIM)
                x = hf_ref[rows, hs].astype(F32) + hb_ref[rows, hs].astype(F32)
                mu = jnp.mean(x, axis=-1, keepdims=True)
                xc = x - mu
                var = jnp.mean(xc * xc, axis=-1, keepdims=True)
                hn = (xc * lax.rsqrt(var + EPS)) * ng_ref[:, hs]
                a_ref[rows, hs] = (_twice_sigmoid_of_half(mo_ref[rows, hs].astype(F32)) * hn).astype(BF16)
            return carry
        lax.fori_loop(0, TM4 // LN_CH, body, 0)

    ya = jnp.dot(a_ref[...], wa_ref[...], preferred_element_type=F32)
    yb = jnp.dot(bo_ref[...], wb_ref[...], preferred_element_type=F32)
    o_ref[...] = (_twice_sigmoid_of_half(gm_ref[...].astype(F32)) * ya
                  + _twice_sigmoid_of_half(gd_ref[...].astype(F32)) * yb).astype(o_ref.dtype)


def _merge(hf, hb, proj, norm_g, b_out, wa, wb):
    return pl.pallas_call(
        _merge_kernel,
        grid=(R // TM4, D_MODEL // TN4),
        in_specs=[
            pl.BlockSpec((TM4, M_WIDTH), lambda m, n: (m, 0)),
            pl.BlockSpec((TM4, M_WIDTH), lambda m, n: (m, 0)),
            pl.BlockSpec((TM4, M_WIDTH), lambda m, n: (m, C_MO // M_WIDTH)),
            pl.BlockSpec((1, M_WIDTH), lambda m, n: (0, 0)),
            pl.BlockSpec((TM4, D_WIDTH), lambda m, n: (m, 0)),
            pl.BlockSpec((M_WIDTH, TN4), lambda m, n: (0, n)),
            pl.BlockSpec((D_WIDTH, TN4), lambda m, n: (0, n)),
            pl.BlockSpec((TM4, TN4), lambda m, n: (m, C_GM // TN4 + n)),
            pl.BlockSpec((TM4, TN4), lambda m, n: (m, C_GD // TN4 + n)),
        ],
        out_specs=pl.BlockSpec((TM4, TN4), lambda m, n: (m, n)),
        out_shape=jax.ShapeDtypeStruct((R, D_MODEL), BF16),
        scratch_shapes=[pltpu.VMEM((TM4, M_WIDTH), BF16)],
        compiler_params=_cparams(("parallel", "arbitrary")),
        name="merge",
    )(hf, hb, proj, norm_g, b_out, wa, wb, proj, proj)


def _outproj_kernel(x_ref, t_ref, mg_ref, w_ref, g2_ref, h1_ref, u2_ref):
    def run(h0_ref):
        h1 = h0_ref[...] + jnp.dot(mg_ref[...], w_ref[...], preferred_element_type=F32)
        h1_ref[...] = h1
        r = lax.rsqrt(jnp.mean(h1 * h1, axis=-1, keepdims=True) + EPS)
        u2_ref[...] = ((h1 * r) * g2_ref[...]).astype(BF16)

    pl.when(pl.program_id(1) < NT0 - 1)(lambda: run(x_ref))
    pl.when(pl.program_id(1) == NT0 - 1)(lambda: run(t_ref))


def _outproj(x, tail, merged3, w, g2):
    row_spec = pl.BlockSpec((None, TM0, D_MODEL), lambda b, j: (b, j, 0))
    return pl.pallas_call(
        _outproj_kernel,
        grid=(BATCH, NT0),
        in_specs=_x_tile_specs() + [
            row_spec,
            pl.BlockSpec((D_MODEL, D_MODEL), lambda b, j: (0, 0)),
            pl.BlockSpec((1, D_MODEL), lambda b, j: (0, 0)),
        ],
        out_specs=[row_spec, row_spec],
        out_shape=[
            jax.ShapeDtypeStruct((BATCH, LP, D_MODEL), F32),
            jax.ShapeDtypeStruct((BATCH, LP, D_MODEL), BF16),
        ],
        compiler_params=_cparams(("parallel", "parallel")),
        name="outproj",
    )(x, tail, merged3, w, g2)


TMU = 1024
TNU = 512
HALO = 16
NNU = FFN_DIM // TNU


def _ffn_up_kernel(x_ref, xp_ref, xn_ref, wg_ref, wv_ref, cg_ref, cv_ref, o_ref, xs_ref, sg_ref, sv_ref):
    @pl.when(pl.program_id(2) == 0)
    def _():
        last_tile = pl.program_id(1) == SEQ // TMU - 1
        xs_ref[0:HALO, :] = xp_ref[...]
        xs_ref[HALO:HALO + TMU, :] = x_ref[...]
        xn = xn_ref[...]
        xs_ref[HALO + TMU:, :] = jnp.where(last_tile, jnp.zeros_like(xn), xn)

    def branch(w_ref, c, s_ref):
        s_ref[...] = jnp.dot(xs_ref[...], w_ref[...].astype(BF16), preferred_element_type=F32)
        return _conv3_rows(s_ref, c, HALO, TMU)

    gate_half = branch(wg_ref, 0.5 * cg_ref[...], sg_ref)
    val = branch(wv_ref, cv_ref[...], sv_ref)
    o_ref[...] = (_silu_of_half(gate_half) * val).astype(o_ref.dtype)


def _ffn_up(u2_3, w_up, conv_w):
    hb = TMU // HALO
    meta_hblk = SEQ // HALO
    return pl.pallas_call(
        _ffn_up_kernel,
        grid=(BATCH, SEQ // TMU, NNU),
        in_specs=[
            pl.BlockSpec((None, TMU, D_MODEL), lambda b, i, n: (b, i, 0)),
            pl.BlockSpec((None, HALO, D_MODEL), lambda b, i, n: (b, jnp.where(i == 0, meta_hblk, i * hb - 1), 0)),
            pl.BlockSpec((None, HALO, D_MODEL), lambda b, i, n: (b, (i + 1) * hb, 0)),
            pl.BlockSpec((D_MODEL, TNU), lambda b, i, n: (0, n)),
            pl.BlockSpec((D_MODEL, TNU), lambda b, i, n: (0, NNU + n)),
            pl.BlockSpec((3, TNU), lambda b, i, n: (0, n)),
            pl.BlockSpec((3, TNU), lambda b, i, n: (0, NNU + n)),
        ],
        out_specs=pl.BlockSpec((None, TMU, TNU), lambda b, i, n: (b, i, n)),
        out_shape=jax.ShapeDtypeStruct((BATCH, SEQ, FFN_DIM), BF16),
        scratch_shapes=[
            pltpu.VMEM((TMU + 2 * HALO, D_MODEL), BF16),
            pltpu.VMEM((TMU + 2 * HALO, TNU), F32),
            pltpu.VMEM((TMU + 2 * HALO, TNU), F32),
        ],
        compiler_params=_cparams(("parallel", "parallel", "arbitrary")),
        name="ffn_up",
    )(u2_3, u2_3, u2_3, w_up, w_up, conv_w, conv_w)


TM7 = 1024
TK7 = 512
NK7 = FFN_DIM // TK7


def _down_kernel(a_ref, wd_ref, h1_hbm, gf_ref, o_ref, acc_ref, h1_ref, h1_sem):
    b, i, k = pl.program_id(0), pl.program_id(1), pl.program_id(2)

    def h1_copy():
        rows = pl.ds(pl.multiple_of(i * TM7, TM7), TM7)
        return pltpu.make_async_copy(h1_hbm.at[b, rows, :], h1_ref, h1_sem)

    def part():
        return jnp.dot(a_ref[...], wd_ref[...], preferred_element_type=F32)

    @pl.when(k == 0)
    def _():
        h1_copy().start()
        acc_ref[...] = part()

    @pl.when(jnp.logical_and(k > 0, k < NK7 - 1))
    def _():
        acc_ref[...] += part()

    @pl.when(k == NK7 - 1)
    def _():
        h1_copy().wait()
        h2 = h1_ref[...] + (acc_ref[...] + part())
        r = lax.rsqrt(jnp.mean(h2 * h2, axis=-1, keepdims=True) + EPS)
        o_ref[...] = (h2 * r) * gf_ref[...]


def _ffn_down(act, w_down, h1_3, gf):
    return pl.pallas_call(
        _down_kernel,
        grid=(BATCH, SEQ // TM7, NK7),
        in_specs=[
            pl.BlockSpec((None, TM7, TK7), lambda b, i, k: (b, i, k)),
            pl.BlockSpec((TK7, D_MODEL), lambda b, i, k: (k, 0)),
            pl.BlockSpec(memory_space=pl.ANY),
            pl.BlockSpec((1, D_MODEL), lambda b, i, k: (0, 0)),
        ],
        out_specs=pl.BlockSpec((None, TM7, D_MODEL), lambda b, i, k: (b, i, 0)),
        out_shape=jax.ShapeDtypeStruct((BATCH, SEQ, D_MODEL), F32),
        scratch_shapes=[
            pltpu.VMEM((TM7, D_MODEL), F32),
            pltpu.VMEM((TM7, D_MODEL), F32),
            pltpu.SemaphoreType.DMA(()),
        ],
        compiler_params=_cparams(("arbitrary", "arbitrary", "arbitrary")),
        name="ffn_down",
    )(act, w_down, h1_3, gf)


def kernel(x, meta_tokens, norm1_g, w_in, mlstm_conv_w, mlstm_gate_bias, mlstm_norm_g, lambda_q1, lambda_k1,
           lambda_q2, lambda_k2, diff_subln_g, w_branch_m, w_branch_d, w_out, norm2_g, w_up, ffn_conv_w,
           w_down, norm_f_g):
    assert x.shape == (BATCH, SEQ, D_MODEL) and w_in.shape[0] == 1
    tail = _tail_tile(x, meta_tokens)

    wit = w_in[0].T
    halved = np.ones((wit.shape[0], 1), np.float32)
    for c0, width in ((C_MO, M_WIDTH), (C_GM + GROWS, D_MODEL), (C_GD + GROWS, D_MODEL)):
        halved[c0:c0 + width] = 0.5
    w_pa = w_pb = (wit * halved).astype(BF16)
    gbias = mlstm_gate_bias[0].reshape(GROWS, 1).astype(F32)

    u, gt = _prenorm(x, tail, norm1_g[0].reshape(1, D_MODEL), wit, gbias)
    proj = _inproj(u.reshape(R, D_MODEL), w_pa, w_pb, _rope_tables())
    proj3 = proj.reshape(BATCH, LP, PROJ_W)

    arow, gcol = _scans(gt)

    kscale = jnp.concatenate([jnp.ones((1, M_WIDTH), F32), jnp.full((1, M_WIDTH), M_HEAD_DIM ** -0.5, F32)], axis=1)
    qk3 = _qkconv(proj3, mlstm_conv_w[0], kscale)
    hf, hb = _mlstm(qk3, proj3, gcol, arow)

    lam4 = jnp.stack([lambda_q1[0], lambda_k1[0], lambda_q2[0], lambda_k2[0]]).astype(F32)
    b_out = _attention(lam4, proj3, diff_subln_g[0].reshape(1, D_V_DIM))

    merged = _merge(hf.reshape(R, M_WIDTH), hb.reshape(R, M_WIDTH), proj,
                    (0.5 * mlstm_norm_g[0]).reshape(1, M_WIDTH), b_out.reshape(R, D_WIDTH),
                    (0.5 * w_branch_m[0]).astype(BF16), (0.5 * w_branch_d[0]).astype(BF16))
    h1, u2 = _outproj(x, tail, merged.reshape(BATCH, LP, D_MODEL), w_out[0].astype(BF16),
                      norm2_g[0].reshape(1, D_MODEL))

    act = _ffn_up(u2, w_up[0], ffn_conv_w[0])
    return _ffn_down(act, w_down[0].astype(BF16), h1, norm_f_g.reshape(1, D_MODEL))
```

```python
import math

import numpy as np
import jax
import jax.numpy as jnp
from jax import lax
from jax.experimental import pallas as pl
from jax.experimental.pallas import tpu as pltpu

F32 = jnp.float32
BF16 = jnp.bfloat16

D_MODEL = 2048
BATCH = 2
SEQ = 4096
N_META = 16
EPS = 1e-6
M_HEADS = 4
M_WIDTH = 1024
M_HEAD_DIM = 256
D_HEADS = 8
D_V_DIM = 128
D_QK_DIM = 64
D_WIDTH = 1024
ROPE_THETA = 500000.0
ROPE_DIM = 16
FFN_DIM = 5632
LAM_INIT = 0.8 - 0.6 * math.exp(-0.3 * 0)

BLK = 128
LP = SEQ + BLK
NBLK = LP // BLK
META_BLK = SEQ // BLK
R = BATCH * LP
NEG = -1e30

C_MQK, C_MV, C_MO, C_DQ, C_DK, C_DV, C_GM, C_GD = 0, 2048, 3072, 4096, 5120, 6144, 7168, 9216
PROJ_W = 11264

V7X_VMEM_BYTES = 64 * 1024 * 1024
VMEM_LIMIT = V7X_VMEM_BYTES - 8 * 1024 * 1024


def _cparams(sem):
    return pltpu.CompilerParams(dimension_semantics=sem, vmem_limit_bytes=VMEM_LIMIT)


def _twice_sigmoid_of_half(xh):
    return jnp.tanh(xh) + 1.0


def _silu_of_half(yh):
    return yh * (jnp.tanh(yh) + 1.0)


TM0 = 384
NT0 = LP // TM0
X_TAIL0 = (NT0 - 1) * TM0
assert X_TAIL0 <= SEQ < X_TAIL0 + TM0


def _tail_tile(x, meta_tokens):
    meta = jnp.broadcast_to(meta_tokens[None].astype(x.dtype), (BATCH, N_META, D_MODEL))
    pad = jnp.zeros((BATCH, LP - SEQ - N_META, D_MODEL), x.dtype)
    return jnp.concatenate([x[:, X_TAIL0:], meta, pad], axis=1)


def _x_tile_specs():
    return [
        pl.BlockSpec((None, TM0, D_MODEL), lambda b, j, *_: (b, jnp.minimum(j, NT0 - 2), 0)),
        pl.BlockSpec((None, TM0, D_MODEL), lambda b, j, *_: (b, 0, 0)),
    ]


GROWS = 16
GATE_COL0 = 4 * M_WIDTH


def _prenorm_kernel(x_ref, t_ref, g_ref, wg_ref, gb_ref, u_ref, gt_ref):
    def run(src_ref):
        x = src_ref[...]
        r = lax.rsqrt(jnp.mean(x * x, axis=-1, keepdims=True) + EPS)
        u = ((x * r) * g_ref[...]).astype(BF16)
        u_ref[...] = u
        gt = lax.dot_general(wg_ref[...].astype(BF16), u, (((1,), (1,)), ((), ())), preferred_element_type=F32)
        gt_ref[...] = gt + gb_ref[...]

    pl.when(pl.program_id(1) < NT0 - 1)(lambda: run(x_ref))
    pl.when(pl.program_id(1) == NT0 - 1)(lambda: run(t_ref))


def _prenorm(x, tail, g, w_in_t, gbias):
    return pl.pallas_call(
        _prenorm_kernel,
        grid=(BATCH, NT0),
        in_specs=_x_tile_specs() + [
            pl.BlockSpec((1, D_MODEL), lambda b, j: (0, 0)),
            pl.BlockSpec((GROWS, D_MODEL), lambda b, j: (GATE_COL0 // GROWS, 0)),
            pl.BlockSpec((GROWS, 1), lambda b, j: (0, 0)),
        ],
        out_specs=[
            pl.BlockSpec((None, TM0, D_MODEL), lambda b, j: (b, j, 0)),
            pl.BlockSpec((GROWS, TM0), lambda b, j: (0, b * NT0 + j)),
        ],
        out_shape=[
            jax.ShapeDtypeStruct((BATCH, LP, D_MODEL), BF16),
            jax.ShapeDtypeStruct((GROWS, R), F32),
        ],
        compiler_params=_cparams(("parallel", "parallel")),
        name="prenorm",
    )(x, tail, g, w_in_t, gbias)


TM1 = 1408
TN1 = 1024


def _rope_tile(acc, c, s):
    lane = lax.broadcasted_iota(jnp.int32, c.shape, 1)
    first = (lane % D_QK_DIM) < (ROPE_DIM // 2)
    outs = []
    for g in range(acc.shape[1] // 128):
        y = acc[:, g * 128:(g + 1) * 128]
        rot = jnp.where(first, pltpu.roll(y, 128 - ROPE_DIM // 2, 1), pltpu.roll(y, ROPE_DIM // 2, 1))
        outs.append(y * c + rot * s)
    return jnp.concatenate(outs, axis=1)


NA1 = C_DQ // TN1


def _inproj_kernel(u_ref, wa_ref, wb_ref, cq_ref, sq_ref, ck_ref, sk_ref, o_ref):
    n = pl.program_id(1)

    def mm(wt_ref):
        return lax.dot_general(u_ref[...], wt_ref[...], (((1,), (1,)), ((), ())), preferred_element_type=F32)

    q0, k0, v0 = C_DQ // TN1, C_DK // TN1, C_DV // TN1

    @pl.when(n < NA1)
    def _():
        o_ref[...] = mm(wa_ref).astype(o_ref.dtype)

    @pl.when(jnp.logical_and(n >= q0, n < k0))
    def _():
        o_ref[...] = _rope_tile(mm(wb_ref), cq_ref[...], sq_ref[...]).astype(o_ref.dtype)

    @pl.when(jnp.logical_and(n >= k0, n < v0))
    def _():
        o_ref[...] = _rope_tile(mm(wb_ref), ck_ref[...], sk_ref[...]).astype(o_ref.dtype)

    @pl.when(n >= v0)
    def _():
        o_ref[...] = mm(wb_ref).astype(o_ref.dtype)


def _inproj(u, wa, wb, tabs):
    tiles_per_batch = LP // TM1
    tab_spec = pl.BlockSpec((TM1, 128), lambda m, n: (m % tiles_per_batch, 0))
    return pl.pallas_call(
        _inproj_kernel,
        grid=(R // TM1, PROJ_W // TN1),
        in_specs=[
            pl.BlockSpec((TM1, D_MODEL), lambda m, n: (m, 0)),
            pl.BlockSpec((TN1, D_MODEL), lambda m, n: (jnp.minimum(n, NA1 - 1), 0)),
            pl.BlockSpec((pl.Element(TN1), pl.Element(D_MODEL)),
                         lambda m, n: (pl.multiple_of(GATE_COL0 + GROWS + jnp.maximum(n - NA1, 0) * TN1, GROWS), 0)),
            tab_spec, tab_spec, tab_spec, tab_spec,
        ],
        out_specs=pl.BlockSpec((TM1, TN1), lambda m, n: (m, n)),
        out_shape=jax.ShapeDtypeStruct((R, PROJ_W), BF16),
        compiler_params=_cparams(("parallel", "parallel")),
        name="inproj",
    )(u, wa, wb, *tabs)


def _rope_tables():
    pos = np.zeros((LP,), np.float64)
    pos[:SEQ] = np.arange(SEQ) + N_META
    pos[SEQ:SEQ + N_META] = np.arange(N_META)
    half = ROPE_DIM // 2
    inv = ROPE_THETA ** (-np.arange(0, ROPE_DIM, 2, dtype=np.float64) / ROPE_DIM)
    ang = pos[:, None] * inv[None, :]
    cos64 = np.ones((LP, D_QK_DIM)); sin64 = np.zeros((LP, D_QK_DIM))
    cos64[:, :half] = np.cos(ang); cos64[:, half:ROPE_DIM] = np.cos(ang)
    sin64[:, :half] = -np.sin(ang); sin64[:, half:ROPE_DIM] = np.sin(ang)
    cos = np.concatenate([cos64, cos64], axis=1)
    sin = np.concatenate([sin64, sin64], axis=1)
    qs = D_QK_DIM ** -0.5 * math.log2(math.e)
    return [jnp.asarray(t, F32) for t in (cos * qs, sin * qs, cos, sin)]


def _lane_scan(x, op, ident, reverse):
    lane = lax.broadcasted_iota(jnp.int32, x.shape, 1)
    k = 1
    while k < BLK:
        if reverse:
            sh = pltpu.roll(x, BLK - k, 1)
            x = op(x, jnp.where(lane < BLK - k, sh, ident))
        else:
            sh = pltpu.roll(x, k, 1)
            x = op(x, jnp.where(lane >= k, sh, ident))
        k *= 2
    return x


def _log_sigmoid(x):
    return jnp.minimum(x, 0.0) - jnp.log1p(jnp.exp(-jnp.abs(x)))


def _scan_kernel(g_ref, arow_ref, gcol_ref, colt_ref):
    colt_ref[...] = jnp.zeros_like(colt_ref)
    lane = lax.broadcasted_iota(jnp.int32, (8, BLK), 1)
    for d in range(2):
        reverse = d == 1
        carry_b = jnp.zeros((8, BLK), F32)
        carry_m = jnp.full((8, BLK), NEG, F32)
        logical = [META_BLK] + list(range(META_BLK))
        order = logical[::-1] if reverse else logical
        edge = 0 if reverse else BLK - 1
        for j in order:
            cols = slice(j * BLK, (j + 1) * BLK)
            li = g_ref[d * 8:(d + 1) * 8, cols]
            lf = pltpu.roll(_log_sigmoid(li), M_HEADS, 0)
            if j == META_BLK:
                valid = lane < N_META
                li = jnp.where(valid, li, NEG)
                lf = jnp.where(valid, lf, 0.0)
            bc = carry_b + _lane_scan(lf, jnp.add, 0.0, reverse)
            a = li - bc
            m = jnp.maximum(carry_m, _lane_scan(a, jnp.maximum, NEG, reverse))
            m_last = jnp.broadcast_to(m[:, edge:edge + 1], (8, BLK))
            arow_ref[d * 8:(d + 1) * 8, cols] = a
            base = d * 40
            colt_ref[j, base:base + 8, :] = a
            colt_ref[j, base + 8:base + 16, :] = m
            colt_ref[j, base + 16:base + 24, :] = bc + m
            colt_ref[j, base + 24:base + 32, :] = carry_m
            colt_ref[j, base + 32:base + 40, :] = m_last
            carry_b = jnp.broadcast_to(bc[:, edge:edge + 1], (8, BLK))
            carry_m = m_last
    for j in range(NBLK):
        gcol_ref[j * BLK:(j + 1) * BLK, :] = colt_ref[j].T


def _scans(gt):
    return pl.pallas_call(
        _scan_kernel,
        grid=(BATCH,),
        in_specs=[pl.BlockSpec((GROWS, LP), lambda b: (0, b))],
        out_specs=[
            pl.BlockSpec((None, 16, LP), lambda b: (b, 0, 0)),
            pl.BlockSpec((None, LP, BLK), lambda b: (b, 0, 0)),
        ],
        out_shape=[
            jax.ShapeDtypeStruct((BATCH, 16, LP), F32),
            jax.ShapeDtypeStruct((BATCH, LP, BLK), F32),
        ],
        scratch_shapes=[pltpu.VMEM((NBLK, BLK, BLK), F32)],
        compiler_params=_cparams(("parallel",)),
        name="gate_scans",
    )(gt)


TC_COLS = 256
CONV_CH = 256


def _conv3_rows(s_ref, w, lo, n):
    x = s_ref[lo:lo + n, :]
    prev = jnp.concatenate([s_ref[lo - 1:lo + 7, :], pltpu.roll(x, 1, 0)[8:]], axis=0)
    nxt = jnp.concatenate([pltpu.roll(x, n - 1, 0)[:n - 8], s_ref[lo + n - 7:lo + n + 1, :]], axis=0)
    return prev * w[0:1] + x * w[1:2] + nxt * w[2:3]


def _qkconv_kernel(x_ref, w_ref, sc_ref, o_ref, s_ref):
    xo = 8 + N_META
    s_ref[0:8, :] = jnp.zeros((8, TC_COLS), F32)
    s_ref[xo + SEQ:xo + SEQ + 8, :] = jnp.zeros((8, TC_COLS), F32)
    s_ref[8:xo, :] = x_ref[SEQ:SEQ + N_META, :].astype(F32)
    for i in range(SEQ // CONV_CH):
        s_ref[xo + i * CONV_CH:xo + (i + 1) * CONV_CH, :] = x_ref[i * CONV_CH:(i + 1) * CONV_CH, :].astype(F32)
    w = 0.5 * w_ref[...]
    sc = sc_ref[...]

    def act(yh):
        return (_silu_of_half(yh) * sc).astype(o_ref.dtype)

    for i in range(SEQ // CONV_CH):
        o_ref[i * CONV_CH:(i + 1) * CONV_CH, :] = act(_conv3_rows(s_ref, w, xo + i * CONV_CH, CONV_CH))
    o_ref[SEQ:SEQ + N_META, :] = act(_conv3_rows(s_ref, w, 8, N_META))
    o_ref[SEQ + N_META:LP, :] = jnp.zeros((LP - SEQ - N_META, TC_COLS), o_ref.dtype)


def _qkconv(proj3, w, scale):
    return pl.pallas_call(
        _qkconv_kernel,
        grid=(BATCH, 2 * M_WIDTH // TC_COLS),
        in_specs=[
            pl.BlockSpec((None, LP, TC_COLS), lambda b, c: (b, 0, C_MQK // TC_COLS + c)),
            pl.BlockSpec((3, TC_COLS), lambda b, c: (0, c)),
            pl.BlockSpec((1, TC_COLS), lambda b, c: (0, c)),
        ],
        out_specs=pl.BlockSpec((None, LP, TC_COLS), lambda b, c: (b, 0, c)),
        out_shape=jax.ShapeDtypeStruct((BATCH, LP, 2 * M_WIDTH), BF16),
        scratch_shapes=[pltpu.VMEM((SEQ + N_META + 16, TC_COLS), F32)],
        compiler_params=_cparams(("parallel", "parallel")),
        name="qk_conv",
    )(proj3, w, scale)


def _mlstm_kernel(qf, kf, vf, gcf, arf, qb, kb, vb, gcb, arb, hf_ref, hb_ref, c_ref):
    c = pl.program_id(1)

    @pl.when(c == 0)
    def _():
        c_ref[...] = jnp.zeros_like(c_ref)

    row = lax.broadcasted_iota(jnp.int32, (BLK, BLK), 0)
    col = lax.broadcasted_iota(jnp.int32, (BLK, BLK), 1)
    ones_b = jnp.ones((BLK, 128), BF16)
    ones_f = jnp.ones((BLK, 128), F32)
    dirs = ((qf, kf, vf, gcf, arf, hf_ref), (qb, kb, vb, gcb, arb, hb_ref))
    for d, (q_ref, k_ref, v_ref, gc_ref, ar_ref, o_ref) in enumerate(dirs):
        mask = (col <= row) if d == 0 else (col >= row)
        gc = gc_ref[...]
        for h in range(M_HEADS):
            st = d * M_HEADS + h
            def colq(qi):
                j = d * 40 + qi * 8 + h
                return gc[:, j:j + 1]
            a_col, m_col, e_col, mp_col, ml_col = (colq(i) for i in range(5))
            a_row = ar_ref[d * 8 + h:d * 8 + h + 1, :]
            hs = slice(h * M_HEAD_DIM, (h + 1) * M_HEAD_DIM)
            q = q_ref[:, hs]
            k = k_ref[:, hs]
            v = v_ref[:, hs]
            s = lax.dot_general(q, k, (((1,), (1,)), ((), ())), preferred_element_type=F32)
            p = jnp.where(mask, jnp.exp(a_row - m_col), 0.0)
            sp = (s * p).astype(BF16)
            a = jnp.exp(mp_col - m_col)
            cn = c_ref[st]
            tot = (jnp.dot(sp, jnp.concatenate([v, ones_b], axis=1), preferred_element_type=F32)
                   + a * jnp.dot(q, cn.astype(BF16), preferred_element_type=F32))
            r = 1.0 / jnp.maximum(jnp.abs(tot[:, M_HEAD_DIM:]), jnp.exp(-e_col))
            o_ref[:, hs] = (tot[:, :M_HEAD_DIM] * jnp.concatenate([r, r], axis=1)).astype(o_ref.dtype)
            w = jnp.exp(a_col - ml_col)
            dec = jnp.exp(mp_col[0:1, :] - ml_col[0:1, :])
            wvn = (w * jnp.concatenate([v.astype(F32), ones_f], axis=1)).astype(BF16)
            c_ref[st] = dec * cn + lax.dot_general(
                k, wvn, (((0,), (0,)), ((), ())), preferred_element_type=F32)


def _fwd_blk(c):
    return (c + NBLK - 1) % NBLK


def _bwd_blk(c):
    return (2 * NBLK - 2 - c) % NBLK


def _mlstm(qk3, proj3, gcol, arow):
    def specs(blk):
        return [
            pl.BlockSpec((None, BLK, M_WIDTH), lambda b, c: (b, blk(c), 0)),
            pl.BlockSpec((None, BLK, M_WIDTH), lambda b, c: (b, blk(c), 1)),
            pl.BlockSpec((None, BLK, M_WIDTH), lambda b, c: (b, blk(c), C_MV // M_WIDTH)),
            pl.BlockSpec((None, BLK, BLK), lambda b, c: (b, blk(c), 0)),
            pl.BlockSpec((None, 16, BLK), lambda b, c: (b, 0, blk(c))),
        ]
    out_sd = jax.ShapeDtypeStruct((BATCH, LP, M_WIDTH), BF16)
    return pl.pallas_call(
        _mlstm_kernel,
        grid=(BATCH, NBLK),
        in_specs=specs(_fwd_blk) + specs(_bwd_blk),
        out_specs=[
            pl.BlockSpec((None, BLK, M_WIDTH), lambda b, c: (b, _fwd_blk(c), 0)),
            pl.BlockSpec((None, BLK, M_WIDTH), lambda b, c: (b, _bwd_blk(c), 0)),
        ],
        out_shape=[out_sd, out_sd],
        scratch_shapes=[pltpu.VMEM((2 * M_HEADS, M_HEAD_DIM, M_HEAD_DIM + 128), F32)],
        compiler_params=_cparams(("parallel", "arbitrary")),
        name="mlstm",
    )(qk3, qk3, proj3, gcol, arow, qk3, qk3, proj3, gcol, arow)


RQ = 128
NGRP = LP // RQ
EXP_CH = 512
ATTN_NBUF = 2


def _attn_kernel(lam_ref, q_ref, k_ref, v_ref, g_ref, o_ref, *scratch):
    s_refs, p_refs, l_refs, m_refs = (scratch[i * ATTN_NBUF:(i + 1) * ATTN_NBUF] for i in range(4))
    lq = lam_ref[...]
    la = jnp.sum(lq[0:1] * lq[1:2], axis=1, keepdims=True)
    lb = jnp.sum(lq[2:3] * lq[3:4], axis=1, keepdims=True)
    lam = jnp.exp(la) - jnp.exp(lb) + LAM_INIT

    colid = lax.broadcasted_iota(jnp.int32, (1, BLK), 1)
    o_ref[SEQ + N_META:, :] = jnp.zeros((LP - SEQ - N_META, D_V_DIM), o_ref.dtype)

    def group_rows(static_g):
        return N_META if static_g == NGRP - 1 else RQ

    def rows(g, rq):
        return pl.ds(pl.multiple_of(g * RQ, RQ), rq)

    def stage_scores(g, rq, s_ref, m_ref):
        qg = q_ref[rows(g, rq), :]
        lane = lax.broadcasted_iota(jnp.int32, (rq, 128), 1)
        zero = jnp.zeros_like(qg)
        q2 = jnp.concatenate([jnp.where(lane < D_QK_DIM, qg, zero), jnp.where(lane >= D_QK_DIM, qg, zero)], axis=0)
        s = lax.dot_general(q2, k_ref[...], (((1,), (1,)), ((), ())), preferred_element_type=F32)
        tail = jnp.where(colid < N_META, s[:, LP - BLK:], -jnp.inf)
        s_ref[:2 * rq, :LP - BLK] = s[:, :LP - BLK]
        s_ref[:2 * rq, LP - BLK:] = tail
        m = jnp.maximum(jnp.max(s[:, :LP - BLK], axis=1, keepdims=True), jnp.max(tail, axis=1, keepdims=True))
        m_ref[:2 * rq, :] = jnp.broadcast_to(m, (2 * rq, 128))

    def stage_exp(rq, s_ref, m_ref, p_ref, l_ref):
        m = m_ref[:2 * rq, 0:1]
        part = jnp.zeros((2 * rq, 128), F32)
        for c0 in range(0, LP, EXP_CH):
            c1 = min(c0 + EXP_CH, LP)
            pc = jnp.exp2(s_ref[:2 * rq, c0:c1] - m)
            for j in range(0, c1 - c0, 128):
                part = part + pc[:, j:j + 128]
            p_ref[:2 * rq, c0:c1] = pc.astype(BF16)
        l_ref[:2 * rq, :] = jnp.broadcast_to(jnp.sum(part, axis=1, keepdims=True), (2 * rq, 128))

    def stage_pv(g, rq, p_ref, l_ref):
        l = l_ref[:2 * rq, 0:1]
        ratio = (lam * l[:rq] / l[rq:]).astype(BF16)
        a = p_ref[:rq, :] - ratio * p_ref[rq:2 * rq, :]
        o = jnp.dot(a, v_ref[...], preferred_element_type=F32) / l[:rq]
        r = lax.rsqrt(jnp.mean(o * o, axis=-1, keepdims=True) + EPS)
        o_ref[rows(g, rq), :] = (((o * r) * g_ref[...]) * (1.0 - LAM_INIT)).astype(o_ref.dtype)

    nb = len(s_refs)
    sbuf = tuple(zip(s_refs, m_refs))
    pbuf = tuple(zip(p_refs, l_refs))

    def tick(t, static):
        if static < NGRP:
            stage_scores(t, group_rows(static), *sbuf[static % nb])
        if 1 <= static <= NGRP:
            stage_exp(group_rows(static - 1), *sbuf[(static - 1) % nb], *pbuf[(static - 1) % nb])
        if 2 <= static <= NGRP + 1:
            stage_pv(t - 2, group_rows(static - 2), *pbuf[(static - 2) % nb])

    tick(0, 0)
    tick(1, 1)
    n_rounds = (NGRP - 2) // nb

    def body(i, carry):
        t = nb * i + 2
        for j in range(nb):
            tick(t + j, 2 + j)
        return carry

    lax.fori_loop(0, n_rounds, body, 0)
    for t in range(2 + nb * n_rounds, NGRP + 2):
        tick(t, t)


def _attention(lam4, proj3, subln_g):
    return pl.pallas_call(
        _attn_kernel,
        grid=(BATCH, D_HEADS),
        in_specs=[
            pl.BlockSpec((4, D_QK_DIM), lambda b, h: (0, 0)),
            pl.BlockSpec((None, LP, 128), lambda b, h: (b, 0, C_DQ // 128 + h)),
            pl.BlockSpec((None, LP, 128), lambda b, h: (b, 0, C_DK // 128 + h)),
            pl.BlockSpec((None, LP, 128), lambda b, h: (b, 0, C_DV // 128 + h)),
            pl.BlockSpec((1, D_V_DIM), lambda b, h: (0, 0)),
        ],
        out_specs=pl.BlockSpec((None, LP, D_V_DIM), lambda b, h: (b, 0, h)),
        out_shape=jax.ShapeDtypeStruct((BATCH, LP, D_WIDTH), BF16),
        scratch_shapes=(
            [pltpu.VMEM((2 * RQ, LP), F32)] * ATTN_NBUF
            + [pltpu.VMEM((2 * RQ, LP), BF16)] * ATTN_NBUF
            + [pltpu.VMEM((2 * RQ, 128), F32)] * ATTN_NBUF
            + [pltpu.VMEM((2 * RQ, 128), F32)] * ATTN_NBUF
        ),
        compiler_params=_cparams(("parallel", "parallel")),
        name="diff_attn",
    )(lam4, proj3, proj3, proj3, subln_g)


TM4 = 1056
TN4 = 1024
LN_CH = 96


def _merge_kernel(hf_ref, hb_ref, mo_ref, ng_ref, bo_ref, wa_ref, wb_ref, gm_ref, gd_ref, o_ref, a_ref):
    n = pl.program_id(1)

    @pl.when(n == 0)
    def _():
        def body(i, carry):
            rows = pl.ds(pl.multiple_of(i * LN_CH, LN_CH), LN_CH)
            for h in range(M_HEADS):
                hs = slice(h * M_HEAD_DIM, (h + 1) * M_HEAD_DIM)
                x = hf_ref[rows, hs].astype(F32) + hb_ref[rows, hs].astype(F32)
                mu = jnp.mean(x, axis=-1, keepdims=True)
                xc = x - mu
                var = jnp.mean(xc * xc, axis=-1, keepdims=True)
                hn = (xc * lax.rsqrt(var + EPS)) * ng_ref[:, hs]
                a_ref[rows, hs] = (_twice_sigmoid_of_half(mo_ref[rows, hs].astype(F32)) * hn).astype(BF16)
            return carry
        lax.fori_loop(0, TM4 // LN_CH, body, 0)

    ya = jnp.dot(a_ref[...], wa_ref[...], preferred_element_type=F32)
    yb = jnp.dot(bo_ref[...], wb_ref[...], preferred_element_type=F32)
    o_ref[...] = (_twice_sigmoid_of_half(gm_ref[...].astype(F32)) * ya
                  + _twice_sigmoid_of_half(gd_ref[...].astype(F32)) * yb).astype(o_ref.dtype)


def _merge(hf, hb, proj, norm_g, b_out, wa, wb):
    return pl.pallas_call(
        _merge_kernel,
        grid=(R // TM4, D_MODEL // TN4),
        in_specs=[
            pl.BlockSpec((TM4, M_WIDTH), lambda m, n: (m, 0)),
            pl.BlockSpec((TM4, M_WIDTH), lambda m, n: (m, 0)),
            pl.BlockSpec((TM4, M_WIDTH), lambda m, n: (m, C_MO // M_WIDTH)),
            pl.BlockSpec((1, M_WIDTH), lambda m, n: (0, 0)),
            pl.BlockSpec((TM4, D_WIDTH), lambda m, n: (m, 0)),
            pl.BlockSpec((M_WIDTH, TN4), lambda m, n: (0, n)),
            pl.BlockSpec((D_WIDTH, TN4), lambda m, n: (0, n)),
            pl.BlockSpec((TM4, TN4), lambda m, n: (m, C_GM // TN4 + n)),
            pl.BlockSpec((TM4, TN4), lambda m, n: (m, C_GD // TN4 + n)),
        ],
        out_specs=pl.BlockSpec((TM4, TN4), lambda m, n: (m, n)),
        out_shape=jax.ShapeDtypeStruct((R, D_MODEL), BF16),
        scratch_shapes=[pltpu.VMEM((TM4, M_WIDTH), BF16)],
        compiler_params=_cparams(("parallel", "arbitrary")),
        name="merge",
    )(hf, hb, proj, norm_g, b_out, wa, wb, proj, proj)


def _outproj_kernel(x_ref, t_ref, mg_ref, w_ref, g2_ref, h1_ref, u2_ref):
    def run(h0_ref):
        h1 = h0_ref[...] + jnp.dot(mg_ref[...], w_ref[...], preferred_element_type=F32)
        h1_ref[...] = h1
        r = lax.rsqrt(jnp.mean(h1 * h1, axis=-1, keepdims=True) + EPS)
        u2_ref[...] = ((h1 * r) * g2_ref[...]).astype(BF16)

    pl.when(pl.program_id(1) < NT0 - 1)(lambda: run(x_ref))
    pl.when(pl.program_id(1) == NT0 - 1)(lambda: run(t_ref))


def _outproj(x, tail, merged3, w, g2):
    row_spec = pl.BlockSpec((None, TM0, D_MODEL), lambda b, j: (b, j, 0))
    return pl.pallas_call(
        _outproj_kernel,
        grid=(BATCH, NT0),
        in_specs=_x_tile_specs() + [
            row_spec,
            pl.BlockSpec((D_MODEL, D_MODEL), lambda b, j: (0, 0)),
            pl.BlockSpec((1, D_MODEL), lambda b, j: (0, 0)),
        ],
        out_specs=[row_spec, row_spec],
        out_shape=[
            jax.ShapeDtypeStruct((BATCH, LP, D_MODEL), F32),
            jax.ShapeDtypeStruct((BATCH, LP, D_MODEL), BF16),
        ],
        compiler_params=_cparams(("parallel", "parallel")),
        name="outproj",
    )(x, tail, merged3, w, g2)


TMU = 1024
TNU = 512
HALO = 16
NNU = FFN_DIM // TNU


def _ffn_up_kernel(x_ref, xp_ref, xn_ref, wg_ref, wv_ref, cg_ref, cv_ref, o_ref, xs_ref, sg_ref, sv_ref):
    @pl.when(pl.program_id(2) == 0)
    def _():
        last_tile = pl.program_id(1) == SEQ // TMU - 1
        xs_ref[0:HALO, :] = xp_ref[...]
        xs_ref[HALO:HALO + TMU, :] = x_ref[...]
        xn = xn_ref[...]
        xs_ref[HALO + TMU:, :] = jnp.where(last_tile, jnp.zeros_like(xn), xn)

    def branch(w_ref, c, s_ref):
        s_ref[...] = jnp.dot(xs_ref[...], w_ref[...].astype(BF16), preferred_element_type=F32)
        return _conv3_rows(s_ref, c, HALO, TMU)

    gate_act = _silu_of_half(branch(wg_ref, 0.5 * cg_ref[...], sg_ref))

    m = gate_act[:, 0:128]
    for j in range(1, TNU // 128):
        m = jnp.maximum(m, gate_act[:, j * 128:(j + 1) * 128])
    n = TMU
    while n > 8:
        n //= 2
        m = jnp.maximum(m[:n], m[n:2 * n])
    zero = pltpu.bitcast((pltpu.bitcast(m, jnp.uint32) >> 16) >> 16, BF16)

    wv = wv_ref[...].astype(BF16)
    half = (TMU + 2 * HALO) // 2
    sv_ref[0:half, :] = jnp.dot(xs_ref[0:half, :], wv, preferred_element_type=F32)
    lhs = xs_ref[half:, :]
    top = jnp.concatenate([lhs[0:16, 0:128] + zero, lhs[0:16, 128:]], axis=1)
    lhs = jnp.concatenate([top, lhs[16:]], axis=0)
    sv_ref[half:, :] = jnp.dot(lhs, wv, preferred_element_type=F32)
    val = _conv3_rows(sv_ref, cv_ref[...], HALO, TMU)
    o_ref[...] = (gate_act * val).astype(o_ref.dtype)


def _ffn_up(u2_3, w_up, conv_w):
    hb = TMU // HALO
    meta_hblk = SEQ // HALO
    return pl.pallas_call(
        _ffn_up_kernel,
        grid=(BATCH, SEQ // TMU, NNU),
        in_specs=[
            pl.BlockSpec((None, TMU, D_MODEL), lambda b, i, n: (b, i, 0)),
            pl.BlockSpec((None, HALO, D_MODEL), lambda b, i, n: (b, jnp.where(i == 0, meta_hblk, i * hb - 1), 0)),
            pl.BlockSpec((None, HALO, D_MODEL), lambda b, i, n: (b, (i + 1) * hb, 0)),
            pl.BlockSpec((D_MODEL, TNU), lambda b, i, n: (0, n)),
            pl.BlockSpec((D_MODEL, TNU), lambda b, i, n: (0, NNU + n)),
            pl.BlockSpec((3, TNU), lambda b, i, n: (0, n)),
            pl.BlockSpec((3, TNU), lambda b, i, n: (0, NNU + n)),
        ],
        out_specs=pl.BlockSpec((None, TMU, TNU), lambda b, i, n: (b, i, n)),
        out_shape=jax.ShapeDtypeStruct((BATCH, SEQ, FFN_DIM), BF16),
        scratch_shapes=[
            pltpu.VMEM((TMU + 2 * HALO, D_MODEL), BF16),
            pltpu.VMEM((TMU + 2 * HALO, TNU), F32),
            pltpu.VMEM((TMU + 2 * HALO, TNU), F32),
        ],
        compiler_params=_cparams(("parallel", "parallel", "arbitrary")),
        name="ffn_up",
    )(u2_3, u2_3, u2_3, w_up, w_up, conv_w, conv_w)


TM7 = 1024
TK7 = 512
NK7 = FFN_DIM // TK7


def _down_kernel(a_ref, wd_ref, h1_ref, gf_ref, o_ref, acc_ref):
    k = pl.program_id(2)

    def part():
        return jnp.dot(a_ref[...], wd_ref[...], preferred_element_type=F32)

    @pl.when(k == 0)
    def _():
        acc_ref[...] = part()

    @pl.when(jnp.logical_and(k > 0, k < NK7 - 1))
    def _():
        acc_ref[...] += part()

    @pl.when(k == NK7 - 1)
    def _():
        h2 = h1_ref[...] + (acc_ref[...] + part())
        r = lax.rsqrt(jnp.mean(h2 * h2, axis=-1, keepdims=True) + EPS)
        o_ref[...] = (h2 * r) * gf_ref[...]


def _ffn_down(act, w_down, h1_3, gf):
    return pl.pallas_call(
        _down_kernel,
        grid=(BATCH, SEQ // TM7, NK7),
        in_specs=[
            pl.BlockSpec((None, TM7, TK7), lambda b, i, k: (b, i, k)),
            pl.BlockSpec((TK7, D_MODEL), lambda b, i, k: (k, 0)),
            pl.BlockSpec((None, TM7, D_MODEL), lambda b, i, k: (b, i, 0)),
            pl.BlockSpec((1, D_MODEL), lambda b, i, k: (0, 0)),
        ],
        out_specs=pl.BlockSpec((None, TM7, D_MODEL), lambda b, i, k: (b, i, 0)),
        out_shape=jax.ShapeDtypeStruct((BATCH, SEQ, D_MODEL), F32),
        scratch_shapes=[pltpu.VMEM((TM7, D_MODEL), F32)],
        compiler_params=_cparams(("parallel", "parallel", "arbitrary")),
        name="ffn_down",
    )(act, w_down, h1_3, gf)


def kernel(x, meta_tokens, norm1_g, w_in, mlstm_conv_w, mlstm_gate_bias, mlstm_norm_g, lambda_q1, lambda_k1,
           lambda_q2, lambda_k2, diff_subln_g, w_branch_m, w_branch_d, w_out, norm2_g, w_up, ffn_conv_w,
           w_down, norm_f_g):
    assert x.shape == (BATCH, SEQ, D_MODEL) and w_in.shape[0] == 1
    tail = _tail_tile(x, meta_tokens)

    wit = w_in[0].T
    halved = np.ones((wit.shape[0], 1), np.float32)
    for c0, width in ((C_MO, M_WIDTH), (C_GM + GROWS, D_MODEL), (C_GD + GROWS, D_MODEL)):
        halved[c0:c0 + width] = 0.5
    w_pa = w_pb = (wit * halved).astype(BF16)
    gbias = mlstm_gate_bias[0].reshape(GROWS, 1).astype(F32)

    u, gt = _prenorm(x, tail, norm1_g[0].reshape(1, D_MODEL), wit, gbias)
    proj = _inproj(u.reshape(R, D_MODEL), w_pa, w_pb, _rope_tables())
    proj3 = proj.reshape(BATCH, LP, PROJ_W)

    arow, gcol = _scans(gt)

    kscale = jnp.concatenate([jnp.ones((1, M_WIDTH), F32), jnp.full((1, M_WIDTH), M_HEAD_DIM ** -0.5, F32)], axis=1)
    qk3 = _qkconv(proj3, mlstm_conv_w[0], kscale)
    hf, hb = _mlstm(qk3, proj3, gcol, arow)

    lam4 = jnp.stack([lambda_q1[0], lambda_k1[0], lambda_q2[0], lambda_k2[0]]).astype(F32)
    b_out = _attention(lam4, proj3, diff_subln_g[0].reshape(1, D_V_DIM))

    merged = _merge(hf.reshape(R, M_WIDTH), hb.reshape(R, M_WIDTH), proj,
                    (0.5 * mlstm_norm_g[0]).reshape(1, M_WIDTH), b_out.reshape(R, D_WIDTH),
                    (0.5 * w_branch_m[0]).astype(BF16), (0.5 * w_branch_d[0]).astype(BF16))
    h1, u2 = _outproj(x, tail, merged.reshape(BATCH, LP, D_MODEL), w_out[0].astype(BF16),
                      norm2_g[0].reshape(1, D_MODEL))

    act = _ffn_up(u2, w_up[0], ffn_conv_w[0])
    return _ffn_down(act, w_down[0].astype(BF16), h1, norm_f_g.reshape(1, D_MODEL))
```

```python
import math

import numpy as np
import jax
import jax.numpy as jnp
from jax import lax
from jax.experimental import pallas as pl
from jax.experimental.pallas import tpu as pltpu

F32 = jnp.float32
BF16 = jnp.bfloat16

D_MODEL = 2048
BATCH = 2
SEQ = 4096
N_META = 16
EPS = 1e-6
M_HEADS = 4
M_WIDTH = 1024
M_HEAD_DIM = 256
D_HEADS = 8
D_V_DIM = 128
D_QK_DIM = 64
D_WIDTH = 1024
ROPE_THETA = 500000.0
ROPE_DIM = 16
FFN_DIM = 5632
LAM_INIT = 0.8 - 0.6 * math.exp(-0.3 * 0)

BLK = 128
LP = SEQ + BLK
NBLK = LP // BLK
META_BLK = SEQ // BLK
R = BATCH * LP
NEG = -1e30

C_MQK, C_MV, C_MO, C_DQ, C_DK, C_DV, C_GM, C_GD = 0, 2048, 3072, 4096, 5120, 6144, 7168, 9216
PROJ_W = 11264

V7X_VMEM_BYTES = 64 * 1024 * 1024
VMEM_LIMIT = V7X_VMEM_BYTES - 8 * 1024 * 1024


def _cparams(sem):
    return pltpu.CompilerParams(dimension_semantics=sem, vmem_limit_bytes=VMEM_LIMIT)


def _twice_sigmoid_of_half(xh):
    return jnp.tanh(xh) + 1.0


def _silu_of_half(yh):
    return yh * (jnp.tanh(yh) + 1.0)


TM0 = 384
NT0 = LP // TM0
X_TAIL0 = (NT0 - 1) * TM0
assert X_TAIL0 <= SEQ < X_TAIL0 + TM0


def _tail_tile(x, meta_tokens):
    meta = jnp.broadcast_to(meta_tokens[None].astype(x.dtype), (BATCH, N_META, D_MODEL))
    pad = jnp.zeros((BATCH, LP - SEQ - N_META, D_MODEL), x.dtype)
    return jnp.concatenate([x[:, X_TAIL0:], meta, pad], axis=1)


def _x_tile_specs():
    return [
        pl.BlockSpec((None, TM0, D_MODEL), lambda b, j, *_: (b, jnp.minimum(j, NT0 - 2), 0)),
        pl.BlockSpec((None, TM0, D_MODEL), lambda b, j, *_: (b, 0, 0)),
    ]


GROWS = 16
GATE_COL0 = 4 * M_WIDTH


def _prenorm_kernel(x_ref, t_ref, g_ref, wg_ref, gb_ref, u_ref, gt_ref):
    def run(src_ref):
        x = src_ref[...]
        r = lax.rsqrt(jnp.mean(x * x, axis=-1, keepdims=True) + EPS)
        u = ((x * r) * g_ref[...]).astype(BF16)
        u_ref[...] = u
        gt = lax.dot_general(wg_ref[...].astype(BF16), u, (((1,), (1,)), ((), ())), preferred_element_type=F32)
        gt_ref[...] = gt + gb_ref[...]

    pl.when(pl.program_id(1) < NT0 - 1)(lambda: run(x_ref))
    pl.when(pl.program_id(1) == NT0 - 1)(lambda: run(t_ref))


def _prenorm(x, tail, g, w_in_t, gbias):
    return pl.pallas_call(
        _prenorm_kernel,
        grid=(BATCH, NT0),
        in_specs=_x_tile_specs() + [
            pl.BlockSpec((1, D_MODEL), lambda b, j: (0, 0)),
            pl.BlockSpec((GROWS, D_MODEL), lambda b, j: (GATE_COL0 // GROWS, 0)),
            pl.BlockSpec((GROWS, 1), lambda b, j: (0, 0)),
        ],
        out_specs=[
            pl.BlockSpec((None, TM0, D_MODEL), lambda b, j: (b, j, 0)),
            pl.BlockSpec((GROWS, TM0), lambda b, j: (0, b * NT0 + j)),
        ],
        out_shape=[
            jax.ShapeDtypeStruct((BATCH, LP, D_MODEL), BF16),
            jax.ShapeDtypeStruct((GROWS, R), F32),
        ],
        compiler_params=_cparams(("parallel", "parallel")),
        name="prenorm",
    )(x, tail, g, w_in_t, gbias)


TM1 = 1408
TN1 = 1024


def _rope_tile(acc, c, s):
    lane = lax.broadcasted_iota(jnp.int32, c.shape, 1)
    first = (lane % D_QK_DIM) < (ROPE_DIM // 2)
    outs = []
    for g in range(acc.shape[1] // 128):
        y = acc[:, g * 128:(g + 1) * 128]
        rot = jnp.where(first, pltpu.roll(y, 128 - ROPE_DIM // 2, 1), pltpu.roll(y, ROPE_DIM // 2, 1))
        outs.append(y * c + rot * s)
    return jnp.concatenate(outs, axis=1)


NA1 = C_DQ // TN1


def _inproj_kernel(u_ref, wa_ref, wb_ref, cq_ref, sq_ref, ck_ref, sk_ref, o_ref):
    n = pl.program_id(1)

    def mm(wt_ref):
        return lax.dot_general(u_ref[...], wt_ref[...], (((1,), (1,)), ((), ())), preferred_element_type=F32)

    q0, k0, v0 = C_DQ // TN1, C_DK // TN1, C_DV // TN1

    @pl.when(n < NA1)
    def _():
        o_ref[...] = mm(wa_ref).astype(o_ref.dtype)

    @pl.when(jnp.logical_and(n >= q0, n < k0))
    def _():
        o_ref[...] = _rope_tile(mm(wb_ref), cq_ref[...], sq_ref[...]).astype(o_ref.dtype)

    @pl.when(jnp.logical_and(n >= k0, n < v0))
    def _():
        o_ref[...] = _rope_tile(mm(wb_ref), ck_ref[...], sk_ref[...]).astype(o_ref.dtype)

    @pl.when(n >= v0)
    def _():
        o_ref[...] = mm(wb_ref).astype(o_ref.dtype)


def _inproj(u, wa, wb, tabs):
    tiles_per_batch = LP // TM1
    tab_spec = pl.BlockSpec((TM1, 128), lambda m, n: (m % tiles_per_batch, 0))
    return pl.pallas_call(
        _inproj_kernel,
        grid=(R // TM1, PROJ_W // TN1),
        in_specs=[
            pl.BlockSpec((TM1, D_MODEL), lambda m, n: (m, 0)),
            pl.BlockSpec((TN1, D_MODEL), lambda m, n: (jnp.minimum(n, NA1 - 1), 0)),
            pl.BlockSpec((pl.Element(TN1), pl.Element(D_MODEL)),
                         lambda m, n: (pl.multiple_of(GATE_COL0 + GROWS + jnp.maximum(n - NA1, 0) * TN1, GROWS), 0)),
            tab_spec, tab_spec, tab_spec, tab_spec,
        ],
        out_specs=pl.BlockSpec((TM1, TN1), lambda m, n: (m, n)),
        out_shape=jax.ShapeDtypeStruct((R, PROJ_W), BF16),
        compiler_params=_cparams(("parallel", "parallel")),
        name="inproj",
    )(u, wa, wb, *tabs)


def _rope_tables():
    pos = np.zeros((LP,), np.float64)
    pos[:SEQ] = np.arange(SEQ) + N_META
    pos[SEQ:SEQ + N_META] = np.arange(N_META)
    half = ROPE_DIM // 2
    inv = ROPE_THETA ** (-np.arange(0, ROPE_DIM, 2, dtype=np.float64) / ROPE_DIM)
    ang = pos[:, None] * inv[None, :]
    cos64 = np.ones((LP, D_QK_DIM)); sin64 = np.zeros((LP, D_QK_DIM))
    cos64[:, :half] = np.cos(ang); cos64[:, half:ROPE_DIM] = np.cos(ang)
    sin64[:, :half] = -np.sin(ang); sin64[:, half:ROPE_DIM] = np.sin(ang)
    cos = np.concatenate([cos64, cos64], axis=1)
    sin = np.concatenate([sin64, sin64], axis=1)
    qs = D_QK_DIM ** -0.5 * math.log2(math.e)
    return [jnp.asarray(t, F32) for t in (cos * qs, sin * qs, cos, sin)]


def _lane_scan(x, op, ident, reverse):
    lane = lax.broadcasted_iota(jnp.int32, x.shape, 1)
    k = 1
    while k < BLK:
        if reverse:
            sh = pltpu.roll(x, BLK - k, 1)
            x = op(x, jnp.where(lane < BLK - k, sh, ident))
        else:
            sh = pltpu.roll(x, k, 1)
            x = op(x, jnp.where(lane >= k, sh, ident))
        k *= 2
    return x


def _log_sigmoid(x):
    return jnp.minimum(x, 0.0) - jnp.log1p(jnp.exp(-jnp.abs(x)))


def _scan_kernel(g_ref, arow_ref, gcol_ref, colt_ref):
    colt_ref[...] = jnp.zeros_like(colt_ref)
    lane = lax.broadcasted_iota(jnp.int32, (8, BLK), 1)
    for d in range(2):
        reverse = d == 1
        carry_b = jnp.zeros((8, BLK), F32)
        carry_m = jnp.full((8, BLK), NEG, F32)
        logical = [META_BLK] + list(range(META_BLK))
        order = logical[::-1] if reverse else logical
        edge = 0 if reverse else BLK - 1
        for j in order:
            cols = slice(j * BLK, (j + 1) * BLK)
            li = g_ref[d * 8:(d + 1) * 8, cols]
            lf = pltpu.roll(_log_sigmoid(li), M_HEADS, 0)
            if j == META_BLK:
                valid = lane < N_META
                li = jnp.where(valid, li, NEG)
                lf = jnp.where(valid, lf, 0.0)
            bc = carry_b + _lane_scan(lf, jnp.add, 0.0, reverse)
            a = li - bc
            m = jnp.maximum(carry_m, _lane_scan(a, jnp.maximum, NEG, reverse))
            m_last = jnp.broadcast_to(m[:, edge:edge + 1], (8, BLK))
            arow_ref[d * 8:(d + 1) * 8, cols] = a
            base = d * 40
            colt_ref[j, base:base + 8, :] = a
            colt_ref[j, base + 8:base + 16, :] = m
            colt_ref[j, base + 16:base + 24, :] = bc + m
            colt_ref[j, base + 24:base + 32, :] = carry_m
            colt_ref[j, base + 32:base + 40, :] = m_last
            carry_b = jnp.broadcast_to(bc[:, edge:edge + 1], (8, BLK))
            carry_m = m_last
    for j in range(NBLK):
        gcol_ref[j * BLK:(j + 1) * BLK, :] = colt_ref[j].T


def _scans(gt):
    return pl.pallas_call(
        _scan_kernel,
        grid=(BATCH,),
        in_specs=[pl.BlockSpec((GROWS, LP), lambda b: (0, b))],
        out_specs=[
            pl.BlockSpec((None, 16, LP), lambda b: (b, 0, 0)),
            pl.BlockSpec((None, LP, BLK), lambda b: (b, 0, 0)),
        ],
        out_shape=[
            jax.ShapeDtypeStruct((BATCH, 16, LP), F32),
            jax.ShapeDtypeStruct((BATCH, LP, BLK), F32),
        ],
        scratch_shapes=[pltpu.VMEM((NBLK, BLK, BLK), F32)],
        compiler_params=_cparams(("parallel",)),
        name="gate_scans",
    )(gt)


TC_COLS = 256
CONV_CH = 256


def _conv3_rows(s_ref, w, lo, n):
    x = s_ref[lo:lo + n, :]
    prev = jnp.concatenate([s_ref[lo - 1:lo + 7, :], pltpu.roll(x, 1, 0)[8:]], axis=0)
    nxt = jnp.concatenate([pltpu.roll(x, n - 1, 0)[:n - 8], s_ref[lo + n - 7:lo + n + 1, :]], axis=0)
    return prev * w[0:1] + x * w[1:2] + nxt * w[2:3]


def _qkconv_kernel(x_ref, w_ref, sc_ref, o_ref, s_ref):
    xo = 8 + N_META
    s_ref[0:8, :] = jnp.zeros((8, TC_COLS), F32)
    s_ref[xo + SEQ:xo + SEQ + 8, :] = jnp.zeros((8, TC_COLS), F32)
    s_ref[8:xo, :] = x_ref[SEQ:SEQ + N_META, :].astype(F32)
    for i in range(SEQ // CONV_CH):
        s_ref[xo + i * CONV_CH:xo + (i + 1) * CONV_CH, :] = x_ref[i * CONV_CH:(i + 1) * CONV_CH, :].astype(F32)
    w = 0.5 * w_ref[...]
    sc = sc_ref[...]

    def act(yh):
        return (_silu_of_half(yh) * sc).astype(o_ref.dtype)

    for i in range(SEQ // CONV_CH):
        o_ref[i * CONV_CH:(i + 1) * CONV_CH, :] = act(_conv3_rows(s_ref, w, xo + i * CONV_CH, CONV_CH))
    o_ref[SEQ:SEQ + N_META, :] = act(_conv3_rows(s_ref, w, 8, N_META))
    o_ref[SEQ + N_META:LP, :] = jnp.zeros((LP - SEQ - N_META, TC_COLS), o_ref.dtype)


def _qkconv(proj3, w, scale):
    return pl.pallas_call(
        _qkconv_kernel,
        grid=(BATCH, 2 * M_WIDTH // TC_COLS),
        in_specs=[
            pl.BlockSpec((None, LP, TC_COLS), lambda b, c: (b, 0, C_MQK // TC_COLS + c)),
            pl.BlockSpec((3, TC_COLS), lambda b, c: (0, c)),
            pl.BlockSpec((1, TC_COLS), lambda b, c: (0, c)),
        ],
        out_specs=pl.BlockSpec((None, LP, TC_COLS), lambda b, c: (b, 0, c)),
        out_shape=jax.ShapeDtypeStruct((BATCH, LP, 2 * M_WIDTH), BF16),
        scratch_shapes=[pltpu.VMEM((SEQ + N_META + 16, TC_COLS), F32)],
        compiler_params=_cparams(("parallel", "parallel")),
        name="qk_conv",
    )(proj3, w, scale)


def _mlstm_kernel(qf, kf, vf, gcf, arf, qb, kb, vb, gcb, arb, hf_ref, hb_ref, c_ref):
    c = pl.program_id(1)

    @pl.when(c == 0)
    def _():
        c_ref[...] = jnp.zeros_like(c_ref)

    row = lax.broadcasted_iota(jnp.int32, (BLK, BLK), 0)
    col = lax.broadcasted_iota(jnp.int32, (BLK, BLK), 1)
    ones_b = jnp.ones((BLK, 128), BF16)
    ones_f = jnp.ones((BLK, 128), F32)
    dirs = ((qf, kf, vf, gcf, arf, hf_ref), (qb, kb, vb, gcb, arb, hb_ref))
    for d, (q_ref, k_ref, v_ref, gc_ref, ar_ref, o_ref) in enumerate(dirs):
        mask = (col <= row) if d == 0 else (col >= row)
        gc = gc_ref[...]
        for h in range(M_HEADS):
            st = d * M_HEADS + h
            def colq(qi):
                j = d * 40 + qi * 8 + h
                return gc[:, j:j + 1]
            a_col, m_col, e_col, mp_col, ml_col = (colq(i) for i in range(5))
            a_row = ar_ref[d * 8 + h:d * 8 + h + 1, :]
            hs = slice(h * M_HEAD_DIM, (h + 1) * M_HEAD_DIM)
            q = q_ref[:, hs]
            k = k_ref[:, hs]
            v = v_ref[:, hs]
            s = lax.dot_general(q, k, (((1,), (1,)), ((), ())), preferred_element_type=F32)
            p = jnp.where(mask, jnp.exp(a_row - m_col), 0.0)
            sp = (s * p).astype(BF16)
            a = jnp.exp(mp_col - m_col)
            cn = c_ref[st]
            tot = (jnp.dot(sp, jnp.concatenate([v, ones_b], axis=1), preferred_element_type=F32)
                   + a * jnp.dot(q, cn.astype(BF16), preferred_element_type=F32))
            r = 1.0 / jnp.maximum(jnp.abs(tot[:, M_HEAD_DIM:]), jnp.exp(-e_col))
            o_ref[:, hs] = (tot[:, :M_HEAD_DIM] * jnp.concatenate([r, r], axis=1)).astype(o_ref.dtype)
            w = jnp.exp(a_col - ml_col)
            dec = jnp.exp(mp_col[0:1, :] - ml_col[0:1, :])
            wvn = (w * jnp.concatenate([v.astype(F32), ones_f], axis=1)).astype(BF16)
            c_ref[st] = dec * cn + lax.dot_general(
                k, wvn, (((0,), (0,)), ((), ())), preferred_element_type=F32)


def _fwd_blk(c):
    return (c + NBLK - 1) % NBLK


def _bwd_blk(c):
    return (2 * NBLK - 2 - c) % NBLK


def _mlstm(qk3, proj3, gcol, arow):
    def specs(blk):
        return [
            pl.BlockSpec((None, BLK, M_WIDTH), lambda b, c: (b, blk(c), 0)),
            pl.BlockSpec((None, BLK, M_WIDTH), lambda b, c: (b, blk(c), 1)),
            pl.BlockSpec((None, BLK, M_WIDTH), lambda b, c: (b, blk(c), C_MV // M_WIDTH)),
            pl.BlockSpec((None, BLK, BLK), lambda b, c: (b, blk(c), 0)),
            pl.BlockSpec((None, 16, BLK), lambda b, c: (b, 0, blk(c))),
        ]
    out_sd = jax.ShapeDtypeStruct((BATCH, LP, M_WIDTH), BF16)
    return pl.pallas_call(
        _mlstm_kernel,
        grid=(BATCH, NBLK),
        in_specs=specs(_fwd_blk) + specs(_bwd_blk),
        out_specs=[
            pl.BlockSpec((None, BLK, M_WIDTH), lambda b, c: (b, _fwd_blk(c), 0)),
            pl.BlockSpec((None, BLK, M_WIDTH), lambda b, c: (b, _bwd_blk(c), 0)),
        ],
        out_shape=[out_sd, out_sd],
        scratch_shapes=[pltpu.VMEM((2 * M_HEADS, M_HEAD_DIM, M_HEAD_DIM + 128), F32)],
        compiler_params=_cparams(("parallel", "arbitrary")),
        name="mlstm",
    )(qk3, qk3, proj3, gcol, arow, qk3, qk3, proj3, gcol, arow)


RQ = 128
NGRP = LP // RQ
EXP_CH = 512
ATTN_NBUF = 2


def _attn_kernel(lam_ref, q_ref, k_ref, v_ref, g_ref, o_ref, *scratch):
    s_refs, p_refs, l_refs, m_refs = (scratch[i * ATTN_NBUF:(i + 1) * ATTN_NBUF] for i in range(4))
    lq = lam_ref[...]
    la = jnp.sum(lq[0:1] * lq[1:2], axis=1, keepdims=True)
    lb = jnp.sum(lq[2:3] * lq[3:4], axis=1, keepdims=True)
    lam = jnp.exp(la) - jnp.exp(lb) + LAM_INIT

    colid = lax.broadcasted_iota(jnp.int32, (1, BLK), 1)
    o_ref[SEQ + N_META:, :] = jnp.zeros((LP - SEQ - N_META, D_V_DIM), o_ref.dtype)

    def group_rows(static_g):
        return N_META if static_g == NGRP - 1 else RQ

    def rows(g, rq):
        return pl.ds(pl.multiple_of(g * RQ, RQ), rq)

    def stage_scores(g, rq, s_ref, m_ref):
        qg = q_ref[rows(g, rq), :]
        lane = lax.broadcasted_iota(jnp.int32, (rq, 128), 1)
        zero = jnp.zeros_like(qg)
        q2 = jnp.concatenate([jnp.where(lane < D_QK_DIM, qg, zero), jnp.where(lane >= D_QK_DIM, qg, zero)], axis=0)
        s = lax.dot_general(q2, k_ref[...], (((1,), (1,)), ((), ())), preferred_element_type=F32)
        tail = jnp.where(colid < N_META, s[:, LP - BLK:], -jnp.inf)
        s_ref[:2 * rq, :LP - BLK] = s[:, :LP - BLK]
        s_ref[:2 * rq, LP - BLK:] = tail
        m = jnp.maximum(jnp.max(s[:, :LP - BLK], axis=1, keepdims=True), jnp.max(tail, axis=1, keepdims=True))
        m_ref[:2 * rq, :] = jnp.broadcast_to(m, (2 * rq, 128))

    def stage_exp(rq, s_ref, m_ref, p_ref, l_ref):
        m = m_ref[:2 * rq, 0:1]
        part = jnp.zeros((2 * rq, 128), F32)
        for c0 in range(0, LP, EXP_CH):
            c1 = min(c0 + EXP_CH, LP)
            pc = jnp.exp2(s_ref[:2 * rq, c0:c1] - m)
            for j in range(0, c1 - c0, 128):
                part = part + pc[:, j:j + 128]
            p_ref[:2 * rq, c0:c1] = pc.astype(BF16)
        l_ref[:2 * rq, :] = jnp.broadcast_to(jnp.sum(part, axis=1, keepdims=True), (2 * rq, 128))

    def stage_pv(g, rq, p_ref, l_ref):
        l = l_ref[:2 * rq, 0:1]
        ratio = (lam * l[:rq] / l[rq:]).astype(BF16)
        a = p_ref[:rq, :] - ratio * p_ref[rq:2 * rq, :]
        o = jnp.dot(a, v_ref[...], preferred_element_type=F32) / l[:rq]
        r = lax.rsqrt(jnp.mean(o * o, axis=-1, keepdims=True) + EPS)
        o_ref[rows(g, rq), :] = (((o * r) * g_ref[...]) * (1.0 - LAM_INIT)).astype(o_ref.dtype)

    nb = len(s_refs)
    sbuf = tuple(zip(s_refs, m_refs))
    pbuf = tuple(zip(p_refs, l_refs))

    def tick(t, static):
        if static < NGRP:
            stage_scores(t, group_rows(static), *sbuf[static % nb])
        if 1 <= static <= NGRP:
            stage_exp(group_rows(static - 1), *sbuf[(static - 1) % nb], *pbuf[(static - 1) % nb])
        if 2 <= static <= NGRP + 1:
            stage_pv(t - 2, group_rows(static - 2), *pbuf[(static - 2) % nb])

    tick(0, 0)
    tick(1, 1)
    n_rounds = (NGRP - 2) // nb

    def body(i, carry):
        t = nb * i + 2
        for j in range(nb):
            tick(t + j, 2 + j)
        return carry

    lax.fori_loop(0, n_rounds, body, 0)
    for t in range(2 + nb * n_rounds, NGRP + 2):
        tick(t, t)


def _attention(lam4, proj3, subln_g):
    return pl.pallas_call(
        _attn_kernel,
        grid=(BATCH, D_HEADS),
        in_specs=[
            pl.BlockSpec((4, D_QK_DIM), lambda b, h: (0, 0)),
            pl.BlockSpec((None, LP, 128), lambda b, h: (b, 0, C_DQ // 128 + h)),
            pl.BlockSpec((None, LP, 128), lambda b, h: (b, 0, C_DK // 128 + h)),
            pl.BlockSpec((None, LP, 128), lambda b, h: (b, 0, C_DV // 128 + h)),
            pl.BlockSpec((1, D_V_DIM), lambda b, h: (0, 0)),
        ],
        out_specs=pl.BlockSpec((None, LP, D_V_DIM), lambda b, h: (b, 0, h)),
        out_shape=jax.ShapeDtypeStruct((BATCH, LP, D_WIDTH), BF16),
        scratch_shapes=(
            [pltpu.VMEM((2 * RQ, LP), F32)] * ATTN_NBUF
            + [pltpu.VMEM((2 * RQ, LP), BF16)] * ATTN_NBUF
            + [pltpu.VMEM((2 * RQ, 128), F32)] * ATTN_NBUF
            + [pltpu.VMEM((2 * RQ, 128), F32)] * ATTN_NBUF
        ),
        compiler_params=_cparams(("parallel", "parallel")),
        name="diff_attn",
    )(lam4, proj3, proj3, proj3, subln_g)


TM4 = 1056
TN4 = 1024
LN_CH = 96


def _merge_kernel(hf_ref, hb_ref, mo_ref, ng_ref, bo_ref, wa_ref, wb_ref, gm_ref, gd_ref, o_ref, a_ref):
    n = pl.program_id(1)

    @pl.when(n == 0)
    def _():
        def body(i, carry):
            rows = pl.ds(pl.multiple_of(i * LN_CH, LN_CH), LN_CH)
            for h in range(M_HEADS):
                hs = slice(h * M_HEAD_DIM, (h + 1) * M_HEAD_DIM)
                x = hf_ref[rows, hs].astype(F32) + hb_ref[rows, hs].astype(F32)
                mu = jnp.mean(x, axis=-1, keepdims=True)
                xc = x - mu
                var = jnp.mean(xc * xc, axis=-1, keepdims=True)
                hn = (xc * lax.rsqrt(var + EPS)) * ng_ref[:, hs]
                a_ref[rows, hs] = (_twice_sigmoid_of_half(mo_ref[rows, hs].astype(F32)) * hn).astype(BF16)
            return carry
        lax.fori_loop(0, TM4 // LN_CH, body, 0)

    ya = jnp.dot(a_ref[...], wa_ref[...], preferred_element_type=F32)
    yb = jnp.dot(bo_ref[...], wb_ref[...], preferred_element_type=F32)
    o_ref[...] = (_twice_sigmoid_of_half(gm_ref[...].astype(F32)) * ya
                  + _twice_sigmoid_of_half(gd_ref[...].astype(F32)) * yb).astype(o_ref.dtype)


def _merge(hf, hb, proj, norm_g, b_out, wa, wb):
    return pl.pallas_call(
        _merge_kernel,
        grid=(R // TM4, D_MODEL // TN4),
        in_specs=[
            pl.BlockSpec((TM4, M_WIDTH), lambda m, n: (m, 0)),
            pl.BlockSpec((TM4, M_WIDTH), lambda m, n: (m, 0)),
            pl.BlockSpec((TM4, M_WIDTH), lambda m, n: (m, C_MO // M_WIDTH)),
            pl.BlockSpec((1, M_WIDTH), lambda m, n: (0, 0)),
            pl.BlockSpec((TM4, D_WIDTH), lambda m, n: (m, 0)),
            pl.BlockSpec((M_WIDTH, TN4), lambda m, n: (0, n)),
            pl.BlockSpec((D_WIDTH, TN4), lambda m, n: (0, n)),
            pl.BlockSpec((TM4, TN4), lambda m, n: (m, C_GM // TN4 + n)),
            pl.BlockSpec((TM4, TN4), lambda m, n: (m, C_GD // TN4 + n)),
        ],
        out_specs=pl.BlockSpec((TM4, TN4), lambda m, n: (m, n)),
        out_shape=jax.ShapeDtypeStruct((R, D_MODEL), BF16),
        scratch_shapes=[pltpu.VMEM((TM4, M_WIDTH), BF16)],
        compiler_params=_cparams(("parallel", "arbitrary")),
        name="merge",
    )(hf, hb, proj, norm_g, b_out, wa, wb, proj, proj)


def _outproj_kernel(x_ref, t_ref, mg_ref, w_ref, g2_ref, h1_ref, u2_ref):
    def run(h0_ref):
        h1 = h0_ref[...] + jnp.dot(mg_ref[...], w_ref[...], preferred_element_type=F32)
        h1_ref[...] = h1
        r = lax.rsqrt(jnp.mean(h1 * h1, axis=-1, keepdims=True) + EPS)
        u2_ref[...] = ((h1 * r) * g2_ref[...]).astype(BF16)

    pl.when(pl.program_id(1) < NT0 - 1)(lambda: run(x_ref))
    pl.when(pl.program_id(1) == NT0 - 1)(lambda: run(t_ref))


def _outproj(x, tail, merged3, w, g2):
    row_spec = pl.BlockSpec((None, TM0, D_MODEL), lambda b, j: (b, j, 0))
    return pl.pallas_call(
        _outproj_kernel,
        grid=(BATCH, NT0),
        in_specs=_x_tile_specs() + [
            row_spec,
            pl.BlockSpec((D_MODEL, D_MODEL), lambda b, j: (0, 0)),
            pl.BlockSpec((1, D_MODEL), lambda b, j: (0, 0)),
        ],
        out_specs=[row_spec, row_spec],
        out_shape=[
            jax.ShapeDtypeStruct((BATCH, LP, D_MODEL), F32),
            jax.ShapeDtypeStruct((BATCH, LP, D_MODEL), BF16),
        ],
        compiler_params=_cparams(("parallel", "parallel")),
        name="outproj",
    )(x, tail, merged3, w, g2)


TMU = 1024
TNU = 512
HALO = 16
NNU = FFN_DIM // TNU


def _ffn_up_kernel(x_ref, xp_ref, xn_ref, wg_ref, wv_ref, cg_ref, cv_ref, o_ref, xs_ref, sg_ref, sv_ref):
    @pl.when(pl.program_id(2) == 0)
    def _():
        last_tile = pl.program_id(1) == SEQ // TMU - 1
        xs_ref[0:HALO, :] = xp_ref[...]
        xs_ref[HALO:HALO + TMU, :] = x_ref[...]
        xn = xn_ref[...]
        xs_ref[HALO + TMU:, :] = jnp.where(last_tile, jnp.zeros_like(xn), xn)

    def branch(w_ref, c, s_ref):
        s_ref[...] = jnp.dot(xs_ref[...], w_ref[...].astype(BF16), preferred_element_type=F32)
        return _conv3_rows(s_ref, c, HALO, TMU)

    gate_half = branch(wg_ref, 0.5 * cg_ref[...], sg_ref)
    val = branch(wv_ref, cv_ref[...], sv_ref)
    o_ref[...] = (_silu_of_half(gate_half) * val).astype(o_ref.dtype)


def _ffn_up(u2_3, w_up, conv_w):
    hb = TMU // HALO
    meta_hblk = SEQ // HALO
    return pl.pallas_call(
        _ffn_up_kernel,
        grid=(BATCH, SEQ // TMU, NNU),
        in_specs=[
            pl.BlockSpec((None, TMU, D_MODEL), lambda b, i, n: (b, i, 0)),
            pl.BlockSpec((None, HALO, D_MODEL), lambda b, i, n: (b, jnp.where(i == 0, meta_hblk, i * hb - 1), 0)),
            pl.BlockSpec((None, HALO, D_MODEL), lambda b, i, n: (b, (i + 1) * hb, 0)),
            pl.BlockSpec((D_MODEL, TNU), lambda b, i, n: (0, n)),
            pl.BlockSpec((D_MODEL, TNU), lambda b, i, n: (0, NNU + n)),
            pl.BlockSpec((3, TNU), lambda b, i, n: (0, n)),
            pl.BlockSpec((3, TNU), lambda b, i, n: (0, NNU + n)),
        ],
        out_specs=pl.BlockSpec((None, TMU, TNU), lambda b, i, n: (b, i, n)),
        out_shape=jax.ShapeDtypeStruct((BATCH, SEQ, FFN_DIM), BF16),
        scratch_shapes=[
            pltpu.VMEM((TMU + 2 * HALO, D_MODEL), BF16),
            pltpu.VMEM((TMU + 2 * HALO, TNU), F32),
            pltpu.VMEM((TMU + 2 * HALO, TNU), F32),
        ],
        compiler_params=_cparams(("parallel", "parallel", "arbitrary")),
        name="ffn_up",
    )(u2_3, u2_3, u2_3, w_up, w_up, conv_w, conv_w)


TM7 = 1024
TK7 = 512
NK7 = FFN_DIM // TK7


def _down_kernel(a_ref, wd_ref, h1_ref, gf_ref, o_ref, acc_ref):
    k = pl.program_id(2)

    def part():
        return jnp.dot(a_ref[...], wd_ref[...].astype(BF16), preferred_element_type=F32)

    @pl.when(k == 0)
    def _():
        acc_ref[...] = part()

    @pl.when(jnp.logical_and(k > 0, k < NK7 - 1))
    def _():
        acc_ref[...] += part()

    @pl.when(k == NK7 - 1)
    def _():
        h2 = h1_ref[...] + (acc_ref[...] + part())
        r = lax.rsqrt(jnp.mean(h2 * h2, axis=-1, keepdims=True) + EPS)
        o_ref[...] = (h2 * r) * gf_ref[...]


def _ffn_down(act, w_down, h1_3, gf):
    return pl.pallas_call(
        _down_kernel,
        grid=(BATCH, SEQ // TM7, NK7),
        in_specs=[
            pl.BlockSpec((None, TM7, TK7), lambda b, i, k: (b, i, k)),
            pl.BlockSpec((TK7, D_MODEL), lambda b, i, k: (k, 0)),
            pl.BlockSpec((None, TM7, D_MODEL), lambda b, i, k: (b, i, 0)),
            pl.BlockSpec((1, D_MODEL), lambda b, i, k: (0, 0)),
        ],
        out_specs=pl.BlockSpec((None, TM7, D_MODEL), lambda b, i, k: (b, i, 0)),
        out_shape=jax.ShapeDtypeStruct((BATCH, SEQ, D_MODEL), F32),
        scratch_shapes=[pltpu.VMEM((TM7, D_MODEL), F32)],
        compiler_params=pltpu.CompilerParams(
            dimension_semantics=("parallel", "parallel", "arbitrary"),
            vmem_limit_bytes=V7X_VMEM_BYTES - 4 * 1024 * 1024),
        name="ffn_down",
    )(act, w_down, h1_3, gf)


def kernel(x, meta_tokens, norm1_g, w_in, mlstm_conv_w, mlstm_gate_bias, mlstm_norm_g, lambda_q1, lambda_k1,
           lambda_q2, lambda_k2, diff_subln_g, w_branch_m, w_branch_d, w_out, norm2_g, w_up, ffn_conv_w,
           w_down, norm_f_g):
    assert x.shape == (BATCH, SEQ, D_MODEL) and w_in.shape[0] == 1
    tail = _tail_tile(x, meta_tokens)

    wit = w_in[0].T
    halved = np.ones((wit.shape[0], 1), np.float32)
    for c0, width in ((C_MO, M_WIDTH), (C_GM + GROWS, D_MODEL), (C_GD + GROWS, D_MODEL)):
        halved[c0:c0 + width] = 0.5
    w_pa = w_pb = (wit * halved).astype(BF16)
    gbias = mlstm_gate_bias[0].reshape(GROWS, 1).astype(F32)

    u, gt = _prenorm(x, tail, norm1_g[0].reshape(1, D_MODEL), wit, gbias)
    proj = _inproj(u.reshape(R, D_MODEL), w_pa, w_pb, _rope_tables())
    proj3 = proj.reshape(BATCH, LP, PROJ_W)

    arow, gcol = _scans(gt)

    kscale = jnp.concatenate([jnp.ones((1, M_WIDTH), F32), jnp.full((1, M_WIDTH), M_HEAD_DIM ** -0.5, F32)], axis=1)
    qk3 = _qkconv(proj3, mlstm_conv_w[0], kscale)
    hf, hb = _mlstm(qk3, proj3, gcol, arow)

    lam4 = jnp.stack([lambda_q1[0], lambda_k1[0], lambda_q2[0], lambda_k2[0]]).astype(F32)
    b_out = _attention(lam4, proj3, diff_subln_g[0].reshape(1, D_V_DIM))

    merged = _merge(hf.reshape(R, M_WIDTH), hb.reshape(R, M_WIDTH), proj,
                    (0.5 * mlstm_norm_g[0]).reshape(1, M_WIDTH), b_out.reshape(R, D_WIDTH),
                    (0.5 * w_branch_m[0]).astype(BF16), (0.5 * w_branch_d[0]).astype(BF16))
    h1, u2 = _outproj(x, tail, merged.reshape(BATCH, LP, D_MODEL), w_out[0].astype(BF16),
                      norm2_g[0].reshape(1, D_MODEL))

    act = _ffn_up(u2, w_up[0], ffn_conv_w[0])
    return _ffn_down(act, w_down[0], h1, norm_f_g.reshape(1, D_MODEL))
```
